```python
import math
import jax, jax.numpy as jnp
from jax import lax
import numpy as np

D_MODEL = 1024
BATCH = 8
SEQ = 2048
DEPTH = 1
DEC_BATCH = 128
DEC_SEQ = 4
PAST_LEN = 16384
PAGE_SIZE = 128

SSM_EXPAND = 2
SSM_D_INNER = SSM_EXPAND * D_MODEL
SSM_HEAD_DIM = 64
SSM_HEADS = SSM_D_INNER // SSM_HEAD_DIM
SSM_GROUPS = 4
SSM_STATE = 128
SSM_CONV = 4
SSM_CONV_DIM = SSM_D_INNER + 2 * SSM_GROUPS * SSM_STATE
DT_MIN = 0.001
DT_MAX = 0.1
RET_HEADS = 4
RET_QK_DIM = D_MODEL // RET_HEADS
RET_V_DIM = 2 * RET_QK_DIM
RET_QK = RET_HEADS * RET_QK_DIM
RET_V = RET_HEADS * RET_V_DIM
ROPE_BASE = 10000.0
D_FF = 2816
CHUNK = 128
NORM_EPS = 1e-6
GATED_NORM_EPS = 1e-5

IN_SPLITS = (SSM_D_INNER, SSM_CONV_DIM, SSM_HEADS, RET_QK, RET_QK, RET_V, RET_V, D_MODEL, D_MODEL)
IN_DIM = sum(IN_SPLITS)
IN_OFFSETS = [int(v) for v in np.cumsum(IN_SPLITS)[:-1]]

kernel_name = 'hybrid_ssd_retention_macaron_step'


def rms_norm(x, g, eps=NORM_EPS):
    xf = x.astype(jnp.float32)
    y = xf * lax.rsqrt(jnp.mean(xf * xf, axis=-1, keepdims=True) + eps)
    return (y * g.astype(jnp.float32)).astype(x.dtype)


def swiglu(x, w1, w3, w2):
    return (jax.nn.silu(x @ w1) * (x @ w3)) @ w2


def causal_conv(xbc, buf, w, b):
    L = xbc.shape[1]
    full = jnp.concatenate([buf, xbc], axis=1)
    out = b + full[:, 0:L] * w[0]
    for k in range(1, SSM_CONV):
        out = out + full[:, k:k + L] * w[k]
    return jax.nn.silu(out), full[:, L:]


def to_chunks(t, nc, q):
    return jnp.moveaxis(t.reshape((t.shape[0], nc, q) + t.shape[2:]), 1, 0)


def ssd_scan(x, dt, a, bm, cm, h0):
    Bsz, L = x.shape[:2]
    q = math.gcd(L, CHUNK)
    nc = L // q
    hg = SSM_HEADS // SSM_GROUPS
    xs = to_chunks(x.reshape(Bsz, L, SSM_GROUPS, hg, SSM_HEAD_DIM), nc, q)
    dts = to_chunks(dt.reshape(Bsz, L, SSM_GROUPS, hg), nc, q)
    bs = to_chunks(bm, nc, q)
    cs = to_chunks(cm, nc, q)
    a_g = a.reshape(SSM_GROUPS, hg)
    causal = jnp.tril(jnp.ones((q, q), dtype=bool))[None, :, :, None, None]

    def step(h, inp):
        xc, dtc, bc, cc = inp
        la = jnp.cumsum(dtc * a_g, axis=1)
        seg = la[:, :, None] - la[:, None, :]
        decay = jnp.exp(jnp.where(causal, seg, -jnp.inf))
        cb = jnp.einsum('btgn,bsgn->btsg', cc, bc)
        w = cb[..., None] * decay * dtc[:, None]
        y = jnp.einsum('btsgh,bsghp->btghp', w, xc)
        y = y + jnp.einsum('btgn,bghpn->btghp', cc, h) * jnp.exp(la)[..., None]
        tail = jnp.exp(la[:, -1:] - la) * dtc
        h = h * jnp.exp(la[:, -1])[..., None, None] + jnp.einsum('bsgn,bsgh,bsghp->bghpn', bc, tail, xc)
        return h, y

    h0g = h0.reshape(Bsz, SSM_GROUPS, hg, SSM_HEAD_DIM, SSM_STATE)
    h, ys = lax.scan(step, h0g, (xs, dts, bs, cs))
    y = jnp.moveaxis(ys, 0, 1).reshape(Bsz, L, SSM_HEADS, SSM_HEAD_DIM)
    return y, h.reshape(Bsz, SSM_HEADS, SSM_HEAD_DIM, SSM_STATE)


def ret_log_decay():
    return jnp.log1p(-jnp.exp2(-5.0 - jnp.arange(RET_HEADS, dtype=jnp.float32)))


def retention_scan(q, k, v, h0):
    Bsz, L = q.shape[:2]
    c = math.gcd(L, CHUNK)
    nc = L // c
    lg = ret_log_decay()
    idx = jnp.arange(c, dtype=jnp.float32)
    rel = idx[:, None] - idx[None, :]
    dmat = jnp.where((rel >= 0)[..., None], jnp.exp(jnp.maximum(rel, 0.0)[..., None] * lg), 0.0)
    dmat = dmat.transpose(2, 0, 1)
    q_dec = jnp.exp((idx + 1.0)[:, None] * lg)[:, :, None]
    k_dec = jnp.exp((c - 1.0 - idx)[:, None] * lg)[:, :, None]
    chunk_dec = jnp.exp(c * lg)[:, None, None]

    def step(h, inp):
        qc, kc, vc = inp
        s = jnp.einsum('bthd,bshd->bhts', qc, kc) * dmat
        y = jnp.einsum('bhts,bshe->bthe', s, vc)
        y = y + jnp.einsum('bthd,bhde->bthe', qc * q_dec, h)
        h = h * chunk_dec + jnp.einsum('bshd,bshe->bhde', kc * k_dec, vc)
        return h, y

    h, ys = lax.scan(step, h0, (to_chunks(q, nc, c), to_chunks(k, nc, c), to_chunks(v, nc, c)))
    y = jnp.moveaxis(ys, 0, 1).reshape(Bsz, L, RET_HEADS, RET_V_DIM)
    return y, h


def rotary(t, pos):
    half = RET_QK_DIM // 2
    inv = ROPE_BASE ** (-jnp.arange(half, dtype=jnp.float32) / half)
    ang = pos[:, None] * inv
    cos = jnp.cos(ang)[None, :, None]
    sin = jnp.sin(ang)[None, :, None]
    t1, t2 = t[..., :half], t[..., half:]
    return jnp.concatenate([t1 * cos - t2 * sin, t1 * sin + t2 * cos], axis=-1)


def trunk_layer(x, pos0, ssm_h0, conv_buf, ret_h0, p):
    Bsz, L, _ = x.shape
    f32 = jnp.float32
    h = x + 0.5 * swiglu(rms_norm(x, p['norm_ffn1']), p['ffn1_w1'], p['ffn1_w3'], p['ffn1_w2'])
    u = rms_norm(h, p['norm_mix'])
    proj = u @ p['w_in']
    z, xbc, dt_raw, rq, rk, rv, rg, ga, gb = jnp.split(proj, IN_OFFSETS, axis=-1)
    xbc_c, conv_new = causal_conv(xbc, conv_buf.astype(xbc.dtype), p['conv_w'], p['conv_b'])
    xs, bm, cm = jnp.split(xbc_c.astype(f32), [SSM_D_INNER, SSM_D_INNER + SSM_GROUPS * SSM_STATE], axis=-1)
    dt = jax.nn.softplus(dt_raw.astype(f32) + p['dt_bias'].astype(f32))
    a = -jnp.exp(p['a_log'].astype(f32))
    xh = xs.reshape(Bsz, L, SSM_HEADS, SSM_HEAD_DIM)
    y_ssm, ssm_new = ssd_scan(xh, dt, a,
                              bm.reshape(Bsz, L, SSM_GROUPS, SSM_STATE),
                              cm.reshape(Bsz, L, SSM_GROUPS, SSM_STATE),
                              ssm_h0.astype(f32))
    y_ssm = y_ssm + p['ssm_d'].astype(f32)[:, None] * xh
    y_ssm = y_ssm.reshape(Bsz, L, SSM_D_INNER) * jax.nn.silu(z.astype(f32))
    y_ssm = rms_norm(y_ssm, p['ssm_norm'], GATED_NORM_EPS).astype(x.dtype)
    branch_ssm = y_ssm @ p['w_branch_ssm']
    pos = pos0 + jnp.arange(L, dtype=f32)
    q = rotary(rq.astype(f32).reshape(Bsz, L, RET_HEADS, RET_QK_DIM), pos)
    k = rotary(rk.astype(f32).reshape(Bsz, L, RET_HEADS, RET_QK_DIM), pos) * (RET_QK_DIM ** -0.5)
    v = rv.astype(f32).reshape(Bsz, L, RET_HEADS, RET_V_DIM)
    y_ret, ret_new = retention_scan(q, k, v, ret_h0.astype(f32))
    y_ret = rms_norm(y_ret, p['ret_norm'].reshape(RET_HEADS, RET_V_DIM), NORM_EPS).reshape(Bsz, L, RET_V)
    y_ret = (jax.nn.silu(rg.astype(f32)) * y_ret).astype(x.dtype)
    branch_ret = y_ret @ p['w_branch_ret']
    merged = jax.nn.sigmoid(ga) * branch_ssm + jax.nn.sigmoid(gb) * branch_ret
    h = h + merged @ p['w_out']
    h = h + 0.5 * swiglu(rms_norm(h, p['norm_ffn2']), p['ffn2_w1'], p['ffn2_w3'], p['ffn2_w2'])
    return h, ssm_new, conv_new, ret_new


def setup_inputs(seed: int = 0) -> dict:
    key = jax.random.key(seed)
    ks = iter(jax.random.split(key, 32))

    def nrm(shape, scale):
        return jax.random.normal(next(ks), shape, jnp.float32) * scale

    def gain(shape):
        return 1.0 + nrm(shape, 0.02)

    dt0 = jnp.exp(jax.random.uniform(next(ks), (DEPTH, SSM_HEADS), jnp.float32,
                                     minval=math.log(DT_MIN), maxval=math.log(DT_MAX)))
    dt_bias = dt0 + jnp.log(-jnp.expm1(-dt0))
    a_log = jnp.log(jax.random.uniform(next(ks), (DEPTH, SSM_HEADS), jnp.float32, minval=1.0, maxval=16.0))
    return {
        'x_prompt': nrm((BATCH, SEQ, D_MODEL), 1.0),
        'x_sample': nrm((DEC_BATCH, DEC_SEQ, D_MODEL), 1.0),
        'state_ssm': nrm((DEPTH, DEC_BATCH, SSM_HEADS, SSM_HEAD_DIM, SSM_STATE), 0.1),
        'state_conv': nrm((DEPTH, DEC_BATCH, SSM_CONV - 1, SSM_CONV_DIM), 1.0),
        'state_ret': nrm((DEPTH, DEC_BATCH, RET_HEADS, RET_QK_DIM, RET_V_DIM), 0.5),
        'norm_ffn1': gain((DEPTH, D_MODEL)),
        'ffn1_w1': nrm((DEPTH, D_MODEL, D_FF), D_MODEL ** -0.5),
        'ffn1_w3': nrm((DEPTH, D_MODEL, D_FF), D_MODEL ** -0.5),
        'ffn1_w2': nrm((DEPTH, D_FF, D_MODEL), D_FF ** -0.5),
        'norm_mix': gain((DEPTH, D_MODEL)),
        'w_in': nrm((DEPTH, D_MODEL, IN_DIM), D_MODEL ** -0.5),
        'conv_w': nrm((DEPTH, SSM_CONV, SSM_CONV_DIM), SSM_CONV ** -0.5),
        'conv_b': nrm((DEPTH, SSM_CONV_DIM), 0.02),
        'dt_bias': dt_bias,
        'a_log': a_log,
        'ssm_d': 1.0 + nrm((DEPTH, SSM_HEADS), 0.1),
        'ssm_norm': gain((DEPTH, SSM_D_INNER)),
        'ret_norm': gain((DEPTH, RET_V)),
        'w_branch_ssm': nrm((DEPTH, SSM_D_INNER, D_MODEL), SSM_D_INNER ** -0.5),
        'w_branch_ret': nrm((DEPTH, RET_V, D_MODEL), RET_V ** -0.5),
        'w_out': nrm((DEPTH, D_MODEL, D_MODEL), D_MODEL ** -0.5),
        'norm_ffn2': gain((DEPTH, D_MODEL)),
        'ffn2_w1': nrm((DEPTH, D_MODEL, D_FF), D_MODEL ** -0.5),
        'ffn2_w3': nrm((DEPTH, D_MODEL, D_FF), D_MODEL ** -0.5),
        'ffn2_w2': nrm((DEPTH, D_FF, D_MODEL), D_FF ** -0.5),
        'norm_final': gain((D_MODEL,)),
    }


def reference(x_prompt, x_sample, state_ssm, state_conv, state_ret,
              norm_ffn1, ffn1_w1, ffn1_w3, ffn1_w2, norm_mix, w_in, conv_w, conv_b,
              dt_bias, a_log, ssm_d, ssm_norm, ret_norm, w_branch_ssm, w_branch_ret, w_out,
              norm_ffn2, ffn2_w1, ffn2_w3, ffn2_w2, norm_final):
    bp = x_prompt.shape[0]
    hp, hs = x_prompt, x_sample
    ssm_p, conv_p, ret_p, ssm_s, conv_s, ret_s = [], [], [], [], [], []
    for l in range(DEPTH):
        p = dict(norm_ffn1=norm_ffn1[l], ffn1_w1=ffn1_w1[l], ffn1_w3=ffn1_w3[l], ffn1_w2=ffn1_w2[l],
                 norm_mix=norm_mix[l], w_in=w_in[l], conv_w=conv_w[l], conv_b=conv_b[l],
                 dt_bias=dt_bias[l], a_log=a_log[l], ssm_d=ssm_d[l], ssm_norm=ssm_norm[l],
                 ret_norm=ret_norm[l], w_branch_ssm=w_branch_ssm[l], w_branch_ret=w_branch_ret[l],
                 w_out=w_out[l], norm_ffn2=norm_ffn2[l], ffn2_w1=ffn2_w1[l], ffn2_w3=ffn2_w3[l],
                 ffn2_w2=ffn2_w2[l])
        hp, a1, b1, c1 = trunk_layer(
            hp, 0.0,
            jnp.zeros((bp, SSM_HEADS, SSM_HEAD_DIM, SSM_STATE), jnp.float32),
            jnp.zeros((bp, SSM_CONV - 1, SSM_CONV_DIM), x_prompt.dtype),
            jnp.zeros((bp, RET_HEADS, RET_QK_DIM, RET_V_DIM), jnp.float32), p)
        hs, a2, b2, c2 = trunk_layer(hs, float(PAST_LEN), state_ssm[l], state_conv[l], state_ret[l], p)
        ssm_p.append(a1); conv_p.append(b1); ret_p.append(c1)
        ssm_s.append(a2); conv_s.append(b2); ret_s.append(c2)
    y_prompt = rms_norm(hp, norm_final)
    y_sample = rms_norm(hs, norm_final)
    return (y_prompt, y_sample,
            jnp.stack(ssm_p).astype(state_ssm.dtype), jnp.stack(conv_p).astype(state_conv.dtype),
            jnp.stack(ret_p).astype(state_ret.dtype),
            jnp.stack(ssm_s).astype(state_ssm.dtype), jnp.stack(conv_s).astype(state_conv.dtype),
            jnp.stack(ret_s).astype(state_ret.dtype))
```

```python
import functools
import math

import jax
import jax.numpy as jnp
import numpy as np
from jax import lax
from jax.experimental import pallas as pl
from jax.experimental.pallas import tpu as pltpu

F32 = jnp.float32
BF16 = jnp.bfloat16

D_MODEL = 1024
D_FF = 2816
SSM_D_INNER = 2048
SSM_HEAD_DIM = 64
SSM_HEADS = 32
SSM_GROUPS = 4
SSM_STATE = 128
SSM_CONV = 4
SSM_BC = 2 * SSM_GROUPS * SSM_STATE
SSM_CONV_DIM = SSM_D_INNER + SSM_BC
RET_HEADS = 4
RET_QK_DIM = 256
RET_V_DIM = 512
RET_QK = RET_HEADS * RET_QK_DIM
RET_V = RET_HEADS * RET_V_DIM
ROPE_BASE = 10000.0
PAST_LEN = 16384
CHUNK = 128
NORM_EPS = 1e-6
GATED_NORM_EPS = 1e-5

V7X_SUBLANES = 8
V7X_LANES = 128
V7X_VMEM_BYTES = 64 * 1024 * 1024
VMEM_LIMIT = V7X_VMEM_BYTES - 8 * 1024 * 1024

PROJ_Z = 0
PROJ_X = 2048
PROJ_V = 4096
PROJ_RG = 6144
PROJ_BC = 8192
PROJ_Q = 9216
PROJ_K = 10240
PROJ_DIM = 11264
PROJ_TN = 1024
DT_PAD = V7X_LANES
LOG_DT_FLOOR = -1e30


def _rms(x, g, eps):
    return x * lax.rsqrt(jnp.mean(x * x, axis=-1, keepdims=True) + eps) * g


def _sigmoid(x):
    return 1.0 / (1.0 + jnp.exp(-x))


def _silu(x):
    return x * _sigmoid(x)


def _dot(a, b):
    return jnp.dot(a.astype(BF16), b.astype(BF16), preferred_element_type=F32)


def _dot_nt(a, b):
    return lax.dot_general(a.astype(BF16), b.astype(BF16), (((1,), (1,)), ((), ())), preferred_element_type=F32)


def _dot_tn(a, b):
    return lax.dot_general(a.astype(BF16), b.astype(BF16), (((0,), (0,)), ((), ())), preferred_element_type=F32)


def _dot_split(a, b_f32, passes):
    acc = None
    rem = b_f32
    for _ in range(passes):
        piece = rem.astype(BF16)
        term = jnp.dot(a, piece, preferred_element_type=F32)
        acc = term if acc is None else acc + term
        rem = rem - piece.astype(F32)
    return acc


def _dot_split_lhs(a_f32, b, passes):
    acc = None
    rem = a_f32
    for _ in range(passes):
        piece = rem.astype(BF16)
        term = jnp.dot(piece, b, preferred_element_type=F32)
        acc = term if acc is None else acc + term
        rem = rem - piece.astype(F32)
    return acc


def _transpose_split(eye, x_f32, passes):
    acc = None
    rem = x_f32
    for _ in range(passes):
        piece = rem.astype(BF16)
        term = lax.dot_general(eye, piece, (((1,), (1,)), ((), ())), preferred_element_type=F32)
        acc = term if acc is None else acc + term
        rem = rem - piece.astype(F32)
    return acc


def _softplus(x):
    return jnp.maximum(x, 0.0) + jnp.log1p(jnp.exp(-jnp.abs(x)))


def _const_spec(shape):
    nd = len(shape)
    return pl.BlockSpec(shape, lambda *_: (0,) * nd, pipeline_mode=pl.Buffered(1))


def _params(n_grid_dims):
    return pltpu.CompilerParams(dimension_semantics=("arbitrary",) * n_grid_dims, vmem_limit_bytes=VMEM_LIMIT)


def _ffn_kernel(x_ref, g_ref, w1_ref, w3_ref, w2_ref, gf_ref, o_ref, *, final_norm):
    x = x_ref[...]
    xn = _rms(x, g_ref[...], NORM_EPS).astype(BF16)
    a = jnp.dot(xn, w1_ref[...], preferred_element_type=F32)
    b = jnp.dot(xn, w3_ref[...], preferred_element_type=F32)
    gated = (_silu(a) * b).astype(BF16)
    h = x + 0.5 * jnp.dot(gated, w2_ref[...], preferred_element_type=F32)
    if final_norm:
        h = _rms(h, gf_ref[...], NORM_EPS)
    o_ref[...] = h


def _ffn(x, g, w1, w3, w2, gf, *, final_norm, tm):
    t = x.shape[0]
    assert t % tm == 0
    return pl.pallas_call(
        functools.partial(_ffn_kernel, final_norm=final_norm),
        grid=(t // tm,),
        in_specs=[
            pl.BlockSpec((tm, D_MODEL), lambda i: (i, 0)),
            _const_spec((1, D_MODEL)),
            _const_spec((D_MODEL, D_FF)),
            _const_spec((D_MODEL, D_FF)),
            _const_spec((D_FF, D_MODEL)),
            _const_spec((1, D_MODEL)),
        ],
        out_specs=pl.BlockSpec((tm, D_MODEL), lambda i: (i, 0)),
        out_shape=jax.ShapeDtypeStruct((t, D_MODEL), F32),
        compiler_params=_params(1),
        name="ffn_final" if final_norm else "ffn",
    )(x, g, w1, w3, w2, gf)


def _inproj_kernel(h_ref, g_ref, w_ref, wdt_ref, o_ref, odt_ref, u_scr):
    @pl.when(pl.program_id(1) == 0)
    def _():
        u = _rms(h_ref[...], g_ref[...], NORM_EPS).astype(BF16)
        u_scr[...] = u
        odt_ref[...] = jnp.dot(u, wdt_ref[...], preferred_element_type=F32)

    o_ref[...] = jnp.dot(u_scr[...], w_ref[...], preferred_element_type=F32)


def _inproj(h, g, w, wdt, *, tm):
    t = h.shape[0]
    assert t % tm == 0 and PROJ_DIM % PROJ_TN == 0
    return pl.pallas_call(
        _inproj_kernel,
        grid=(t // tm, PROJ_DIM // PROJ_TN),
        in_specs=[
            pl.BlockSpec((tm, D_MODEL), lambda i, j: (i, 0)),
            _const_spec((1, D_MODEL)),
            pl.BlockSpec((D_MODEL, PROJ_TN), lambda i, j: (0, j)),
            _const_spec((D_MODEL, DT_PAD)),
        ],
        out_specs=[
            pl.BlockSpec((tm, PROJ_TN), lambda i, j: (i, j)),
            pl.BlockSpec((tm, DT_PAD), lambda i, j: (i, 0)),
        ],
        out_shape=[
            jax.ShapeDtypeStruct((t, PROJ_DIM), F32),
            jax.ShapeDtypeStruct((t, DT_PAD), F32),
        ],
        scratch_shapes=[pltpu.VMEM((tm, D_MODEL), BF16)],
        compiler_params=_params(2),
        name="inproj",
    )(h, g, w, wdt)


def _rope_kernel(inv_ref, cos_ref, sin_ref, *, pos0):
    rows = cos_ref.shape[0]
    pos = pos0 + lax.broadcasted_iota(jnp.int32, (rows, V7X_LANES), 0).astype(F32)
    ang = pos * inv_ref[...]
    cos_ref[...] = jnp.cos(ang)
    sin_ref[...] = jnp.sin(ang)


def _rope_tables(inv, rows, pos0):
    return pl.pallas_call(
        functools.partial(_rope_kernel, pos0=pos0),
        out_shape=[jax.ShapeDtypeStruct((rows, V7X_LANES), F32)] * 2,
        name="rope",
    )(inv)


def _load_rows(ref, pad_scr, q):
    qin = ref.shape[1]
    if qin == q:
        return ref[0]
    pad_scr[...] = jnp.zeros_like(pad_scr)
    pad_scr[0:qin, :] = ref[0]
    return pad_scr[...]


def _ssd_kernel(z_ref, x_ref, bc_ref, dt_ref, h0_ref, conv0_ref, cw_ref, cb_ref, dtb_ref, alog_ref, dexp_ref,
                gn_ref, e_ref, y_ref, hout_ref, convout_ref, xp_scr, h_scr, zpad_scr, dtpad_scr, *, q, nv):
    c = pl.program_id(1)
    qin = x_ref.shape[1]
    hg = SSM_HEADS // SSM_GROUPS
    gw = hg * SSM_HEAD_DIM
    top = V7X_SUBLANES

    @pl.when(c == 0)
    def _init():
        h_scr[...] = h0_ref[0]
        xp_scr[...] = jnp.zeros_like(xp_scr)
        xp_scr[top - (SSM_CONV - 1):top, :] = conv0_ref[0]

    xp_scr[top:top + qin, 0:SSM_D_INNER] = x_ref[0]
    xp_scr[top:top + qin, SSM_D_INNER:SSM_CONV_DIM] = bc_ref[0]
    conv = cb_ref[...] + xp_scr[top - 3:top - 3 + q, :] * cw_ref[0:1, :]
    for k in range(1, SSM_CONV):
        conv = conv + xp_scr[top - 3 + k:top - 3 + k + q, :] * cw_ref[k:k + 1, :]
    convout_ref[0] = xp_scr[top + nv - (SSM_CONV - 1):top + nv, :]
    xp_scr[0:top, :] = xp_scr[q:q + top, :]
    xc = _silu(conv)
    xs = xc[:, 0:SSM_D_INNER]
    bm = xc[:, SSM_D_INNER:SSM_D_INNER + SSM_GROUPS * SSM_STATE]
    cm = xc[:, SSM_D_INNER + SSM_GROUPS * SSM_STATE:SSM_CONV_DIM]

    dt = _softplus(_load_rows(dt_ref, dtpad_scr, q) + dtb_ref[...])
    a = -jnp.exp(alog_ref[...])
    row = lax.broadcasted_iota(jnp.int32, (q, q), 0)
    col = lax.broadcasted_iota(jnp.int32, (q, q), 1)
    causal = row >= col
    la = _dot_split(jnp.where(causal, 1.0, 0.0).astype(BF16), dt * a, 3)
    lsd = la - jnp.maximum(jnp.log(dt), LOG_DT_FLOOR)
    eye_r = lax.broadcasted_iota(jnp.int32, (DT_PAD, DT_PAD), 0)
    eye_c = lax.broadcasted_iota(jnp.int32, (DT_PAD, DT_PAD), 1)
    eye = jnp.where(eye_r == eye_c, 1.0, 0.0).astype(BF16)
    la_t = _transpose_split(eye, la, 3)
    lsd_t = _transpose_split(eye, lsd, 3)
    la_last = la[nv - 1:nv, :]
    la_last_t = la_t[:, nv - 1:nv]

    lane = lax.broadcasted_iota(jnp.int32, (q, 2 * SSM_HEAD_DIM), 1)
    first_head = lane < SSM_HEAD_DIM
    y_parts = []
    for g in range(SSM_GROUPS):
        cg = cm[:, g * SSM_STATE:(g + 1) * SSM_STATE]
        bg = bm[:, g * SSM_STATE:(g + 1) * SSM_STATE]
        cbg = _dot_nt(cg, bg)
        for pair in range(hg // 2):
            h0 = g * hg + 2 * pair
            ws = []
            for h in (h0, h0 + 1):
                seg = la[:, h:h + 1] - lsd_t[h:h + 1, :]
                ws.append(cbg * jnp.exp(jnp.where(causal, seg, -jnp.inf)))
            xpair = xs[:, h0 * SSM_HEAD_DIM:(h0 + 2) * SSM_HEAD_DIM]
            rhs = jnp.concatenate([jnp.where(first_head, xpair, 0.0), jnp.where(first_head, 0.0, xpair)], axis=0)
            y_parts.append(_dot(jnp.concatenate(ws, axis=1), rhs))
    y = jnp.concatenate(y_parts, axis=1)

    e = e_ref[...]
    el = _dot_split_lhs(jnp.exp(la), e, 2)
    y_state = jnp.concatenate(
        [_dot_nt(cm[:, g * SSM_STATE:(g + 1) * SSM_STATE], h_scr[g * gw:(g + 1) * gw, :]) for g in range(SSM_GROUPS)],
        axis=1)
    y = y + y_state * el + xs * dexp_ref[...]
    y = y * _silu(_load_rows(z_ref, zpad_scr, q))
    y = _rms(y, gn_ref[...], GATED_NORM_EPS)
    y_ref[0] = y[0:qin, :].astype(y_ref.dtype)

    tail = jnp.exp(la_last - lsd)
    if nv < q:
        tail = jnp.where(lax.broadcasted_iota(jnp.int32, tail.shape, 0) < nv, tail, 0.0)
    xt = (xs * _dot_split_lhs(tail, e, 2)).astype(BF16)
    dec = jnp.broadcast_to(jnp.exp(la_last_t), (DT_PAD, SSM_STATE))
    for g in range(SSM_GROUPS):
        upd = _dot_tn(xt[:, g * gw:(g + 1) * gw], bm[:, g * SSM_STATE:(g + 1) * SSM_STATE])
        for hh in range(hg):
            h = g * hg + hh
            r0 = h * SSM_HEAD_DIM
            h_scr[r0:r0 + SSM_HEAD_DIM, :] = (
                h_scr[r0:r0 + SSM_HEAD_DIM, :] * dec[h:h + 1, :]
                + upd[hh * SSM_HEAD_DIM:(hh + 1) * SSM_HEAD_DIM, :])
    hout_ref[0] = h_scr[...]


def _ssd(proj3, dt3, h0, conv0, cw, cb, dtb, alog, dexp, gn, e, *, nseq, nc, q, nv):
    qin = proj3.shape[1]
    rows = SSM_HEADS * SSM_HEAD_DIM

    def chunk_spec(width, col):
        return pl.BlockSpec((1, qin, width), lambda s, c: (s * nc + c, 0, col // width))

    return pl.pallas_call(
        functools.partial(_ssd_kernel, q=q, nv=nv),
        grid=(nseq, nc),
        in_specs=[
            chunk_spec(SSM_D_INNER, PROJ_Z),
            chunk_spec(SSM_D_INNER, PROJ_X),
            chunk_spec(SSM_BC, PROJ_BC),
            pl.BlockSpec((1, qin, DT_PAD), lambda s, c: (s * nc + c, 0, 0)),
            pl.BlockSpec((1, rows, SSM_STATE), lambda s, c: (s, 0, 0)),
            pl.BlockSpec((1, SSM_CONV - 1, SSM_CONV_DIM), lambda s, c: (s, 0, 0)),
            _const_spec((SSM_CONV, SSM_CONV_DIM)),
            _const_spec((1, SSM_CONV_DIM)),
            _const_spec((1, DT_PAD)),
            _const_spec((1, DT_PAD)),
            _const_spec((1, SSM_D_INNER)),
            _const_spec((1, SSM_D_INNER)),
            _const_spec((DT_PAD, SSM_D_INNER)),
        ],
        out_specs=[
            pl.BlockSpec((1, qin, SSM_D_INNER), lambda s, c: (s * nc + c, 0, 0)),
            pl.BlockSpec((1, rows, SSM_STATE), lambda s, c: (s, 0, 0)),
            pl.BlockSpec((1, SSM_CONV - 1, SSM_CONV_DIM), lambda s, c: (s, 0, 0)),
        ],
        out_shape=[
            jax.ShapeDtypeStruct((nseq * nc, qin, SSM_D_INNER), BF16),
            jax.ShapeDtypeStruct((nseq, rows, SSM_STATE), F32),
            jax.ShapeDtypeStruct((nseq, SSM_CONV - 1, SSM_CONV_DIM), F32),
        ],
        scratch_shapes=[
            pltpu.VMEM((q + V7X_SUBLANES, SSM_CONV_DIM), F32),
            pltpu.VMEM((rows, SSM_STATE), F32),
            pltpu.VMEM((q, SSM_D_INNER), F32),
            pltpu.VMEM((q, DT_PAD), F32),
        ],
        compiler_params=_params(2),
        name="ssd",
    )(proj3, proj3, proj3, dt3, h0, conv0, cw, cb, dtb, alog, dexp, gn, e)


def _ret_log_decay(h):
    return float(np.log1p(-np.exp2(-5.0 - h)))


def _ret_kernel(q_ref, k_ref, v_ref, rg_ref, cos_ref, sin_ref, s0_ref, gn_ref, y_ref, sout_ref,
                s_scr, qpad_scr, kpad_scr, vpad_scr, gpad_scr, *, q, nv):
    c = pl.program_id(1)
    qin = q_ref.shape[1]
    half = RET_QK_DIM // 2

    @pl.when(c == 0)
    def _init():
        s_scr[...] = s0_ref[0]

    qq = _load_rows(q_ref, qpad_scr, q)
    kk = _load_rows(k_ref, kpad_scr, q)
    vv = _load_rows(v_ref, vpad_scr, q)
    rg = _load_rows(rg_ref, gpad_scr, q)
    cos = cos_ref[...]
    sin = sin_ref[...]
    row = lax.broadcasted_iota(jnp.int32, (q, q), 0)
    col = lax.broadcasted_iota(jnp.int32, (q, q), 1)
    rel = (row - col).astype(F32)
    t_idx = lax.broadcasted_iota(jnp.int32, (q, RET_QK_DIM), 0).astype(F32)
    valid = t_idx < nv

    def rot(t, h):
        t1 = t[:, h * RET_QK_DIM:h * RET_QK_DIM + half]
        t2 = t[:, h * RET_QK_DIM + half:(h + 1) * RET_QK_DIM]
        return jnp.concatenate([t1 * cos - t2 * sin, t1 * sin + t2 * cos], axis=1)

    y_parts = []
    for h in range(RET_HEADS):
        lg = _ret_log_decay(h)
        dmat = jnp.where(rel >= 0, jnp.exp(jnp.maximum(rel, 0.0) * lg), 0.0)
        q_dec = jnp.exp((t_idx + 1.0) * lg)
        k_dec = jnp.where(valid, jnp.exp((nv - 1.0 - t_idx) * lg), 0.0)
        qr = rot(qq, h)
        kr = rot(kk, h) * (RET_QK_DIM ** -0.5)
        vh = vv[:, h * RET_V_DIM:(h + 1) * RET_V_DIM].astype(BF16)
        sh = s_scr[h * RET_QK_DIM:(h + 1) * RET_QK_DIM, :]
        scores = _dot_nt(qr, kr) * dmat
        y = _dot(scores, vh) + _dot(qr * q_dec, sh)
        s_scr[h * RET_QK_DIM:(h + 1) * RET_QK_DIM, :] = sh * math.exp(nv * lg) + _dot_tn(kr * k_dec, vh)
        g = gn_ref[:, h * RET_V_DIM:(h + 1) * RET_V_DIM]
        y = _rms(y, g, NORM_EPS) * _silu(rg[:, h * RET_V_DIM:(h + 1) * RET_V_DIM])
        y_parts.append(y)
    y_ref[0] = jnp.concatenate(y_parts, axis=1)[0:qin, :].astype(y_ref.dtype)
    sout_ref[0] = s_scr[...]


def _ret(proj3, cos, sin, s0, gn, *, nseq, nc, q, nv):
    qin = proj3.shape[1]
    rows = RET_HEADS * RET_QK_DIM

    def chunk_spec(width, col):
        return pl.BlockSpec((1, qin, width), lambda s, c: (s * nc + c, 0, col // width))

    return pl.pallas_call(
        functools.partial(_ret_kernel, q=q, nv=nv),
        grid=(nseq, nc),
        in_specs=[
            chunk_spec(RET_QK, PROJ_Q),
            chunk_spec(RET_QK, PROJ_K),
            chunk_spec(RET_V, PROJ_V),
            chunk_spec(RET_V, PROJ_RG),
            pl.BlockSpec((q, V7X_LANES), lambda s, c: (c, 0)),
            pl.BlockSpec((q, V7X_LANES), lambda s, c: (c, 0)),
            pl.BlockSpec((1, rows, RET_V_DIM), lambda s, c: (s, 0, 0)),
            _const_spec((1, RET_V)),
        ],
        out_specs=[
            pl.BlockSpec((1, qin, RET_V), lambda s, c: (s * nc + c, 0, 0)),
            pl.BlockSpec((1, rows, RET_V_DIM), lambda s, c: (s, 0, 0)),
        ],
        out_shape=[
            jax.ShapeDtypeStruct((nseq * nc, qin, RET_V), BF16),
            jax.ShapeDtypeStruct((nseq, rows, RET_V_DIM), F32),
        ],
        scratch_shapes=[
            pltpu.VMEM((rows, RET_V_DIM), F32),
            pltpu.VMEM((q, RET_QK), F32),
            pltpu.VMEM((q, RET_QK), F32),
            pltpu.VMEM((q, RET_V), F32),
            pltpu.VMEM((q, RET_V), F32),
        ],
        compiler_params=_params(2),
        name="ret",
    )(proj3, proj3, proj3, proj3, cos, sin, s0, gn)


def _merge_kernel(h_ref, ys_ref, yr_ref, g_ref, wg_ref, wbs_ref, wbr_ref, wo_ref, o_ref):
    h = h_ref[...]
    u = _rms(h, g_ref[...], NORM_EPS).astype(BF16)
    gates = jnp.dot(u, wg_ref[...], preferred_element_type=F32)
    branch_ssm = jnp.dot(ys_ref[...], wbs_ref[...], preferred_element_type=F32)
    branch_ret = jnp.dot(yr_ref[...], wbr_ref[...], preferred_element_type=F32)
    merged = _sigmoid(gates[:, 0:D_MODEL]) * branch_ssm + _sigmoid(gates[:, D_MODEL:2 * D_MODEL]) * branch_ret
    o_ref[...] = h + jnp.dot(merged.astype(BF16), wo_ref[...], preferred_element_type=F32)


def _merge(h, ys, yr, g, wg, wbs, wbr, wo, *, tm):
    t = h.shape[0]
    assert t % tm == 0
    return pl.pallas_call(
        _merge_kernel,
        grid=(t // tm,),
        in_specs=[
            pl.BlockSpec((tm, D_MODEL), lambda i: (i, 0)),
            pl.BlockSpec((tm, SSM_D_INNER), lambda i: (i, 0)),
            pl.BlockSpec((tm, RET_V), lambda i: (i, 0)),
            _const_spec((1, D_MODEL)),
            _const_spec((D_MODEL, 2 * D_MODEL)),
            _const_spec((SSM_D_INNER, D_MODEL)),
            _const_spec((RET_V, D_MODEL)),
            _const_spec((D_MODEL, D_MODEL)),
        ],
        out_specs=pl.BlockSpec((tm, D_MODEL), lambda i: (i, 0)),
        out_shape=jax.ShapeDtypeStruct((t, D_MODEL), F32),
        compiler_params=_params(1),
        name="merge",
    )(h, ys, yr, g, wg, wbs, wbr, wo)


def _layer(x, pos0, ssm0, conv0, ret0, w, *, tm_ffn, tm_proj, q, final_gain):
    nseq, seq_len, _ = x.shape
    qin = math.gcd(seq_len, CHUNK)
    nc = seq_len // qin
    t = nseq * seq_len
    tm_ffn = min(tm_ffn, t)
    tm_proj = min(tm_proj, t)
    x2 = x.reshape(t, D_MODEL)

    h = _ffn(x2, w["norm_ffn1"], w["ffn1_w1"], w["ffn1_w3"], w["ffn1_w2"], final_gain, final_norm=False, tm=tm_ffn)
    proj, dtp = _inproj(h, w["norm_mix"], w["w_proj"], w["w_dt"], tm=tm_proj)
    proj3 = proj.reshape(nseq * nc, qin, PROJ_DIM)
    dt3 = dtp.reshape(nseq * nc, qin, DT_PAD)

    y_ssm, ssm_new, conv_new = _ssd(
        proj3, dt3, ssm0.reshape(nseq, SSM_HEADS * SSM_HEAD_DIM, SSM_STATE), conv0,
        w["conv_w"], w["conv_b"], w["dt_bias"], w["a_log"], w["d_exp"], w["ssm_norm"], w["head_expand"],
        nseq=nseq, nc=nc, q=q, nv=qin)

    cos, sin = _rope_tables(w["rope_inv"], nc * q, pos0)
    y_ret, ret_new = _ret(proj3, cos, sin, ret0.reshape(nseq, RET_HEADS * RET_QK_DIM, RET_V_DIM), w["ret_norm"],
                          nseq=nseq, nc=nc, q=q, nv=qin)

    h = _merge(h, y_ssm.reshape(t, SSM_D_INNER), y_ret.reshape(t, RET_V), w["norm_mix"], w["w_gates"],
               w["w_branch_ssm"], w["w_branch_ret"], w["w_out"], tm=tm_ffn)
    y = _ffn(h, w["norm_ffn2"], w["ffn2_w1"], w["ffn2_w3"], w["ffn2_w2"], final_gain, final_norm=True, tm=tm_ffn)
    return (y.reshape(nseq, seq_len, D_MODEL),
            ssm_new.reshape(nseq, SSM_HEADS, SSM_HEAD_DIM, SSM_STATE),
            conv_new,
            ret_new.reshape(nseq, RET_HEADS, RET_QK_DIM, RET_V_DIM))


def _prep_weights(norm_ffn1, ffn1_w1, ffn1_w3, ffn1_w2, norm_mix, w_in, conv_w, conv_b, dt_bias, a_log, ssm_d,
                  ssm_norm, ret_norm, w_branch_ssm, w_branch_ret, w_out, norm_ffn2, ffn2_w1, ffn2_w3, ffn2_w2):
    o_z = 0
    o_xbc = o_z + SSM_D_INNER
    o_dt = o_xbc + SSM_CONV_DIM
    o_q = o_dt + SSM_HEADS
    o_k = o_q + RET_QK
    o_v = o_k + RET_QK
    o_rg = o_v + RET_V
    o_ga = o_rg + RET_V
    o_end = o_ga + 2 * D_MODEL
    assert o_end == w_in.shape[1]

    def cols(a, b):
        return w_in[:, a:b]

    w_proj = jnp.concatenate([
        cols(o_z, o_xbc),
        cols(o_xbc, o_xbc + SSM_D_INNER),
        cols(o_v, o_rg),
        cols(o_rg, o_ga),
        cols(o_xbc + SSM_D_INNER, o_dt),
        cols(o_q, o_k),
        cols(o_k, o_v),
    ], axis=1).astype(BF16)
    w_dt = jnp.pad(cols(o_dt, o_q), ((0, 0), (0, DT_PAD - SSM_HEADS))).astype(BF16)
    half = RET_QK_DIM // 2
    head_of_channel = jnp.arange(SSM_D_INNER, dtype=jnp.int32) // SSM_HEAD_DIM
    return {
        "norm_ffn1": norm_ffn1.reshape(1, -1), "ffn1_w1": ffn1_w1.astype(BF16), "ffn1_w3": ffn1_w3.astype(BF16),
        "ffn1_w2": ffn1_w2.astype(BF16),
        "norm_mix": norm_mix.reshape(1, -1), "w_proj": w_proj, "w_dt": w_dt,
        "w_gates": cols(o_ga, o_end).astype(BF16),
        "conv_w": conv_w, "conv_b": conv_b.reshape(1, -1),
        "dt_bias": jnp.pad(dt_bias, (0, DT_PAD - SSM_HEADS)).reshape(1, -1),
        "a_log": jnp.pad(a_log, (0, DT_PAD - SSM_HEADS)).reshape(1, -1),
        "d_exp": jnp.repeat(ssm_d, SSM_HEAD_DIM).reshape(1, -1),
        "ssm_norm": ssm_norm.reshape(1, -1), "ret_norm": ret_norm.reshape(1, -1),
        "head_expand": (head_of_channel[None, :] == jnp.arange(DT_PAD, dtype=jnp.int32)[:, None]).astype(BF16),
        "rope_inv": (ROPE_BASE ** (-jnp.arange(half, dtype=F32) / half)).reshape(1, half),
        "w_branch_ssm": w_branch_ssm.astype(BF16), "w_branch_ret": w_branch_ret.astype(BF16),
        "w_out": w_out.astype(BF16),
        "norm_ffn2": norm_ffn2.reshape(1, -1), "ffn2_w1": ffn2_w1.astype(BF16), "ffn2_w3": ffn2_w3.astype(BF16),
        "ffn2_w2": ffn2_w2.astype(BF16),
    }


def kernel(x_prompt, x_sample, state_ssm, state_conv, state_ret, norm_ffn1, ffn1_w1, ffn1_w3, ffn1_w2, norm_mix, w_in,
           conv_w, conv_b, dt_bias, a_log, ssm_d, ssm_norm, ret_norm, w_branch_ssm, w_branch_ret, w_out, norm_ffn2,
           ffn2_w1, ffn2_w3, ffn2_w2, norm_final):
    depth = norm_ffn1.shape[0]
    assert depth == 1, "the final RMSNorm is fused into the last layer's second FFN"
    bp = x_prompt.shape[0]
    w = _prep_weights(norm_ffn1[0], ffn1_w1[0], ffn1_w3[0], ffn1_w2[0], norm_mix[0], w_in[0], conv_w[0], conv_b[0],
                      dt_bias[0], a_log[0], ssm_d[0], ssm_norm[0], ret_norm[0], w_branch_ssm[0], w_branch_ret[0],
                      w_out[0], norm_ffn2[0], ffn2_w1[0], ffn2_w3[0], ffn2_w2[0])
    final_gain = norm_final.reshape(1, -1)

    yp, ssm_p, conv_p, ret_p = _layer(
        x_prompt, 0.0,
        jnp.zeros((bp, SSM_HEADS, SSM_HEAD_DIM, SSM_STATE), F32),
        jnp.zeros((bp, SSM_CONV - 1, SSM_CONV_DIM), F32),
        jnp.zeros((bp, RET_HEADS, RET_QK_DIM, RET_V_DIM), F32),
        w, tm_ffn=512, tm_proj=1024, q=CHUNK, final_gain=final_gain)
    ys, ssm_s, conv_s, ret_s = _layer(
        x_sample, float(PAST_LEN), state_ssm[0], state_conv[0], state_ret[0],
        w, tm_ffn=256, tm_proj=512, q=V7X_SUBLANES, final_gain=final_gain)
    return (yp, ys, ssm_p[None], conv_p[None], ret_p[None], ssm_s[None], conv_s[None], ret_s[None])
```

```python
import functools
import math

import jax
import jax.numpy as jnp
import numpy as np
from jax import lax
from jax.experimental import pallas as pl
from jax.experimental.pallas import tpu as pltpu

F32 = jnp.float32
BF16 = jnp.bfloat16

D_MODEL = 1024
D_FF = 2816
SSM_D_INNER = 2048
SSM_HEAD_DIM = 64
SSM_HEADS = 32
SSM_GROUPS = 4
SSM_STATE = 128
SSM_CONV = 4
SSM_BC = 2 * SSM_GROUPS * SSM_STATE
SSM_CONV_DIM = SSM_D_INNER + SSM_BC
RET_HEADS = 4
RET_QK_DIM = 256
RET_V_DIM = 512
RET_QK = RET_HEADS * RET_QK_DIM
RET_V = RET_HEADS * RET_V_DIM
ROPE_BASE = 10000.0
PAST_LEN = 16384
CHUNK = 128
NORM_EPS = 1e-6
GATED_NORM_EPS = 1e-5

V7X_SUBLANES = 8
V7X_LANES = 128
V7X_VMEM_BYTES = 64 * 1024 * 1024
VMEM_LIMIT = V7X_VMEM_BYTES - 8 * 1024 * 1024

PROJ_Z = 0
PROJ_X = 2048
PROJ_V = 4096
PROJ_RG = 6144
PROJ_BC = 8192
PROJ_Q = 9216
PROJ_K = 10240
PROJ_DIM = 11264
PROJ_TN = 1024
DT_PAD = V7X_LANES
LOG_DT_FLOOR = -1e30


def _rms(x, g, eps):
    return x * lax.rsqrt(jnp.mean(x * x, axis=-1, keepdims=True) + eps) * g


def _sigmoid(x):
    return 1.0 / (1.0 + jnp.exp(-x))


def _silu(x):
    return x * _sigmoid(x)


def _softplus(x):
    return jnp.maximum(x, 0.0) + jnp.log1p(jnp.exp(-jnp.abs(x)))


def _dot(a, b):
    return jnp.dot(a.astype(BF16), b.astype(BF16), preferred_element_type=F32)


def _dot_nt(a, b):
    return lax.dot_general(a.astype(BF16), b.astype(BF16), (((1,), (1,)), ((), ())), preferred_element_type=F32)


def _dot_tn(a, b):
    return lax.dot_general(a.astype(BF16), b.astype(BF16), (((0,), (0,)), ((), ())), preferred_element_type=F32)


def _dot_split(a, b_f32, passes):
    acc = None
    rem = b_f32
    for _ in range(passes):
        piece = rem.astype(BF16)
        term = jnp.dot(a, piece, preferred_element_type=F32)
        acc = term if acc is None else acc + term
        rem = rem - piece.astype(F32)
    return acc


def _dot_split_lhs(a_f32, b, passes):
    acc = None
    rem = a_f32
    for _ in range(passes):
        piece = rem.astype(BF16)
        term = jnp.dot(piece, b, preferred_element_type=F32)
        acc = term if acc is None else acc + term
        rem = rem - piece.astype(F32)
    return acc


def _transpose_split(eye, x_f32, passes):
    acc = None
    rem = x_f32
    for _ in range(passes):
        piece = rem.astype(BF16)
        term = lax.dot_general(eye, piece, (((1,), (1,)), ((), ())), preferred_element_type=F32)
        acc = term if acc is None else acc + term
        rem = rem - piece.astype(F32)
    return acc


def _const_spec(shape):
    nd = len(shape)
    return pl.BlockSpec(shape, lambda *_: (0,) * nd, pipeline_mode=pl.Buffered(1))


def _params(n_grid_dims):
    return pltpu.CompilerParams(dimension_semantics=("arbitrary",) * n_grid_dims, vmem_limit_bytes=VMEM_LIMIT)


def _ffn_kernel(x_ref, g_ref, w1_ref, w3_ref, w2_ref, gf_ref, o_ref, *, final_norm):
    x = x_ref[...]
    xn = _rms(x, g_ref[...], NORM_EPS).astype(BF16)
    a = jnp.dot(xn, w1_ref[...], preferred_element_type=F32)
    b = jnp.dot(xn, w3_ref[...], preferred_element_type=F32)
    gated = (_silu(a) * b).astype(BF16)
    h = x + 0.5 * jnp.dot(gated, w2_ref[...], preferred_element_type=F32)
    if final_norm:
        h = _rms(h, gf_ref[...], NORM_EPS)
    o_ref[...] = h


def _ffn(x, g, w1, w3, w2, gf, *, final_norm, tm):
    t = x.shape[0]
    assert t % tm == 0
    return pl.pallas_call(
        functools.partial(_ffn_kernel, final_norm=final_norm),
        grid=(t // tm,),
        in_specs=[
            pl.BlockSpec((tm, D_MODEL), lambda i: (i, 0)),
            _const_spec((1, D_MODEL)),
            _const_spec((D_MODEL, D_FF)),
            _const_spec((D_MODEL, D_FF)),
            _const_spec((D_FF, D_MODEL)),
            _const_spec((1, D_MODEL)),
        ],
        out_specs=pl.BlockSpec((tm, D_MODEL), lambda i: (i, 0)),
        out_shape=jax.ShapeDtypeStruct((t, D_MODEL), F32),
        compiler_params=_params(1),
        name="ffn_final" if final_norm else "ffn",
    )(x, g, w1, w3, w2, gf)


def _inproj_kernel(h_ref, g_ref, w_ref, wdt_ref, o_ref, odt_ref, u_scr):
    @pl.when(pl.program_id(1) == 0)
    def _():
        u = _rms(h_ref[...], g_ref[...], NORM_EPS).astype(BF16)
        u_scr[...] = u
        odt_ref[...] = jnp.dot(u, wdt_ref[...], preferred_element_type=F32)

    o_ref[...] = jnp.dot(u_scr[...], w_ref[...], preferred_element_type=F32)


def _inproj(h, g, w, wdt, *, tm):
    t = h.shape[0]
    assert t % tm == 0 and PROJ_DIM % PROJ_TN == 0
    return pl.pallas_call(
        _inproj_kernel,
        grid=(t // tm, PROJ_DIM // PROJ_TN),
        in_specs=[
            pl.BlockSpec((tm, D_MODEL), lambda i, j: (i, 0)),
            _const_spec((1, D_MODEL)),
            pl.BlockSpec((D_MODEL, PROJ_TN), lambda i, j: (0, j)),
            _const_spec((D_MODEL, DT_PAD)),
        ],
        out_specs=[
            pl.BlockSpec((tm, PROJ_TN), lambda i, j: (i, j)),
            pl.BlockSpec((tm, DT_PAD), lambda i, j: (i, 0)),
        ],
        out_shape=[
            jax.ShapeDtypeStruct((t, PROJ_DIM), F32),
            jax.ShapeDtypeStruct((t, DT_PAD), F32),
        ],
        scratch_shapes=[pltpu.VMEM((tm, D_MODEL), BF16)],
        compiler_params=_params(2),
        name="inproj",
    )(h, g, w, wdt)


def _rope_kernel(inv_ref, cos_ref, sin_ref, *, pos0, rows_per_pos):
    rows = cos_ref.shape[0]
    step = lax.broadcasted_iota(jnp.int32, (rows, V7X_LANES), 0) // rows_per_pos
    ang = (pos0 + step.astype(F32)) * inv_ref[...]
    cos_ref[...] = jnp.cos(ang)
    sin_ref[...] = jnp.sin(ang)


def _rope_tables(inv, rows, pos0, rows_per_pos):
    return pl.pallas_call(
        functools.partial(_rope_kernel, pos0=pos0, rows_per_pos=rows_per_pos),
        out_shape=[jax.ShapeDtypeStruct((rows, V7X_LANES), F32)] * 2,
        name="rope",
    )(inv)


def _chunk_rows(ref):
    return ref[...].reshape(-1, ref.shape[-1])


def _row_time_seq(shape, s_blk):
    r = lax.broadcasted_iota(jnp.int32, shape, 0)
    return r // s_blk, r % s_blk


def _pair_mask(q, s_blk):
    r = lax.broadcasted_iota(jnp.int32, (q, q), 0)
    c = lax.broadcasted_iota(jnp.int32, (q, q), 1)
    if s_blk == 1:
        return r >= c
    return jnp.logical_and(r // s_blk >= c // s_blk, r % s_blk == c % s_blk)


def _rows_spec(q, s_blk, nc, width, col, time_major):
    cb = col // width
    assert col % width == 0
    if time_major:
        return pl.BlockSpec((q // s_blk, s_blk, width), lambda s, *rest: (0, s, cb))
    return pl.BlockSpec((q, width), lambda s, *rest: (s * nc + rest[-1], cb))


def _conv_pad_rows(s_blk):
    need = (SSM_CONV - 1) * s_blk
    return -(-need // V7X_SUBLANES) * V7X_SUBLANES


def _ssd_kernel(z_ref, x_ref, bc_ref, dt_ref, h0_ref, conv0_ref, cw_ref, cb_ref, dtb_ref, alog_ref, dexp_ref,
                gn_ref, e_ref, y_ref, hout_ref, convout_ref, xp_scr, *, s_blk, nc):
    c = pl.program_id(1)
    pad = _conv_pad_rows(s_blk)
    q = xp_scr.shape[0] - pad
    carry = (SSM_CONV - 1) * s_blk
    steps = q // s_blk
    hg = SSM_HEADS // SSM_GROUPS
    gw = hg * SSM_HEAD_DIM
    h_src = h0_ref if nc == 1 else hout_ref

    @pl.when(c == 0)
    def _init():
        xp_scr[0:pad, :] = jnp.zeros((pad, SSM_CONV_DIM), F32)
        xp_scr[pad - carry:pad, :] = conv0_ref[0]
        if nc > 1:
            hout_ref[...] = h0_ref[...]

    xp_scr[pad:pad + q, 0:SSM_D_INNER] = _chunk_rows(x_ref)
    xp_scr[pad:pad + q, SSM_D_INNER:SSM_CONV_DIM] = _chunk_rows(bc_ref)
    xall = xp_scr[...]
    conv = cb_ref[...] + xall[pad:pad + q, :] * cw_ref[SSM_CONV - 1:SSM_CONV, :]
    for j in range(1, SSM_CONV):
        d = j * s_blk
        if d % V7X_SUBLANES == 0:
            back = xall[pad - d:pad - d + q, :]
        else:
            back = pltpu.roll(xall, d, 0)[pad:pad + q, :]
        conv = conv + back * cw_ref[SSM_CONV - 1 - j:SSM_CONV - j, :]

    @pl.when(c == nc - 1)
    def _():
        convout_ref[0] = xp_scr[pad + q - carry:pad + q, :]

    if nc > 1:
        xp_scr[0:pad, :] = xp_scr[q:q + pad, :]
    xc = _silu(conv)
    xs = xc[:, 0:SSM_D_INNER]
    bm = xc[:, SSM_D_INNER:SSM_D_INNER + SSM_GROUPS * SSM_STATE]
    cm = xc[:, SSM_D_INNER + SSM_GROUPS * SSM_STATE:SSM_CONV_DIM]

    dt = _softplus(_chunk_rows(dt_ref) + dtb_ref[...])
    a = -jnp.exp(alog_ref[...])
    visible = _pair_mask(q, s_blk)
    la = _dot_split(jnp.where(visible, 1.0, 0.0).astype(BF16), dt * a, 3)
    lsd = la - jnp.maximum(jnp.log(dt), LOG_DT_FLOOR)
    eye_r = lax.broadcasted_iota(jnp.int32, (DT_PAD, DT_PAD), 0)
    eye_c = lax.broadcasted_iota(jnp.int32, (DT_PAD, DT_PAD), 1)
    eye = jnp.where(eye_r == eye_c, 1.0, 0.0).astype(BF16)
    la_t = _transpose_split(eye, la, 3)
    lsd_t = _transpose_split(eye, lsd, 3)
    la_last = la[q - s_blk:q, :]
    la_last_rows = la_last if s_blk == 1 else jnp.concatenate([la_last] * steps, axis=0)
    dec_t = jnp.exp(la_t[:, q - s_blk:q])

    lane = lax.broadcasted_iota(jnp.int32, (q, 2 * SSM_HEAD_DIM), 1)
    first_head = lane < SSM_HEAD_DIM
    y_parts = []
    for g in range(SSM_GROUPS):
        cg = cm[:, g * SSM_STATE:(g + 1) * SSM_STATE]
        bg = bm[:, g * SSM_STATE:(g + 1) * SSM_STATE]
        cbg = _dot_nt(cg, bg)
        for pair in range(hg // 2):
            h0 = g * hg + 2 * pair
            ws = []
            for h in (h0, h0 + 1):
                seg = la[:, h:h + 1] - lsd_t[h:h + 1, :]
                ws.append(cbg * jnp.exp(jnp.where(visible, seg, -jnp.inf)))
            xpair = xs[:, h0 * SSM_HEAD_DIM:(h0 + 2) * SSM_HEAD_DIM]
            rhs = jnp.concatenate([jnp.where(first_head, xpair, 0.0), jnp.where(first_head, 0.0, xpair)], axis=0)
            y_parts.append(_dot(jnp.concatenate(ws, axis=1), rhs))
    y = jnp.concatenate(y_parts, axis=1)

    _, seq_bc = _row_time_seq((q, SSM_BC // 2), s_blk)
    _, seq_x = _row_time_seq((q, SSM_D_INNER), s_blk)
    y_state = None
    for b in range(s_blk):
        cm_b = (cm if s_blk == 1 else jnp.where(seq_bc == b, cm, 0.0)).astype(BF16)
        part = jnp.concatenate(
            [_dot_nt(cm_b[:, g * SSM_STATE:(g + 1) * SSM_STATE], h_src[b, g * gw:(g + 1) * gw, :])
             for g in range(SSM_GROUPS)], axis=1)
        y_state = part if y_state is None else y_state + part

    e = e_ref[...]
    y = y + y_state * _dot_split_lhs(jnp.exp(la), e, 2) + xs * dexp_ref[...]
    y = y * _silu(_chunk_rows(z_ref))
    y = _rms(y, gn_ref[...], GATED_NORM_EPS)
    y_ref[...] = y.reshape(y_ref.shape).astype(y_ref.dtype)

    tail = jnp.exp(la_last_rows - lsd)
    xt = xs * _dot_split_lhs(tail, e, 2)
    bm_bf = bm.astype(BF16)
    for b in range(s_blk):
        xt_b = (xt if s_blk == 1 else jnp.where(seq_x == b, xt, 0.0)).astype(BF16)
        dec_b = jnp.broadcast_to(dec_t[:, b:b + 1], (DT_PAD, SSM_STATE))
        for g in range(SSM_GROUPS):
            upd = _dot_tn(xt_b[:, g * gw:(g + 1) * gw], bm_bf[:, g * SSM_STATE:(g + 1) * SSM_STATE])
            for hh in range(hg):
                h = g * hg + hh
                r0 = h * SSM_HEAD_DIM
                hout_ref[b, r0:r0 + SSM_HEAD_DIM, :] = (
                    h_src[b, r0:r0 + SSM_HEAD_DIM, :] * dec_b[h:h + 1, :]
                    + upd[hh * SSM_HEAD_DIM:(hh + 1) * SSM_HEAD_DIM, :])


def _ssd(proj, dtp, h0, conv0, cw, cb, dtb, alog, dexp, gn, e, *, nblk, nc, q, s_blk, time_major):
    rows = SSM_HEADS * SSM_HEAD_DIM
    carry = (SSM_CONV - 1) * s_blk
    spec = functools.partial(_rows_spec, q, s_blk, nc, time_major=time_major)
    y_shape = (q // s_blk, nblk * s_blk, SSM_D_INNER) if time_major else (nblk * nc * q, SSM_D_INNER)
    return pl.pallas_call(
        functools.partial(_ssd_kernel, s_blk=s_blk, nc=nc),
        grid=(nblk, nc),
        in_specs=[
            spec(SSM_D_INNER, PROJ_Z),
            spec(SSM_D_INNER, PROJ_X),
            spec(SSM_BC, PROJ_BC),
            spec(DT_PAD, 0),
            pl.BlockSpec((s_blk, rows, SSM_STATE), lambda s, c: (s, 0, 0)),
            pl.BlockSpec((1, carry, SSM_CONV_DIM), lambda s, c: (s, 0, 0)),
            _const_spec((SSM_CONV, SSM_CONV_DIM)),
            _const_spec((1, SSM_CONV_DIM)),
            _const_spec((1, DT_PAD)),
            _const_spec((1, DT_PAD)),
            _const_spec((1, SSM_D_INNER)),
            _const_spec((1, SSM_D_INNER)),
            _const_spec((DT_PAD, SSM_D_INNER)),
        ],
        out_specs=[
            spec(SSM_D_INNER, 0),
            pl.BlockSpec((s_blk, rows, SSM_STATE), lambda s, c: (s, 0, 0)),
            pl.BlockSpec((1, carry, SSM_CONV_DIM), lambda s, c: (s, 0, 0)),
        ],
        out_shape=[
            jax.ShapeDtypeStruct(y_shape, BF16),
            jax.ShapeDtypeStruct((nblk * s_blk, rows, SSM_STATE), F32),
            jax.ShapeDtypeStruct((nblk, carry, SSM_CONV_DIM), F32),
        ],
        scratch_shapes=[pltpu.VMEM((_conv_pad_rows(s_blk) + q, SSM_CONV_DIM), F32)],
        compiler_params=_params(2),
        name="ssd",
    )(proj, proj, proj, dtp, h0, conv0, cw, cb, dtb, alog, dexp, gn, e)


def _ret_log_decay(h):
    return float(np.log1p(-np.exp2(-5.0 - h)))


def _ret_kernel(q_ref, k_ref, v_ref, rg_ref, cos_ref, sin_ref, s0_ref, gn_ref, y_ref, sout_ref, *, s_blk, nc, n_heads):
    hb = pl.program_id(1)
    c = pl.program_id(2)
    q = cos_ref.shape[0]
    steps = q // s_blk
    half = RET_QK_DIM // 2
    s_src = s0_ref if nc == 1 else sout_ref

    if nc > 1:
        @pl.when(c == 0)
        def _init():
            sout_ref[...] = s0_ref[...]

    qq = _chunk_rows(q_ref)
    kk = _chunk_rows(k_ref)
    vv = _chunk_rows(v_ref).astype(BF16)
    rg = _chunk_rows(rg_ref)
    cos = cos_ref[...]
    sin = sin_ref[...]
    visible = _pair_mask(q, s_blk)
    r = lax.broadcasted_iota(jnp.int32, (q, q), 0)
    cc = lax.broadcasted_iota(jnp.int32, (q, q), 1)
    rel = (r // s_blk - cc // s_blk).astype(F32)
    t_qk, seq_qk = _row_time_seq((q, RET_QK_DIM), s_blk)
    t_qk = t_qk.astype(F32)

    def rot(t, h):
        t1 = t[:, h * RET_QK_DIM:h * RET_QK_DIM + half]
        t2 = t[:, h * RET_QK_DIM + half:(h + 1) * RET_QK_DIM]
        return jnp.concatenate([t1 * cos - t2 * sin, t1 * sin + t2 * cos], axis=1)

    y_parts = []
    for h in range(n_heads):
        if n_heads == RET_HEADS:
            lg = _ret_log_decay(h)
            chunk_dec = math.exp(steps * lg)
        else:
            assert n_heads == 1
            lg = jnp.float32(_ret_log_decay(0))
            chunk_dec = jnp.float32(math.exp(steps * _ret_log_decay(0)))
            for hh in range(1, RET_HEADS):
                lg = jnp.where(hb == hh, jnp.float32(_ret_log_decay(hh)), lg)
                chunk_dec = jnp.where(hb == hh, jnp.float32(math.exp(steps * _ret_log_decay(hh))), chunk_dec)
        dmat = jnp.where(visible, jnp.exp(jnp.maximum(rel, 0.0) * lg), 0.0)
        q_dec = jnp.exp((t_qk + 1.0) * lg)
        k_dec = jnp.exp((steps - 1.0 - t_qk) * lg)
        qr = rot(qq, h)
        kr = rot(kk, h) * (RET_QK_DIM ** -0.5)
        vh = vv[:, h * RET_V_DIM:(h + 1) * RET_V_DIM]
        scores = _dot_nt(qr, kr) * dmat
        y = _dot(scores, vh)
        qd = qr * q_dec
        kd = kr * k_dec
        r0 = h * RET_QK_DIM
        for b in range(s_blk):
            qd_b = (qd if s_blk == 1 else jnp.where(seq_qk == b, qd, 0.0)).astype(BF16)
            kd_b = (kd if s_blk == 1 else jnp.where(seq_qk == b, kd, 0.0)).astype(BF16)
            sh = s_src[b, r0:r0 + RET_QK_DIM, :]
            y = y + _dot(qd_b, sh)
            sout_ref[b, r0:r0 + RET_QK_DIM, :] = sh * chunk_dec + _dot_tn(kd_b, vh)
        g = gn_ref[:, h * RET_V_DIM:(h + 1) * RET_V_DIM]
        y = _rms(y, g, NORM_EPS) * _silu(rg[:, h * RET_V_DIM:(h + 1) * RET_V_DIM])
        y_parts.append(y)
    y = y_parts[0] if n_heads == 1 else jnp.concatenate(y_parts, axis=1)
    y_ref[...] = y.reshape(y_ref.shape).astype(y_ref.dtype)


def _ret(proj, cos, sin, s0, gn, *, nblk, nc, q, s_blk, n_heads, time_major):
    nhb = RET_HEADS // n_heads
    qk_w = n_heads * RET_QK_DIM
    v_w = n_heads * RET_V_DIM

    def spec(width, col):
        cb = col // width
        assert col % width == 0
        if time_major:
            return pl.BlockSpec((q // s_blk, s_blk, width), lambda s, hb, c: (0, s, cb + hb))
        return pl.BlockSpec((q, width), lambda s, hb, c: (s * nc + c, cb + hb))

    y_shape = (q // s_blk, nblk * s_blk, RET_V) if time_major else (nblk * nc * q, RET_V)
    return pl.pallas_call(
        functools.partial(_ret_kernel, s_blk=s_blk, nc=nc, n_heads=n_heads),
        grid=(nblk, nhb, nc),
        in_specs=[
            spec(qk_w, PROJ_Q),
            spec(qk_w, PROJ_K),
            spec(v_w, PROJ_V),
            spec(v_w, PROJ_RG),
            pl.BlockSpec((q, V7X_LANES), lambda s, hb, c: (c, 0)),
            pl.BlockSpec((q, V7X_LANES), lambda s, hb, c: (c, 0)),
            pl.BlockSpec((s_blk, qk_w, RET_V_DIM), lambda s, hb, c: (s, hb, 0)),
            pl.BlockSpec((1, v_w), lambda s, hb, c: (0, hb)),
        ],
        out_specs=[
            spec(v_w, 0),
            pl.BlockSpec((s_blk, qk_w, RET_V_DIM), lambda s, hb, c: (s, hb, 0)),
        ],
        out_shape=[
            jax.ShapeDtypeStruct(y_shape, BF16),
            jax.ShapeDtypeStruct((nblk * s_blk, RET_HEADS * RET_QK_DIM, RET_V_DIM), F32),
        ],
        compiler_params=_params(3),
        name="ret",
    )(proj, proj, proj, proj, cos, sin, s0, gn)


def _merge_kernel(h_ref, ys_ref, yr_ref, g_ref, wg_ref, wbs_ref, wbr_ref, wo_ref, o_ref):
    h = h_ref[...]
    u = _rms(h, g_ref[...], NORM_EPS).astype(BF16)
    gates = jnp.dot(u, wg_ref[...], preferred_element_type=F32)
    branch_ssm = jnp.dot(ys_ref[...], wbs_ref[...], preferred_element_type=F32)
    branch_ret = jnp.dot(yr_ref[...], wbr_ref[...], preferred_element_type=F32)
    merged = _sigmoid(gates[:, 0:D_MODEL]) * branch_ssm + _sigmoid(gates[:, D_MODEL:2 * D_MODEL]) * branch_ret
    o_ref[...] = h + jnp.dot(merged.astype(BF16), wo_ref[...], preferred_element_type=F32)


def _merge(h, ys, yr, g, wg, wbs, wbr, wo, *, tm):
    t = h.shape[0]
    assert t % tm == 0
    return pl.pallas_call(
        _merge_kernel,
        grid=(t // tm,),
        in_specs=[
            pl.BlockSpec((tm, D_MODEL), lambda i: (i, 0)),
            pl.BlockSpec((tm, SSM_D_INNER), lambda i: (i, 0)),
            pl.BlockSpec((tm, RET_V), lambda i: (i, 0)),
            _const_spec((1, D_MODEL)),
            _const_spec((D_MODEL, 2 * D_MODEL)),
            _const_spec((SSM_D_INNER, D_MODEL)),
            _const_spec((RET_V, D_MODEL)),
            _const_spec((D_MODEL, D_MODEL)),
        ],
        out_specs=pl.BlockSpec((tm, D_MODEL), lambda i: (i, 0)),
        out_shape=jax.ShapeDtypeStruct((t, D_MODEL), F32),
        compiler_params=_params(1),
        name="merge",
    )(h, ys, yr, g, wg, wbs, wbr, wo)


FFN_TILE = 512
PROJ_TILE = 1024
SAMPLE_SEQS_PER_STEP = V7X_SUBLANES
SAMPLE_RET_HEADS_PER_STEP = 1


def _layer(x2, pos0, ssm0, conv0, ret0, w, final_gain, *, nblk, nc, q, s_blk, ret_heads, time_major):
    t = x2.shape[0]
    tm_ffn = min(FFN_TILE, t)
    tm_proj = min(PROJ_TILE, t)
    steps = q // s_blk
    nseq = nblk * s_blk

    h = _ffn(x2, w["norm_ffn1"], w["ffn1_w1"], w["ffn1_w3"], w["ffn1_w2"], final_gain, final_norm=False, tm=tm_ffn)
    proj, dtp = _inproj(h, w["norm_mix"], w["w_proj"], w["w_dt"], tm=tm_proj)
    if time_major:
        proj = proj.reshape(steps, nseq, PROJ_DIM)
        dtp = dtp.reshape(steps, nseq, DT_PAD)

    y_ssm, ssm_new, conv_new = _ssd(
        proj, dtp, ssm0, conv0, w["conv_w"], w["conv_b"], w["dt_bias"], w["a_log"], w["d_exp"], w["ssm_norm"],
        w["head_expand"], nblk=nblk, nc=nc, q=q, s_blk=s_blk, time_major=time_major)
    cos, sin = _rope_tables(w["rope_inv"], nc * q, pos0, s_blk)
    y_ret, ret_new = _ret(proj, cos, sin, ret0, w["ret_norm"], nblk=nblk, nc=nc, q=q, s_blk=s_blk,
                          n_heads=ret_heads, time_major=time_major)

    h = _merge(h, y_ssm.reshape(t, SSM_D_INNER), y_ret.reshape(t, RET_V), w["norm_mix"], w["w_gates"],
               w["w_branch_ssm"], w["w_branch_ret"], w["w_out"], tm=tm_ffn)
    y = _ffn(h, w["norm_ffn2"], w["ffn2_w1"], w["ffn2_w3"], w["ffn2_w2"], final_gain, final_norm=True, tm=tm_ffn)
    return y, ssm_new, conv_new, ret_new


def _prompt_layer(x, w, final_gain):
    nseq, seq_len, _ = x.shape
    q = math.gcd(seq_len, CHUNK)
    y, ssm_new, conv_new, ret_new = _layer(
        x.reshape(nseq * seq_len, D_MODEL), 0.0,
        jnp.zeros((nseq, SSM_HEADS * SSM_HEAD_DIM, SSM_STATE), F32),
        jnp.zeros((nseq, SSM_CONV - 1, SSM_CONV_DIM), F32),
        jnp.zeros((nseq, RET_HEADS * RET_QK_DIM, RET_V_DIM), F32),
        w, final_gain, nblk=nseq, nc=seq_len // q, q=q, s_blk=1, ret_heads=RET_HEADS, time_major=False)
    return (y.reshape(nseq, seq_len, D_MODEL), ssm_new.reshape(nseq, SSM_HEADS, SSM_HEAD_DIM, SSM_STATE), conv_new,
            ret_new.reshape(nseq, RET_HEADS, RET_QK_DIM, RET_V_DIM))


def _sample_layer(x, ssm0, conv0, ret0, w, final_gain):
    nseq, seq_len, _ = x.shape
    s_blk = SAMPLE_SEQS_PER_STEP
    assert nseq % s_blk == 0 and seq_len <= CHUNK
    nblk = nseq // s_blk
    kc = SSM_CONV - 1
    x_tm = jnp.transpose(x, (1, 0, 2)).reshape(seq_len * nseq, D_MODEL)
    conv0_tm = jnp.transpose(conv0.reshape(nblk, s_blk, kc, SSM_CONV_DIM), (0, 2, 1, 3)).reshape(
        nblk, kc * s_blk, SSM_CONV_DIM)
    y, ssm_new, conv_new, ret_new = _layer(
        x_tm, float(PAST_LEN), ssm0.reshape(nseq, SSM_HEADS * SSM_HEAD_DIM, SSM_STATE), conv0_tm,
        ret0.reshape(nseq, RET_HEADS * RET_QK_DIM, RET_V_DIM), w, final_gain,
        nblk=nblk, nc=1, q=s_blk * seq_len, s_blk=s_blk, ret_heads=SAMPLE_RET_HEADS_PER_STEP, time_major=True)
    conv_new = jnp.transpose(conv_new.reshape(nblk, kc, s_blk, SSM_CONV_DIM), (0, 2, 1, 3)).reshape(
        nseq, kc, SSM_CONV_DIM)
    return (jnp.transpose(y.reshape(seq_len, nseq, D_MODEL), (1, 0, 2)),
            ssm_new.reshape(nseq, SSM_HEADS, SSM_HEAD_DIM, SSM_STATE), conv_new,
            ret_new.reshape(nseq, RET_HEADS, RET_QK_DIM, RET_V_DIM))


def _prep_weights(norm_ffn1, ffn1_w1, ffn1_w3, ffn1_w2, norm_mix, w_in, conv_w, conv_b, dt_bias, a_log, ssm_d,
                  ssm_norm, ret_norm, w_branch_ssm, w_branch_ret, w_out, norm_ffn2, ffn2_w1, ffn2_w3, ffn2_w2):
    o_z = 0
    o_xbc = o_z + SSM_D_INNER
    o_dt = o_xbc + SSM_CONV_DIM
    o_q = o_dt + SSM_HEADS
    o_k = o_q + RET_QK
    o_v = o_k + RET_QK
    o_rg = o_v + RET_V
    o_ga = o_rg + RET_V
    o_end = o_ga + 2 * D_MODEL
    assert o_end == w_in.shape[1]

    def cols(a, b):
        return w_in[:, a:b]

    w_proj = jnp.concatenate([
        cols(o_z, o_xbc),
        cols(o_xbc, o_xbc + SSM_D_INNER),
        cols(o_v, o_rg),
        cols(o_rg, o_ga),
        cols(o_xbc + SSM_D_INNER, o_dt),
        cols(o_q, o_k),
        cols(o_k, o_v),
    ], axis=1).astype(BF16)
    w_dt = jnp.pad(cols(o_dt, o_q), ((0, 0), (0, DT_PAD - SSM_HEADS))).astype(BF16)
    half = RET_QK_DIM // 2
    head_of_channel = jnp.arange(SSM_D_INNER, dtype=jnp.int32) // SSM_HEAD_DIM
    return {
        "norm_ffn1": norm_ffn1.reshape(1, -1), "ffn1_w1": ffn1_w1.astype(BF16), "ffn1_w3": ffn1_w3.astype(BF16),
        "ffn1_w2": ffn1_w2.astype(BF16),
        "norm_mix": norm_mix.reshape(1, -1), "w_proj": w_proj, "w_dt": w_dt,
        "w_gates": cols(o_ga, o_end).astype(BF16),
        "conv_w": conv_w, "conv_b": conv_b.reshape(1, -1),
        "dt_bias": jnp.pad(dt_bias, (0, DT_PAD - SSM_HEADS)).reshape(1, -1),
        "a_log": jnp.pad(a_log, (0, DT_PAD - SSM_HEADS)).reshape(1, -1),
        "d_exp": jnp.repeat(ssm_d, SSM_HEAD_DIM).reshape(1, -1),
        "ssm_norm": ssm_norm.reshape(1, -1), "ret_norm": ret_norm.reshape(1, -1),
        "head_expand": (head_of_channel[None, :] == jnp.arange(DT_PAD, dtype=jnp.int32)[:, None]).astype(BF16),
        "rope_inv": (ROPE_BASE ** (-jnp.arange(half, dtype=F32) / half)).reshape(1, half),
        "w_branch_ssm": w_branch_ssm.astype(BF16), "w_branch_ret": w_branch_ret.astype(BF16),
        "w_out": w_out.astype(BF16),
        "norm_ffn2": norm_ffn2.reshape(1, -1), "ffn2_w1": ffn2_w1.astype(BF16), "ffn2_w3": ffn2_w3.astype(BF16),
        "ffn2_w2": ffn2_w2.astype(BF16),
    }


def kernel(x_prompt, x_sample, state_ssm, state_conv, state_ret, norm_ffn1, ffn1_w1, ffn1_w3, ffn1_w2, norm_mix, w_in,
           conv_w, conv_b, dt_bias, a_log, ssm_d, ssm_norm, ret_norm, w_branch_ssm, w_branch_ret, w_out, norm_ffn2,
           ffn2_w1, ffn2_w3, ffn2_w2, norm_final):
    depth = norm_ffn1.shape[0]
    assert depth == 1, "the final RMSNorm is fused into the last layer's second FFN"
    w = _prep_weights(norm_ffn1[0], ffn1_w1[0], ffn1_w3[0], ffn1_w2[0], norm_mix[0], w_in[0], conv_w[0], conv_b[0],
                      dt_bias[0], a_log[0], ssm_d[0], ssm_norm[0], ret_norm[0], w_branch_ssm[0], w_branch_ret[0],
                      w_out[0], norm_ffn2[0], ffn2_w1[0], ffn2_w3[0], ffn2_w2[0])
    final_gain = norm_final.reshape(1, -1)
    yp, ssm_p, conv_p, ret_p = _prompt_layer(x_prompt, w, final_gain)
    ys, ssm_s, conv_s, ret_s = _sample_layer(x_sample, state_ssm[0], state_conv[0], state_ret[0], w, final_gain)
    return (yp, ys, ssm_p[None], conv_p[None], ret_p[None], ssm_s[None], conv_s[None], ret_s[None])
```

```python
import functools
import math

import jax
import jax.numpy as jnp
import numpy as np
from jax import lax
from jax.experimental import pallas as pl
from jax.experimental.pallas import tpu as pltpu

F32 = jnp.float32
BF16 = jnp.bfloat16

D_MODEL = 1024
D_FF = 2816
SSM_D_INNER = 2048
SSM_HEAD_DIM = 64
SSM_HEADS = 32
SSM_GROUPS = 4
SSM_STATE = 128
SSM_CONV = 4
SSM_BC = 2 * SSM_GROUPS * SSM_STATE
SSM_CONV_DIM = SSM_D_INNER + SSM_BC
RET_HEADS = 4
RET_QK_DIM = 256
RET_V_DIM = 512
RET_QK = RET_HEADS * RET_QK_DIM
RET_V = RET_HEADS * RET_V_DIM
ROPE_BASE = 10000.0
PAST_LEN = 16384
CHUNK = 128
NORM_EPS = 1e-6
GATED_NORM_EPS = 1e-5

V7X_SUBLANES = 8
V7X_LANES = 128
V7X_VMEM_BYTES = 64 * 1024 * 1024
VMEM_LIMIT = V7X_VMEM_BYTES - 8 * 1024 * 1024

DT_PAD = V7X_LANES
LOG_DT_FLOOR = -1e30

SSDP_Z = 0
SSDP_XBC = SSM_D_INNER
SSDP_DT = SSM_D_INNER + SSM_CONV_DIM
SSDP_DIM = SSDP_DT + DT_PAD


def _rms(x, g, eps):
    return x * lax.rsqrt(jnp.mean(x * x, axis=-1, keepdims=True) + eps) * g


def _sigmoid(x):
    return 1.0 / (1.0 + jnp.exp(-x))


def _silu(x):
    return x * _sigmoid(x)


def _softplus(x):
    return jnp.maximum(x, 0.0) + jnp.log1p(jnp.exp(-jnp.abs(x)))


def _dot(a, b):
    return jnp.dot(a.astype(BF16), b.astype(BF16), preferred_element_type=F32)


def _dot_nt(a, b):
    return lax.dot_general(a.astype(BF16), b.astype(BF16), (((1,), (1,)), ((), ())), preferred_element_type=F32)


def _dot_tn(a, b):
    return lax.dot_general(a.astype(BF16), b.astype(BF16), (((0,), (0,)), ((), ())), preferred_element_type=F32)


def _dot_split(a, b_f32, passes):
    acc = None
    rem = b_f32
    for _ in range(passes):
        piece = rem.astype(BF16)
        term = jnp.dot(a, piece, preferred_element_type=F32)
        acc = term if acc is None else acc + term
        rem = rem - piece.astype(F32)
    return acc


def _dot_split_lhs(a_f32, b, passes):
    acc = None
    rem = a_f32
    for _ in range(passes):
        piece = rem.astype(BF16)
        term = jnp.dot(piece, b, preferred_element_type=F32)
        acc = term if acc is None else acc + term
        rem = rem - piece.astype(F32)
    return acc


def _transpose_split(eye, x_f32, passes):
    acc = None
    rem = x_f32
    for _ in range(passes):
        piece = rem.astype(BF16)
        term = lax.dot_general(eye, piece, (((1,), (1,)), ((), ())), preferred_element_type=F32)
        acc = term if acc is None else acc + term
        rem = rem - piece.astype(F32)
    return acc


def _const_spec(shape):
    nd = len(shape)
    return pl.BlockSpec(shape, lambda *_: (0,) * nd, pipeline_mode=pl.Buffered(1))


def _params(n_grid_dims):
    return pltpu.CompilerParams(dimension_semantics=("arbitrary",) * n_grid_dims, vmem_limit_bytes=VMEM_LIMIT)


def _ffn_kernel(x_ref, g_ref, w1_ref, w3_ref, w2_ref, gf_ref, o_ref, *, final_norm):
    x = x_ref[...]
    xn = _rms(x, g_ref[...], NORM_EPS).astype(BF16)
    a = jnp.dot(xn, w1_ref[...], preferred_element_type=F32)
    b = jnp.dot(xn, w3_ref[...], preferred_element_type=F32)
    gated = (_silu(a) * b).astype(BF16)
    h = x + 0.5 * jnp.dot(gated, w2_ref[...], preferred_element_type=F32)
    if final_norm:
        h = _rms(h, gf_ref[...], NORM_EPS)
    o_ref[...] = h


def _ffn(x, g, w1, w3, w2, gf, *, final_norm, tm):
    t = x.shape[0]
    assert t % tm == 0
    return pl.pallas_call(
        functools.partial(_ffn_kernel, final_norm=final_norm),
        grid=(t // tm,),
        in_specs=[
            pl.BlockSpec((tm, D_MODEL), lambda i: (i, 0)),
            _const_spec((1, D_MODEL)),
            _const_spec((D_MODEL, D_FF)),
            _const_spec((D_MODEL, D_FF)),
            _const_spec((D_FF, D_MODEL)),
            _const_spec((1, D_MODEL)),
        ],
        out_specs=pl.BlockSpec((tm, D_MODEL), lambda i: (i, 0)),
        out_shape=jax.ShapeDtypeStruct((t, D_MODEL), F32),
        compiler_params=_params(1),
        name="ffn_final" if final_norm else "ffn",
    )(x, g, w1, w3, w2, gf)


def _rope_kernel(inv_ref, cos_ref, sin_ref, *, pos0, rows_per_pos):
    rows = cos_ref.shape[0]
    step = lax.broadcasted_iota(jnp.int32, (rows, V7X_LANES), 0) // rows_per_pos
    ang = (pos0 + step.astype(F32)) * inv_ref[...]
    cos_ref[...] = jnp.cos(ang)
    sin_ref[...] = jnp.sin(ang)


def _rope_tables(inv, rows, pos0, rows_per_pos):
    return pl.pallas_call(
        functools.partial(_rope_kernel, pos0=pos0, rows_per_pos=rows_per_pos),
        out_shape=[jax.ShapeDtypeStruct((rows, V7X_LANES), F32)] * 2,
        name="rope",
    )(inv)


def _chunk_rows(ref):
    return ref[...].reshape(-1, ref.shape[-1])


def _row_time_seq(shape, s_blk):
    r = lax.broadcasted_iota(jnp.int32, shape, 0)
    return r // s_blk, r % s_blk


def _pair_mask(q, s_blk):
    r = lax.broadcasted_iota(jnp.int32, (q, q), 0)
    c = lax.broadcasted_iota(jnp.int32, (q, q), 1)
    if s_blk == 1:
        return r >= c
    return jnp.logical_and(r // s_blk >= c // s_blk, r % s_blk == c % s_blk)


def _rows_spec(q, s_blk, nblk, nc, width, time_major, *, ahead=0, lead_axes=0):
    last = nblk * nc - 1

    def chunk_index(args):
        s, c = args[lead_axes], args[lead_axes + 1]
        return jnp.minimum(s * nc + c + ahead, last)

    if time_major:
        assert nc == 1
        return pl.BlockSpec((q // s_blk, s_blk, width), lambda *a: (0, chunk_index(a), 0))
    return pl.BlockSpec((q, width), lambda *a: (chunk_index(a), 0))


def _project(h_ref, g_ref, w_refs):
    u = _rms(_chunk_rows(h_ref), g_ref[...], NORM_EPS).astype(BF16)
    return [jnp.dot(u, w_ref[...], preferred_element_type=F32) for w_ref in w_refs]


def _conv_pad_rows(s_blk):
    need = (SSM_CONV - 1) * s_blk
    return -(-need // V7X_SUBLANES) * V7X_SUBLANES


def _ssd_kernel(hc_ref, hn_ref, gmix_ref, w_ref, h0_ref, conv0_ref, cw_ref, cb_ref, dtb_ref, alog_ref, dexp_ref,
                gn_ref, e_ref, y_ref, hout_ref, convout_ref, xp_scr, zdt_scr, next_scr, *, s_blk, nc):
    s = pl.program_id(0)
    c = pl.program_id(1)
    pad = _conv_pad_rows(s_blk)
    q = xp_scr.shape[0] - pad
    carry = (SSM_CONV - 1) * s_blk
    steps = q // s_blk
    hg = SSM_HEADS // SSM_GROUPS
    gw = hg * SSM_HEAD_DIM
    h_src = h0_ref if nc == 1 else hout_ref

    @pl.when(jnp.logical_and(s == 0, c == 0))
    def _first():
        next_scr[...] = _project(hc_ref, gmix_ref, [w_ref])[0]

    @pl.when(c == 0)
    def _init():
        xp_scr[0:pad, :] = jnp.zeros((pad, SSM_CONV_DIM), F32)
        xp_scr[pad - carry:pad, :] = conv0_ref[0]
        if nc > 1:
            hout_ref[...] = h0_ref[...]

    xp_scr[pad:pad + q, :] = next_scr[:, SSDP_XBC:SSDP_DT]
    zdt_scr[:, 0:SSM_D_INNER] = next_scr[:, SSDP_Z:SSDP_XBC]
    zdt_scr[:, SSM_D_INNER:SSM_D_INNER + DT_PAD] = next_scr[:, SSDP_DT:SSDP_DIM]
    next_scr[...] = _project(hn_ref, gmix_ref, [w_ref])[0]

    xall = xp_scr[...]
    z_rows = zdt_scr[:, 0:SSM_D_INNER]
    dt_rows = zdt_scr[:, SSM_D_INNER:SSM_D_INNER + DT_PAD]
    conv = cb_ref[...] + xall[pad:pad + q, :] * cw_ref[SSM_CONV - 1:SSM_CONV, :]
    for j in range(1, SSM_CONV):
        d = j * s_blk
        if d % V7X_SUBLANES == 0:
            back = xall[pad - d:pad - d + q, :]
        else:
            back = pltpu.roll(xall, d, 0)[pad:pad + q, :]
        conv = conv + back * cw_ref[SSM_CONV - 1 - j:SSM_CONV - j, :]

    @pl.when(c == nc - 1)
    def _():
        convout_ref[0] = xp_scr[pad + q - carry:pad + q, :]

    if nc > 1:
        xp_scr[0:pad, :] = xp_scr[q:q + pad, :]
    xc = _silu(conv)
    xs = xc[:, 0:SSM_D_INNER]
    bm = xc[:, SSM_D_INNER:SSM_D_INNER + SSM_GROUPS * SSM_STATE]
    cm = xc[:, SSM_D_INNER + SSM_GROUPS * SSM_STATE:SSM_CONV_DIM]

    dt = _softplus(dt_rows + dtb_ref[...])
    a = -jnp.exp(alog_ref[...])
    visible = _pair_mask(q, s_blk)
    la = _dot_split(jnp.where(visible, 1.0, 0.0).astype(BF16), dt * a, 3)
    lsd = la - jnp.maximum(jnp.log(dt), LOG_DT_FLOOR)
    eye_r = lax.broadcasted_iota(jnp.int32, (DT_PAD, DT_PAD), 0)
    eye_c = lax.broadcasted_iota(jnp.int32, (DT_PAD, DT_PAD), 1)
    eye = jnp.where(eye_r == eye_c, 1.0, 0.0).astype(BF16)
    la_t = _transpose_split(eye, la, 3)
    lsd_t = _transpose_split(eye, lsd, 3)
    la_last = la[q - s_blk:q, :]
    la_last_rows = la_last if s_blk == 1 else jnp.concatenate([la_last] * steps, axis=0)
    dec_t = jnp.exp(la_t[:, q - s_blk:q])

    lane = lax.broadcasted_iota(jnp.int32, (q, 2 * SSM_HEAD_DIM), 1)
    first_head = lane < SSM_HEAD_DIM
    y_parts = []
    for g in range(SSM_GROUPS):
        cg = cm[:, g * SSM_STATE:(g + 1) * SSM_STATE]
        bg = bm[:, g * SSM_STATE:(g + 1) * SSM_STATE]
        cbg = _dot_nt(cg, bg)
        for pair in range(hg // 2):
            h0 = g * hg + 2 * pair
            ws = []
            for h in (h0, h0 + 1):
                seg = la[:, h:h + 1] - lsd_t[h:h + 1, :]
                ws.append(cbg * jnp.exp(jnp.where(visible, seg, -jnp.inf)))
            xpair = xs[:, h0 * SSM_HEAD_DIM:(h0 + 2) * SSM_HEAD_DIM]
            rhs = jnp.concatenate([jnp.where(first_head, xpair, 0.0), jnp.where(first_head, 0.0, xpair)], axis=0)
            y_parts.append(_dot(jnp.concatenate(ws, axis=1), rhs))
    y = jnp.concatenate(y_parts, axis=1)

    _, seq_bc = _row_time_seq((q, SSM_BC // 2), s_blk)
    _, seq_x = _row_time_seq((q, SSM_D_INNER), s_blk)
    y_state = None
    for b in range(s_blk):
        cm_b = (cm if s_blk == 1 else jnp.where(seq_bc == b, cm, 0.0)).astype(BF16)
        part = jnp.concatenate(
            [_dot_nt(cm_b[:, g * SSM_STATE:(g + 1) * SSM_STATE], h_src[b, g * gw:(g + 1) * gw, :])
             for g in range(SSM_GROUPS)], axis=1)
        y_state = part if y_state is None else y_state + part

    e = e_ref[...]
    y = y + y_state * _dot_split_lhs(jnp.exp(la), e, 2) + xs * dexp_ref[...]
    y = y * _silu(z_rows)
    y = _rms(y, gn_ref[...], GATED_NORM_EPS)
    y_ref[...] = y.reshape(y_ref.shape).astype(y_ref.dtype)

    tail = jnp.exp(la_last_rows - lsd)
    xt = xs * _dot_split_lhs(tail, e, 2)
    bm_bf = bm.astype(BF16)
    for b in range(s_blk):
        xt_b = (xt if s_blk == 1 else jnp.where(seq_x == b, xt, 0.0)).astype(BF16)
        dec_b = jnp.broadcast_to(dec_t[:, b:b + 1], (DT_PAD, SSM_STATE))
        for g in range(SSM_GROUPS):
            upd = _dot_tn(xt_b[:, g * gw:(g + 1) * gw], bm_bf[:, g * SSM_STATE:(g + 1) * SSM_STATE])
            for hh in range(hg):
                h = g * hg + hh
                r0 = h * SSM_HEAD_DIM
                hout_ref[b, r0:r0 + SSM_HEAD_DIM, :] = (
                    h_src[b, r0:r0 + SSM_HEAD_DIM, :] * dec_b[h:h + 1, :]
                    + upd[hh * SSM_HEAD_DIM:(hh + 1) * SSM_HEAD_DIM, :])


def _ssd(h, gmix, w, h0, conv0, cw, cb, dtb, alog, dexp, gn, e, *, nblk, nc, q, s_blk, time_major):
    rows = SSM_HEADS * SSM_HEAD_DIM
    carry = (SSM_CONV - 1) * s_blk
    spec = functools.partial(_rows_spec, q, s_blk, nblk, nc, time_major=time_major)
    y_shape = (q // s_blk, nblk * s_blk, SSM_D_INNER) if time_major else (nblk * nc * q, SSM_D_INNER)
    return pl.pallas_call(
        functools.partial(_ssd_kernel, s_blk=s_blk, nc=nc),
        grid=(nblk, nc),
        in_specs=[
            spec(D_MODEL),
            spec(D_MODEL, ahead=1),
            _const_spec((1, D_MODEL)),
            _const_spec((D_MODEL, SSDP_DIM)),
            pl.BlockSpec((s_blk, rows, SSM_STATE), lambda s, c: (s, 0, 0)),
            pl.BlockSpec((1, carry, SSM_CONV_DIM), lambda s, c: (s, 0, 0)),
            _const_spec((SSM_CONV, SSM_CONV_DIM)),
            _const_spec((1, SSM_CONV_DIM)),
            _const_spec((1, DT_PAD)),
            _const_spec((1, DT_PAD)),
            _const_spec((1, SSM_D_INNER)),
            _const_spec((1, SSM_D_INNER)),
            _const_spec((DT_PAD, SSM_D_INNER)),
        ],
        out_specs=[
            spec(SSM_D_INNER),
            pl.BlockSpec((s_blk, rows, SSM_STATE), lambda s, c: (s, 0, 0)),
            pl.BlockSpec((1, carry, SSM_CONV_DIM), lambda s, c: (s, 0, 0)),
        ],
        out_shape=[
            jax.ShapeDtypeStruct(y_shape, BF16),
            jax.ShapeDtypeStruct((nblk * s_blk, rows, SSM_STATE), F32),
            jax.ShapeDtypeStruct((nblk, carry, SSM_CONV_DIM), F32),
        ],
        scratch_shapes=[
            pltpu.VMEM((_conv_pad_rows(s_blk) + q, SSM_CONV_DIM), F32),
            pltpu.VMEM((q, SSM_D_INNER + DT_PAD), F32),
            pltpu.VMEM((q, SSDP_DIM), F32),
        ],
        compiler_params=_params(2),
        name="ssd",
    )(h, h, gmix, w, h0, conv0, cw, cb, dtb, alog, dexp, gn, e)


def _ret_log_decay(h):
    return float(np.log1p(-np.exp2(-5.0 - h)))


def _ret_kernel(hc_ref, hn_ref, gmix_ref, wq_ref, wk_ref, wv_ref, wg_ref, cos_ref, sin_ref, s0_ref, gn_ref,
                y_ref, sout_ref, q_scr, k_scr, v_scr, g_scr, nq_scr, nk_scr, nv_scr, ng_scr, *, s_blk, nc, n_heads):
    hb = pl.program_id(0)
    s = pl.program_id(1)
    c = pl.program_id(2)
    q = cos_ref.shape[0]
    steps = q // s_blk
    half = RET_QK_DIM // 2
    s_src = s0_ref if nc == 1 else sout_ref

    def project_next(h_ref):
        pq, pk, pv, pg = _project(h_ref, gmix_ref, [wq_ref, wk_ref, wv_ref, wg_ref])
        nq_scr[...] = pq
        nk_scr[...] = pk
        nv_scr[...] = pv.astype(BF16)
        ng_scr[...] = pg

    @pl.when(jnp.logical_and(s == 0, c == 0))
    def _first():
        project_next(hc_ref)

    if nc > 1:
        @pl.when(c == 0)
        def _init():
            sout_ref[...] = s0_ref[...]

    q_scr[...] = nq_scr[...]
    k_scr[...] = nk_scr[...]
    v_scr[...] = nv_scr[...]
    g_scr[...] = ng_scr[...]
    project_next(hn_ref)

    qq = q_scr[...]
    kk = k_scr[...]
    vv = v_scr[...]
    rg = g_scr[...]
    cos = cos_ref[...]
    sin = sin_ref[...]
    visible = _pair_mask(q, s_blk)
    r = lax.broadcasted_iota(jnp.int32, (q, q), 0)
    cc = lax.broadcasted_iota(jnp.int32, (q, q), 1)
    rel = (r // s_blk - cc // s_blk).astype(F32)
    t_qk, seq_qk = _row_time_seq((q, RET_QK_DIM), s_blk)
    t_qk = t_qk.astype(F32)

    def rot(t, h):
        t1 = t[:, h * RET_QK_DIM:h * RET_QK_DIM + half]
        t2 = t[:, h * RET_QK_DIM + half:(h + 1) * RET_QK_DIM]
        return jnp.concatenate([t1 * cos - t2 * sin, t1 * sin + t2 * cos], axis=1)

    y_parts = []
    for h in range(n_heads):
        if n_heads == RET_HEADS:
            lg = _ret_log_decay(h)
            chunk_dec = math.exp(steps * lg)
        else:
            assert n_heads == 1
            lg = jnp.float32(_ret_log_decay(0))
            chunk_dec = jnp.float32(math.exp(steps * _ret_log_decay(0)))
            for hh in range(1, RET_HEADS):
                lg = jnp.where(hb == hh, jnp.float32(_ret_log_decay(hh)), lg)
                chunk_dec = jnp.where(hb == hh, jnp.float32(math.exp(steps * _ret_log_decay(hh))), chunk_dec)
        dmat = jnp.where(visible, jnp.exp(jnp.maximum(rel, 0.0) * lg), 0.0)
        q_dec = jnp.exp((t_qk + 1.0) * lg)
        k_dec = jnp.exp((steps - 1.0 - t_qk) * lg)
        qr = rot(qq, h)
        kr = rot(kk, h) * (RET_QK_DIM ** -0.5)
        vh = vv[:, h * RET_V_DIM:(h + 1) * RET_V_DIM]
        scores = _dot_nt(qr, kr) * dmat
        y = _dot(scores, vh)
        qd = qr * q_dec
        kd = kr * k_dec
        r0 = h * RET_QK_DIM
        for b in range(s_blk):
            qd_b = (qd if s_blk == 1 else jnp.where(seq_qk == b, qd, 0.0)).astype(BF16)
            kd_b = (kd if s_blk == 1 else jnp.where(seq_qk == b, kd, 0.0)).astype(BF16)
            sh = s_src[b, r0:r0 + RET_QK_DIM, :]
            y = y + _dot(qd_b, sh)
            sout_ref[b, r0:r0 + RET_QK_DIM, :] = sh * chunk_dec + _dot_tn(kd_b, vh)
        g = gn_ref[:, h * RET_V_DIM:(h + 1) * RET_V_DIM]
        y = _rms(y, g, NORM_EPS) * _silu(rg[:, h * RET_V_DIM:(h + 1) * RET_V_DIM])
        y_parts.append(y)
    y = y_parts[0] if n_heads == 1 else jnp.concatenate(y_parts, axis=1)
    y_ref[...] = y.reshape(y_ref.shape).astype(y_ref.dtype)


def _ret(h, gmix, wq, wk, wv, wg, cos, sin, s0, gn, *, nblk, nc, q, s_blk, n_heads, time_major):
    nhb = RET_HEADS // n_heads
    qk_w = n_heads * RET_QK_DIM
    v_w = n_heads * RET_V_DIM
    rows = functools.partial(_rows_spec, q, s_blk, nblk, nc, D_MODEL, time_major, lead_axes=1)

    def head_cols(nrows, width):
        if nhb == 1:
            return _const_spec((nrows, width))
        return pl.BlockSpec((nrows, width), lambda hb, s, c: (0, hb))

    if time_major:
        y_spec = pl.BlockSpec((q // s_blk, s_blk, v_w), lambda hb, s, c: (0, s, hb))
        y_shape = (q // s_blk, nblk * s_blk, RET_V)
    else:
        y_spec = pl.BlockSpec((q, v_w), lambda hb, s, c: (s * nc + c, hb))
        y_shape = (nblk * nc * q, RET_V)
    return pl.pallas_call(
        functools.partial(_ret_kernel, s_blk=s_blk, nc=nc, n_heads=n_heads),
        grid=(nhb, nblk, nc),
        in_specs=[
            rows(),
            rows(ahead=1),
            _const_spec((1, D_MODEL)),
            head_cols(D_MODEL, qk_w),
            head_cols(D_MODEL, qk_w),
            head_cols(D_MODEL, v_w),
            head_cols(D_MODEL, v_w),
            pl.BlockSpec((q, V7X_LANES), lambda hb, s, c: (c, 0)),
            pl.BlockSpec((q, V7X_LANES), lambda hb, s, c: (c, 0)),
            pl.BlockSpec((s_blk, qk_w, RET_V_DIM), lambda hb, s, c: (s, hb, 0)),
            head_cols(1, v_w),
        ],
        out_specs=[
            y_spec,
            pl.BlockSpec((s_blk, qk_w, RET_V_DIM), lambda hb, s, c: (s, hb, 0)),
        ],
        out_shape=[
            jax.ShapeDtypeStruct(y_shape, BF16),
            jax.ShapeDtypeStruct((nblk * s_blk, RET_HEADS * RET_QK_DIM, RET_V_DIM), F32),
        ],
        scratch_shapes=[
            pltpu.VMEM((q, qk_w), F32),
            pltpu.VMEM((q, qk_w), F32),
            pltpu.VMEM((q, v_w), BF16),
            pltpu.VMEM((q, v_w), F32),
        ] * 2,
        compiler_params=_params(3),
        name="ret",
    )(h, h, gmix, wq, wk, wv, wg, cos, sin, s0, gn)


def _merge_kernel(h_ref, ys_ref, yr_ref, g_ref, wg_ref, wbs_ref, wbr_ref, wo_ref, o_ref):
    h = h_ref[...]
    u = _rms(h, g_ref[...], NORM_EPS).astype(BF16)
    gates = jnp.dot(u, wg_ref[...], preferred_element_type=F32)
    branch_ssm = jnp.dot(ys_ref[...], wbs_ref[...], preferred_element_type=F32)
    branch_ret = jnp.dot(yr_ref[...], wbr_ref[...], preferred_element_type=F32)
    merged = _sigmoid(gates[:, 0:D_MODEL]) * branch_ssm + _sigmoid(gates[:, D_MODEL:2 * D_MODEL]) * branch_ret
    o_ref[...] = h + jnp.dot(merged.astype(BF16), wo_ref[...], preferred_element_type=F32)


def _merge(h, ys, yr, g, wg, wbs, wbr, wo, *, tm):
    t = h.shape[0]
    assert t % tm == 0
    return pl.pallas_call(
        _merge_kernel,
        grid=(t // tm,),
        in_specs=[
            pl.BlockSpec((tm, D_MODEL), lambda i: (i, 0)),
            pl.BlockSpec((tm, SSM_D_INNER), lambda i: (i, 0)),
            pl.BlockSpec((tm, RET_V), lambda i: (i, 0)),
            _const_spec((1, D_MODEL)),
            _const_spec((D_MODEL, 2 * D_MODEL)),
            _const_spec((SSM_D_INNER, D_MODEL)),
            _const_spec((RET_V, D_MODEL)),
            _const_spec((D_MODEL, D_MODEL)),
        ],
        out_specs=pl.BlockSpec((tm, D_MODEL), lambda i: (i, 0)),
        out_shape=jax.ShapeDtypeStruct((t, D_MODEL), F32),
        compiler_params=_params(1),
        name="merge",
    )(h, ys, yr, g, wg, wbs, wbr, wo)


FFN_TILE = 512
SAMPLE_SEQS_PER_STEP = V7X_SUBLANES
SAMPLE_RET_HEADS_PER_STEP = 1


def _layer(x2, pos0, ssm0, conv0, ret0, w, final_gain, *, nblk, nc, q, s_blk, ret_heads, time_major):
    t = x2.shape[0]
    tm_ffn = min(FFN_TILE, t)
    steps = q // s_blk
    nseq = nblk * s_blk

    h = _ffn(x2, w["norm_ffn1"], w["ffn1_w1"], w["ffn1_w3"], w["ffn1_w2"], final_gain, final_norm=False, tm=tm_ffn)
    h_rows = h.reshape(steps, nseq, D_MODEL) if time_major else h

    y_ssm, ssm_new, conv_new = _ssd(
        h_rows, w["norm_mix"], w["w_ssd"], ssm0, conv0, w["conv_w"], w["conv_b"], w["dt_bias"], w["a_log"],
        w["d_exp"], w["ssm_norm"], w["head_expand"], nblk=nblk, nc=nc, q=q, s_blk=s_blk, time_major=time_major)
    cos, sin = _rope_tables(w["rope_inv"], nc * q, pos0, s_blk)
    y_ret, ret_new = _ret(h_rows, w["norm_mix"], w["w_q"], w["w_k"], w["w_v"], w["w_rg"], cos, sin, ret0,
                          w["ret_norm"], nblk=nblk, nc=nc, q=q, s_blk=s_blk, n_heads=ret_heads,
                          time_major=time_major)

    h = _merge(h, y_ssm.reshape(t, SSM_D_INNER), y_ret.reshape(t, RET_V), w["norm_mix"], w["w_gates"],
               w["w_branch_ssm"], w["w_branch_ret"], w["w_out"], tm=tm_ffn)
    y = _ffn(h, w["norm_ffn2"], w["ffn2_w1"], w["ffn2_w3"], w["ffn2_w2"], final_gain, final_norm=True, tm=tm_ffn)
    return y, ssm_new, conv_new, ret_new


def _prompt_layer(x, w, final_gain):
    nseq, seq_len, _ = x.shape
    q = math.gcd(seq_len, CHUNK)
    y, ssm_new, conv_new, ret_new = _layer(
        x.reshape(nseq * seq_len, D_MODEL), 0.0,
        jnp.zeros((nseq, SSM_HEADS * SSM_HEAD_DIM, SSM_STATE), F32),
        jnp.zeros((nseq, SSM_CONV - 1, SSM_CONV_DIM), F32),
        jnp.zeros((nseq, RET_HEADS * RET_QK_DIM, RET_V_DIM), F32),
        w, final_gain, nblk=nseq, nc=seq_len // q, q=q, s_blk=1, ret_heads=RET_HEADS, time_major=False)
    return (y.reshape(nseq, seq_len, D_MODEL), ssm_new.reshape(nseq, SSM_HEADS, SSM_HEAD_DIM, SSM_STATE), conv_new,
            ret_new.reshape(nseq, RET_HEADS, RET_QK_DIM, RET_V_DIM))


def _sample_layer(x, ssm0, conv0, ret0, w, final_gain):
    nseq, seq_len, _ = x.shape
    s_blk = SAMPLE_SEQS_PER_STEP
    assert nseq % s_blk == 0 and seq_len <= CHUNK
    nblk = nseq // s_blk
    kc = SSM_CONV - 1
    x_tm = jnp.transpose(x, (1, 0, 2)).reshape(seq_len * nseq, D_MODEL)
    conv0_tm = jnp.transpose(conv0.reshape(nblk, s_blk, kc, SSM_CONV_DIM), (0, 2, 1, 3)).reshape(
        nblk, kc * s_blk, SSM_CONV_DIM)
    y, ssm_new, conv_new, ret_new = _layer(
        x_tm, float(PAST_LEN), ssm0.reshape(nseq, SSM_HEADS * SSM_HEAD_DIM, SSM_STATE), conv0_tm,
        ret0.reshape(nseq, RET_HEADS * RET_QK_DIM, RET_V_DIM), w, final_gain,
        nblk=nblk, nc=1, q=s_blk * seq_len, s_blk=s_blk, ret_heads=SAMPLE_RET_HEADS_PER_STEP, time_major=True)
    conv_new = jnp.transpose(conv_new.reshape(nblk, kc, s_blk, SSM_CONV_DIM), (0, 2, 1, 3)).reshape(
        nseq, kc, SSM_CONV_DIM)
    return (jnp.transpose(y.reshape(seq_len, nseq, D_MODEL), (1, 0, 2)),
            ssm_new.reshape(nseq, SSM_HEADS, SSM_HEAD_DIM, SSM_STATE), conv_new,
            ret_new.reshape(nseq, RET_HEADS, RET_QK_DIM, RET_V_DIM))


def _prep_weights(norm_ffn1, ffn1_w1, ffn1_w3, ffn1_w2, norm_mix, w_in, conv_w, conv_b, dt_bias, a_log, ssm_d,
                  ssm_norm, ret_norm, w_branch_ssm, w_branch_ret, w_out, norm_ffn2, ffn2_w1, ffn2_w3, ffn2_w2):
    o_z = 0
    o_xbc = o_z + SSM_D_INNER
    o_dt = o_xbc + SSM_CONV_DIM
    o_q = o_dt + SSM_HEADS
    o_k = o_q + RET_QK
    o_v = o_k + RET_QK
    o_rg = o_v + RET_V
    o_ga = o_rg + RET_V
    o_end = o_ga + 2 * D_MODEL
    assert o_end == w_in.shape[1]

    def cols(a, b):
        return w_in[:, a:b]

    assert (o_z, o_xbc, o_dt) == (SSDP_Z, SSDP_XBC, SSDP_DT)
    w_ssd = jnp.pad(cols(o_z, o_q), ((0, 0), (0, DT_PAD - SSM_HEADS))).astype(BF16)
    half = RET_QK_DIM // 2
    head_of_channel = jnp.arange(SSM_D_INNER, dtype=jnp.int32) // SSM_HEAD_DIM
    return {
        "norm_ffn1": norm_ffn1.reshape(1, -1), "ffn1_w1": ffn1_w1.astype(BF16), "ffn1_w3": ffn1_w3.astype(BF16),
        "ffn1_w2": ffn1_w2.astype(BF16),
        "norm_mix": norm_mix.reshape(1, -1), "w_ssd": w_ssd,
        "w_q": cols(o_q, o_k).astype(BF16), "w_k": cols(o_k, o_v).astype(BF16),
        "w_v": cols(o_v, o_rg).astype(BF16), "w_rg": cols(o_rg, o_ga).astype(BF16),
        "w_gates": cols(o_ga, o_end).astype(BF16),
        "conv_w": conv_w, "conv_b": conv_b.reshape(1, -1),
        "dt_bias": jnp.pad(dt_bias, (0, DT_PAD - SSM_HEADS)).reshape(1, -1),
        "a_log": jnp.pad(a_log, (0, DT_PAD - SSM_HEADS)).reshape(1, -1),
        "d_exp": jnp.repeat(ssm_d, SSM_HEAD_DIM).reshape(1, -1),
        "ssm_norm": ssm_norm.reshape(1, -1), "ret_norm": ret_norm.reshape(1, -1),
        "head_expand": (head_of_channel[None, :] == jnp.arange(DT_PAD, dtype=jnp.int32)[:, None]).astype(BF16),
        "rope_inv": (ROPE_BASE ** (-jnp.arange(half, dtype=F32) / half)).reshape(1, half),
        "w_branch_ssm": w_branch_ssm.astype(BF16), "w_branch_ret": w_branch_ret.astype(BF16),
        "w_out": w_out.astype(BF16),
        "norm_ffn2": norm_ffn2.reshape(1, -1), "ffn2_w1": ffn2_w1.astype(BF16), "ffn2_w3": ffn2_w3.astype(BF16),
        "ffn2_w2": ffn2_w2.astype(BF16),
    }


def kernel(x_prompt, x_sample, state_ssm, state_conv, state_ret, norm_ffn1, ffn1_w1, ffn1_w3, ffn1_w2, norm_mix, w_in,
           conv_w, conv_b, dt_bias, a_log, ssm_d, ssm_norm, ret_norm, w_branch_ssm, w_branch_ret, w_out, norm_ffn2,
           ffn2_w1, ffn2_w3, ffn2_w2, norm_final):
    depth = norm_ffn1.shape[0]
    assert depth == 1, "the final RMSNorm is fused into the last layer's second FFN"
    w = _prep_weights(norm_ffn1[0], ffn1_w1[0], ffn1_w3[0], ffn1_w2[0], norm_mix[0], w_in[0], conv_w[0], conv_b[0],
                      dt_bias[0], a_log[0], ssm_d[0], ssm_norm[0], ret_norm[0], w_branch_ssm[0], w_branch_ret[0],
                      w_out[0], norm_ffn2[0], ffn2_w1[0], ffn2_w3[0], ffn2_w2[0])
    final_gain = norm_final.reshape(1, -1)
    yp, ssm_p, conv_p, ret_p = _prompt_layer(x_prompt, w, final_gain)
    ys, ssm_s, conv_s, ret_s = _sample_layer(x_sample, state_ssm[0], state_conv[0], state_ret[0], w, final_gain)
    return (yp, ys, ssm_p[None], conv_p[None], ret_p[None], ssm_s[None], conv_s[None], ret_s[None])
```

```python
import functools
import math

import jax
import jax.numpy as jnp
import numpy as np
from jax import lax
from jax.experimental import pallas as pl
from jax.experimental.pallas import tpu as pltpu

F32 = jnp.float32
BF16 = jnp.bfloat16

D_MODEL = 1024
D_FF = 2816
SSM_D_INNER = 2048
SSM_HEAD_DIM = 64
SSM_HEADS = 32
SSM_GROUPS = 4
SSM_STATE = 128
SSM_CONV = 4
SSM_BC = 2 * SSM_GROUPS * SSM_STATE
SSM_CONV_DIM = SSM_D_INNER + SSM_BC
RET_HEADS = 4
RET_QK_DIM = 256
RET_V_DIM = 512
RET_QK = RET_HEADS * RET_QK_DIM
RET_V = RET_HEADS * RET_V_DIM
ROPE_BASE = 10000.0
PAST_LEN = 16384
CHUNK = 128
NORM_EPS = 1e-6
GATED_NORM_EPS = 1e-5

V7X_SUBLANES = 8
V7X_LANES = 128
V7X_VMEM_BYTES = 64 * 1024 * 1024
VMEM_LIMIT = V7X_VMEM_BYTES - 8 * 1024 * 1024

DT_PAD = V7X_LANES
LOG_DT_FLOOR = -1e30
LOG2_E = math.log2(math.e)

SSDP_Z = 0
SSDP_XBC = SSM_D_INNER
SSDP_DT = SSM_D_INNER + SSM_CONV_DIM
SSDP_DIM = SSDP_DT + DT_PAD


def _rms(x, g, eps):
    return x * lax.rsqrt(jnp.mean(x * x, axis=-1, keepdims=True) + eps) * g


def _sigmoid(x):
    return 0.5 + 0.5 * jnp.tanh(0.5 * x)


def _silu(x):
    half = 0.5 * x
    return half + half * jnp.tanh(half)


def _softplus(x):
    return jnp.maximum(x, 0.0) + jnp.log1p(jnp.exp(-jnp.abs(x)))


def _dot(a, b):
    return jnp.dot(a.astype(BF16), b.astype(BF16), preferred_element_type=F32)


def _dot_nt(a, b):
    return lax.dot_general(a.astype(BF16), b.astype(BF16), (((1,), (1,)), ((), ())), preferred_element_type=F32)


def _dot_tn(a, b):
    return lax.dot_general(a.astype(BF16), b.astype(BF16), (((0,), (0,)), ((), ())), preferred_element_type=F32)


def _dot_split(a, b_f32, passes):
    acc = None
    rem = b_f32
    for _ in range(passes):
        piece = rem.astype(BF16)
        term = jnp.dot(a, piece, preferred_element_type=F32)
        acc = term if acc is None else acc + term
        rem = rem - piece.astype(F32)
    return acc


def _dot_split_lhs(a_f32, b, passes):
    acc = None
    rem = a_f32
    for _ in range(passes):
        piece = rem.astype(BF16)
        term = jnp.dot(piece, b, preferred_element_type=F32)
        acc = term if acc is None else acc + term
        rem = rem - piece.astype(F32)
    return acc


def _transpose_split(eye, x_f32, passes):
    acc = None
    rem = x_f32
    for _ in range(passes):
        piece = rem.astype(BF16)
        term = lax.dot_general(eye, piece, (((1,), (1,)), ((), ())), preferred_element_type=F32)
        acc = term if acc is None else acc + term
        rem = rem - piece.astype(F32)
    return acc


def _const_spec(shape):
    nd = len(shape)
    return pl.BlockSpec(shape, lambda *_: (0,) * nd, pipeline_mode=pl.Buffered(1))


def _params(n_grid_dims, flags=None):
    return pltpu.CompilerParams(dimension_semantics=("arbitrary",) * n_grid_dims, vmem_limit_bytes=VMEM_LIMIT,
                                flags=flags)


def _ffn_kernel(x_ref, g_ref, w1_ref, w3_ref, w2_ref, gf_ref, o_ref, *, final_norm):
    x = x_ref[...]
    xn = _rms(x, g_ref[...], NORM_EPS).astype(BF16)
    a = jnp.dot(xn, w1_ref[...], preferred_element_type=F32)
    b = jnp.dot(xn, w3_ref[...], preferred_element_type=F32)
    gated = (_silu(a) * b).astype(BF16)
    h = x + 0.5 * jnp.dot(gated, w2_ref[...], preferred_element_type=F32)
    if final_norm:
        h = _rms(h, gf_ref[...], NORM_EPS)
    o_ref[...] = h


def _ffn(x, g, w1, w3, w2, gf, *, final_norm, tm):
    t = x.shape[0]
    assert t % tm == 0
    return pl.pallas_call(
        functools.partial(_ffn_kernel, final_norm=final_norm),
        grid=(t // tm,),
        in_specs=[
            pl.BlockSpec((tm, D_MODEL), lambda i: (i, 0)),
            _const_spec((1, D_MODEL)),
            _const_spec((D_MODEL, D_FF)),
            _const_spec((D_MODEL, D_FF)),
            _const_spec((D_FF, D_MODEL)),
            _const_spec((1, D_MODEL)),
        ],
        out_specs=pl.BlockSpec((tm, D_MODEL), lambda i: (i, 0)),
        out_shape=jax.ShapeDtypeStruct((t, D_MODEL), F32),
        compiler_params=_params(1),
        name="ffn_final" if final_norm else "ffn",
    )(x, g, w1, w3, w2, gf)


def _rope_kernel(inv_ref, cos_ref, sin_ref, *, pos0, rows_per_pos):
    rows = cos_ref.shape[0]
    step = lax.broadcasted_iota(jnp.int32, (rows, V7X_LANES), 0) // rows_per_pos
    ang = (pos0 + step.astype(F32)) * inv_ref[...]
    cos_ref[...] = jnp.cos(ang)
    sin_ref[...] = jnp.sin(ang)


def _rope_tables(inv, rows, pos0, rows_per_pos):
    return pl.pallas_call(
        functools.partial(_rope_kernel, pos0=pos0, rows_per_pos=rows_per_pos),
        out_shape=[jax.ShapeDtypeStruct((rows, V7X_LANES), F32)] * 2,
        name="rope",
    )(inv)


def _chunk_rows(ref):
    return ref[...].reshape(-1, ref.shape[-1])


def _row_time_seq(shape, s_blk):
    r = lax.broadcasted_iota(jnp.int32, shape, 0)
    return r // s_blk, r % s_blk


def _pair_mask(q, s_blk):
    r = lax.broadcasted_iota(jnp.int32, (q, q), 0)
    c = lax.broadcasted_iota(jnp.int32, (q, q), 1)
    if s_blk == 1:
        return r >= c
    return jnp.logical_and(r // s_blk >= c // s_blk, r % s_blk == c % s_blk)


def _rows_spec(q, s_blk, nblk, nc, width, time_major, *, ahead=0, lead_axes=0):
    last = nblk * nc - 1

    def chunk_index(args):
        s, c = args[lead_axes], args[lead_axes + 1]
        return jnp.minimum(s * nc + c + ahead, last)

    if time_major:
        assert nc == 1
        return pl.BlockSpec((q // s_blk, s_blk, width), lambda *a: (0, chunk_index(a), 0))
    return pl.BlockSpec((q, width), lambda *a: (chunk_index(a), 0))


def _project(h_ref, g_ref, w_refs):
    u = _rms(_chunk_rows(h_ref), g_ref[...], NORM_EPS).astype(BF16)
    return [jnp.dot(u, w_ref[...], preferred_element_type=F32) for w_ref in w_refs]


def _conv_pad_rows(s_blk):
    need = (SSM_CONV - 1) * s_blk
    return -(-need // V7X_SUBLANES) * V7X_SUBLANES


def _ssd_kernel(hc_ref, hn_ref, gmix_ref, w_ref, h0_ref, conv0_ref, cw_ref, cb_ref, dtb_ref, alog_ref, dexp_ref,
                gn_ref, e_ref, y_ref, hout_ref, convout_ref, xp_scr, zdt_scr, next_scr, *, s_blk, nc):
    s = pl.program_id(0)
    c = pl.program_id(1)
    pad = _conv_pad_rows(s_blk)
    q = xp_scr.shape[0] - pad
    carry = (SSM_CONV - 1) * s_blk
    steps = q // s_blk
    hg = SSM_HEADS // SSM_GROUPS
    gw = hg * SSM_HEAD_DIM
    h_src = h0_ref if nc == 1 else hout_ref

    @pl.when(jnp.logical_and(s == 0, c == 0))
    def _first():
        next_scr[...] = _project(hc_ref, gmix_ref, [w_ref])[0]

    @pl.when(c == 0)
    def _init():
        xp_scr[0:pad, :] = jnp.zeros((pad, SSM_CONV_DIM), F32)
        xp_scr[pad - carry:pad, :] = conv0_ref[0]
        if nc > 1:
            hout_ref[...] = h0_ref[...]

    xp_scr[pad:pad + q, :] = next_scr[:, SSDP_XBC:SSDP_DT]
    zdt_scr[:, 0:SSM_D_INNER] = next_scr[:, SSDP_Z:SSDP_XBC]
    zdt_scr[:, SSM_D_INNER:SSM_D_INNER + DT_PAD] = next_scr[:, SSDP_DT:SSDP_DIM]
    next_scr[...] = _project(hn_ref, gmix_ref, [w_ref])[0]

    xall = xp_scr[...]
    conv = cb_ref[...] + xall[pad:pad + q, :] * cw_ref[SSM_CONV - 1:SSM_CONV, :]
    for j in range(1, SSM_CONV):
        d = j * s_blk
        if d % V7X_SUBLANES == 0:
            back = xall[pad - d:pad - d + q, :]
        else:
            back = pltpu.roll(xall, d, 0)[pad:pad + q, :]
        conv = conv + back * cw_ref[SSM_CONV - 1 - j:SSM_CONV - j, :]
    @pl.when(c == nc - 1)
    def _():
        convout_ref[0] = xp_scr[pad + q - carry:pad + q, :]

    if nc > 1:
        xp_scr[0:pad, :] = xp_scr[q:q + pad, :]
    z_rows = zdt_scr[:, 0:SSM_D_INNER]
    dt_rows = zdt_scr[:, SSM_D_INNER:SSM_D_INNER + DT_PAD]
    xc = _silu(conv)
    xs = xc[:, 0:SSM_D_INNER]
    bm = xc[:, SSM_D_INNER:SSM_D_INNER + SSM_GROUPS * SSM_STATE]
    cm = xc[:, SSM_D_INNER + SSM_GROUPS * SSM_STATE:SSM_CONV_DIM]

    dt = _softplus(dt_rows + dtb_ref[...])
    a = -jnp.exp(alog_ref[...])
    visible = _pair_mask(q, s_blk)
    la = _dot_split(jnp.where(visible, 1.0, 0.0).astype(BF16), dt * (a * LOG2_E), 3)
    lsd = la - jnp.maximum(jnp.log2(dt), LOG_DT_FLOOR)
    eye_r = lax.broadcasted_iota(jnp.int32, (DT_PAD, DT_PAD), 0)
    eye_c = lax.broadcasted_iota(jnp.int32, (DT_PAD, DT_PAD), 1)
    eye = jnp.where(eye_r == eye_c, 1.0, 0.0).astype(BF16)
    la_t = _transpose_split(eye, la, 3)
    lsd_t = _transpose_split(eye, lsd, 3)
    la_last = la[q - s_blk:q, :]
    la_last_rows = la_last if s_blk == 1 else jnp.concatenate([la_last] * steps, axis=0)
    dec_t = jnp.exp2(la_t[:, q - s_blk:q])

    lane = lax.broadcasted_iota(jnp.int32, (q, 2 * SSM_HEAD_DIM), 1)
    first_head = lane < SSM_HEAD_DIM
    y_parts = []
    for g in range(SSM_GROUPS):
        cg = cm[:, g * SSM_STATE:(g + 1) * SSM_STATE]
        bg = bm[:, g * SSM_STATE:(g + 1) * SSM_STATE]
        cbg = _dot_nt(cg, bg)
        for pair in range(hg // 2):
            h0 = g * hg + 2 * pair
            ws = []
            for h in (h0, h0 + 1):
                seg = la[:, h:h + 1] - lsd_t[h:h + 1, :]
                ws.append(cbg * jnp.exp2(jnp.where(visible, seg, -jnp.inf)))
            xpair = xs[:, h0 * SSM_HEAD_DIM:(h0 + 2) * SSM_HEAD_DIM]
            rhs = jnp.concatenate([jnp.where(first_head, xpair, 0.0), jnp.where(first_head, 0.0, xpair)], axis=0)
            y_parts.append(_dot(jnp.concatenate(ws, axis=1), rhs))
    y = jnp.concatenate(y_parts, axis=1)

    _, seq_bc = _row_time_seq((q, SSM_BC // 2), s_blk)
    _, seq_x = _row_time_seq((q, SSM_D_INNER), s_blk)
    y_state = None
    for b in range(s_blk):
        cm_b = (cm if s_blk == 1 else jnp.where(seq_bc == b, cm, 0.0)).astype(BF16)
        part = jnp.concatenate(
            [_dot_nt(cm_b[:, g * SSM_STATE:(g + 1) * SSM_STATE], h_src[b, g * gw:(g + 1) * gw, :])
             for g in range(SSM_GROUPS)], axis=1)
        y_state = part if y_state is None else y_state + part

    e = e_ref[...]
    y = y + y_state * _dot_split_lhs(jnp.exp2(la), e, 2) + xs * dexp_ref[...]
    y = y * _silu(z_rows)
    y = _rms(y, gn_ref[...], GATED_NORM_EPS)
    y_ref[...] = y.reshape(y_ref.shape).astype(y_ref.dtype)

    tail = jnp.exp2(la_last_rows - lsd)
    xt = xs * _dot_split_lhs(tail, e, 2)
    bm_bf = bm.astype(BF16)
    for b in range(s_blk):
        xt_b = (xt if s_blk == 1 else jnp.where(seq_x == b, xt, 0.0)).astype(BF16)
        dec_b = jnp.broadcast_to(dec_t[:, b:b + 1], (DT_PAD, SSM_STATE))
        for g in range(SSM_GROUPS):
            upd = _dot_tn(xt_b[:, g * gw:(g + 1) * gw], bm_bf[:, g * SSM_STATE:(g + 1) * SSM_STATE])
            for hh in range(hg):
                h = g * hg + hh
                r0 = h * SSM_HEAD_DIM
                hout_ref[b, r0:r0 + SSM_HEAD_DIM, :] = (
                    h_src[b, r0:r0 + SSM_HEAD_DIM, :] * dec_b[h:h + 1, :]
                    + upd[hh * SSM_HEAD_DIM:(hh + 1) * SSM_HEAD_DIM, :])


def _ssd(h, gmix, w, h0, conv0, cw, cb, dtb, alog, dexp, gn, e, *, nblk, nc, q, s_blk, time_major):
    rows = SSM_HEADS * SSM_HEAD_DIM
    carry = (SSM_CONV - 1) * s_blk
    spec = functools.partial(_rows_spec, q, s_blk, nblk, nc, time_major=time_major)
    y_shape = (q // s_blk, nblk * s_blk, SSM_D_INNER) if time_major else (nblk * nc * q, SSM_D_INNER)
    return pl.pallas_call(
        functools.partial(_ssd_kernel, s_blk=s_blk, nc=nc),
        grid=(nblk, nc),
        in_specs=[
            spec(D_MODEL),
            spec(D_MODEL, ahead=1),
            _const_spec((1, D_MODEL)),
            _const_spec((D_MODEL, SSDP_DIM)),
            pl.BlockSpec((s_blk, rows, SSM_STATE), lambda s, c: (s, 0, 0)),
            pl.BlockSpec((1, carry, SSM_CONV_DIM), lambda s, c: (s, 0, 0)),
            _const_spec((SSM_CONV, SSM_CONV_DIM)),
            _const_spec((1, SSM_CONV_DIM)),
            _const_spec((1, DT_PAD)),
            _const_spec((1, DT_PAD)),
            _const_spec((1, SSM_D_INNER)),
            _const_spec((1, SSM_D_INNER)),
            _const_spec((DT_PAD, SSM_D_INNER)),
        ],
        out_specs=[
            spec(SSM_D_INNER),
            pl.BlockSpec((s_blk, rows, SSM_STATE), lambda s, c: (s, 0, 0)),
            pl.BlockSpec((1, carry, SSM_CONV_DIM), lambda s, c: (s, 0, 0)),
        ],
        out_shape=[
            jax.ShapeDtypeStruct(y_shape, BF16),
            jax.ShapeDtypeStruct((nblk * s_blk, rows, SSM_STATE), F32),
            jax.ShapeDtypeStruct((nblk, carry, SSM_CONV_DIM), F32),
        ],
        scratch_shapes=[
            pltpu.VMEM((_conv_pad_rows(s_blk) + q, SSM_CONV_DIM), F32),
            pltpu.VMEM((q, SSM_D_INNER + DT_PAD), F32),
            pltpu.VMEM((q, SSDP_DIM), F32),
        ],
        compiler_params=_params(2),
        name="ssd",
    )(h, h, gmix, w, h0, conv0, cw, cb, dtb, alog, dexp, gn, e)


def _ret_log_decay(h):
    return float(np.log1p(-np.exp2(-5.0 - h)))


def _ret_kernel(hc_ref, hn_ref, gmix_ref, wq_ref, wk_ref, wv_ref, wg_ref, cos_ref, sin_ref, s0_ref, gn_ref,
                y_ref, sout_ref, q_scr, k_scr, v_scr, g_scr, nq_scr, nk_scr, nv_scr, ng_scr, *, s_blk, nc, n_heads):
    hb = pl.program_id(0)
    s = pl.program_id(1)
    c = pl.program_id(2)
    q = cos_ref.shape[0]
    steps = q // s_blk
    half = RET_QK_DIM // 2
    s_src = s0_ref if nc == 1 else sout_ref

    def project_next(h_ref):
        pq, pk, pv, pg = _project(h_ref, gmix_ref, [wq_ref, wk_ref, wv_ref, wg_ref])
        nq_scr[...] = pq
        nk_scr[...] = pk
        nv_scr[...] = pv.astype(BF16)
        ng_scr[...] = pg

    @pl.when(jnp.logical_and(s == 0, c == 0))
    def _first():
        project_next(hc_ref)

    if nc > 1:
        @pl.when(c == 0)
        def _init():
            sout_ref[...] = s0_ref[...]

    q_scr[...] = nq_scr[...]
    k_scr[...] = nk_scr[...]
    v_scr[...] = nv_scr[...]
    g_scr[...] = ng_scr[...]
    project_next(hn_ref)

    qq = q_scr[...]
    kk = k_scr[...]
    vv = v_scr[...]
    rg = g_scr[...]
    cos = cos_ref[...]
    sin = sin_ref[...]
    visible = _pair_mask(q, s_blk)
    r = lax.broadcasted_iota(jnp.int32, (q, q), 0)
    cc = lax.broadcasted_iota(jnp.int32, (q, q), 1)
    rel = (r // s_blk - cc // s_blk).astype(F32)
    t_qk, seq_qk = _row_time_seq((q, RET_QK_DIM), s_blk)
    t_qk = t_qk.astype(F32)

    def rot(t, h):
        t1 = t[:, h * RET_QK_DIM:h * RET_QK_DIM + half]
        t2 = t[:, h * RET_QK_DIM + half:(h + 1) * RET_QK_DIM]
        return jnp.concatenate([t1 * cos - t2 * sin, t1 * sin + t2 * cos], axis=1)

    y_parts = []
    for h in range(n_heads):
        if n_heads == RET_HEADS:
            lg = _ret_log_decay(h)
            chunk_dec = math.exp(steps * lg)
        else:
            assert n_heads == 1
            lg = jnp.float32(_ret_log_decay(0))
            chunk_dec = jnp.float32(math.exp(steps * _ret_log_decay(0)))
            for hh in range(1, RET_HEADS):
                lg = jnp.where(hb == hh, jnp.float32(_ret_log_decay(hh)), lg)
                chunk_dec = jnp.where(hb == hh, jnp.float32(math.exp(steps * _ret_log_decay(hh))), chunk_dec)
        dmat = jnp.where(visible, jnp.exp(jnp.maximum(rel, 0.0) * lg), 0.0)
        q_dec = jnp.exp((t_qk + 1.0) * lg)
        k_dec = jnp.exp((steps - 1.0 - t_qk) * lg)
        qr = rot(qq, h)
        kr = rot(kk, h) * (RET_QK_DIM ** -0.5)
        vh = vv[:, h * RET_V_DIM:(h + 1) * RET_V_DIM]
        scores = _dot_nt(qr, kr) * dmat
        y = _dot(scores, vh)
        qd = qr * q_dec
        kd = kr * k_dec
        r0 = h * RET_QK_DIM
        for b in range(s_blk):
            qd_b = (qd if s_blk == 1 else jnp.where(seq_qk == b, qd, 0.0)).astype(BF16)
            kd_b = (kd if s_blk == 1 else jnp.where(seq_qk == b, kd, 0.0)).astype(BF16)
            sh = s_src[b, r0:r0 + RET_QK_DIM, :]
            y = y + _dot(qd_b, sh)
            sout_ref[b, r0:r0 + RET_QK_DIM, :] = sh * chunk_dec + _dot_tn(kd_b, vh)
        g = gn_ref[:, h * RET_V_DIM:(h + 1) * RET_V_DIM]
        y = _rms(y, g, NORM_EPS) * _silu(rg[:, h * RET_V_DIM:(h + 1) * RET_V_DIM])
        y_parts.append(y)
    y = y_parts[0] if n_heads == 1 else jnp.concatenate(y_parts, axis=1)
    y_ref[...] = y.reshape(y_ref.shape).astype(y_ref.dtype)


def _ret(h, gmix, wq, wk, wv, wg, cos, sin, s0, gn, *, nblk, nc, q, s_blk, n_heads, time_major):
    nhb = RET_HEADS // n_heads
    qk_w = n_heads * RET_QK_DIM
    v_w = n_heads * RET_V_DIM
    rows = functools.partial(_rows_spec, q, s_blk, nblk, nc, D_MODEL, time_major, lead_axes=1)

    def head_cols(nrows, width):
        if nhb == 1:
            return _const_spec((nrows, width))
        return pl.BlockSpec((nrows, width), lambda hb, s, c: (0, hb))

    if time_major:
        y_spec = pl.BlockSpec((q // s_blk, s_blk, v_w), lambda hb, s, c: (0, s, hb))
        y_shape = (q // s_blk, nblk * s_blk, RET_V)
    else:
        y_spec = pl.BlockSpec((q, v_w), lambda hb, s, c: (s * nc + c, hb))
        y_shape = (nblk * nc * q, RET_V)
    return pl.pallas_call(
        functools.partial(_ret_kernel, s_blk=s_blk, nc=nc, n_heads=n_heads),
        grid=(nhb, nblk, nc),
        in_specs=[
            rows(),
            rows(ahead=1),
            _const_spec((1, D_MODEL)),
            head_cols(D_MODEL, qk_w),
            head_cols(D_MODEL, qk_w),
            head_cols(D_MODEL, v_w),
            head_cols(D_MODEL, v_w),
            pl.BlockSpec((q, V7X_LANES), lambda hb, s, c: (c, 0)),
            pl.BlockSpec((q, V7X_LANES), lambda hb, s, c: (c, 0)),
            pl.BlockSpec((s_blk, qk_w, RET_V_DIM), lambda hb, s, c: (s, hb, 0)),
            head_cols(1, v_w),
        ],
        out_specs=[
            y_spec,
            pl.BlockSpec((s_blk, qk_w, RET_V_DIM), lambda hb, s, c: (s, hb, 0)),
        ],
        out_shape=[
            jax.ShapeDtypeStruct(y_shape, BF16),
            jax.ShapeDtypeStruct((nblk * s_blk, RET_HEADS * RET_QK_DIM, RET_V_DIM), F32),
        ],
        scratch_shapes=[
            pltpu.VMEM((q, qk_w), F32),
            pltpu.VMEM((q, qk_w), F32),
            pltpu.VMEM((q, v_w), BF16),
            pltpu.VMEM((q, v_w), F32),
        ] * 2,
        compiler_params=_params(3),
        name="ret",
    )(h, h, gmix, wq, wk, wv, wg, cos, sin, s0, gn)


def _merge_kernel(h_ref, ys_ref, yr_ref, g_ref, wg_ref, wbs_ref, wbr_ref, wo_ref, o_ref):
    h = h_ref[...]
    u = _rms(h, g_ref[...], NORM_EPS).astype(BF16)
    gates = jnp.dot(u, wg_ref[...], preferred_element_type=F32)
    branch_ssm = jnp.dot(ys_ref[...], wbs_ref[...], preferred_element_type=F32)
    branch_ret = jnp.dot(yr_ref[...], wbr_ref[...], preferred_element_type=F32)
    merged = _sigmoid(gates[:, 0:D_MODEL]) * branch_ssm + _sigmoid(gates[:, D_MODEL:2 * D_MODEL]) * branch_ret
    o_ref[...] = h + jnp.dot(merged.astype(BF16), wo_ref[...], preferred_element_type=F32)


def _merge(h, ys, yr, g, wg, wbs, wbr, wo, *, tm):
    t = h.shape[0]
    assert t % tm == 0
    return pl.pallas_call(
        _merge_kernel,
        grid=(t // tm,),
        in_specs=[
            pl.BlockSpec((tm, D_MODEL), lambda i: (i, 0)),
            pl.BlockSpec((tm, SSM_D_INNER), lambda i: (i, 0)),
            pl.BlockSpec((tm, RET_V), lambda i: (i, 0)),
            _const_spec((1, D_MODEL)),
            _const_spec((D_MODEL, 2 * D_MODEL)),
            _const_spec((SSM_D_INNER, D_MODEL)),
            _const_spec((RET_V, D_MODEL)),
            _const_spec((D_MODEL, D_MODEL)),
        ],
        out_specs=pl.BlockSpec((tm, D_MODEL), lambda i: (i, 0)),
        out_shape=jax.ShapeDtypeStruct((t, D_MODEL), F32),
        compiler_params=_params(1),
        name="merge",
    )(h, ys, yr, g, wg, wbs, wbr, wo)


FFN_TILE = 512
SAMPLE_SEQS_PER_STEP = V7X_SUBLANES
SAMPLE_RET_HEADS_PER_STEP = 1


def _layer(x2, pos0, ssm0, conv0, ret0, w, final_gain, *, nblk, nc, q, s_blk, ret_heads, time_major):
    t = x2.shape[0]
    tm_ffn = min(FFN_TILE, t)
    steps = q // s_blk
    nseq = nblk * s_blk

    h = _ffn(x2, w["norm_ffn1"], w["ffn1_w1"], w["ffn1_w3"], w["ffn1_w2"], final_gain, final_norm=False, tm=tm_ffn)
    h_rows = h.reshape(steps, nseq, D_MODEL) if time_major else h

    y_ssm, ssm_new, conv_new = _ssd(
        h_rows, w["norm_mix"], w["w_ssd"], ssm0, conv0, w["conv_w"], w["conv_b"], w["dt_bias"], w["a_log"],
        w["d_exp"], w["ssm_norm"], w["head_expand"], nblk=nblk, nc=nc, q=q, s_blk=s_blk, time_major=time_major)
    cos, sin = _rope_tables(w["rope_inv"], nc * q, pos0, s_blk)
    y_ret, ret_new = _ret(h_rows, w["norm_mix"], w["w_q"], w["w_k"], w["w_v"], w["w_rg"], cos, sin, ret0,
                          w["ret_norm"], nblk=nblk, nc=nc, q=q, s_blk=s_blk, n_heads=ret_heads,
                          time_major=time_major)

    h = _merge(h, y_ssm.reshape(t, SSM_D_INNER), y_ret.reshape(t, RET_V), w["norm_mix"], w["w_gates"],
               w["w_branch_ssm"], w["w_branch_ret"], w["w_out"], tm=tm_ffn)
    y = _ffn(h, w["norm_ffn2"], w["ffn2_w1"], w["ffn2_w3"], w["ffn2_w2"], final_gain, final_norm=True, tm=tm_ffn)
    return y, ssm_new, conv_new, ret_new


def _prompt_layer(x, w, final_gain):
    nseq, seq_len, _ = x.shape
    q = math.gcd(seq_len, CHUNK)
    y, ssm_new, conv_new, ret_new = _layer(
        x.reshape(nseq * seq_len, D_MODEL), 0.0,
        jnp.zeros((nseq, SSM_HEADS * SSM_HEAD_DIM, SSM_STATE), F32),
        jnp.zeros((nseq, SSM_CONV - 1, SSM_CONV_DIM), F32),
        jnp.zeros((nseq, RET_HEADS * RET_QK_DIM, RET_V_DIM), F32),
        w, final_gain, nblk=nseq, nc=seq_len // q, q=q, s_blk=1, ret_heads=RET_HEADS, time_major=False)
    return (y.reshape(nseq, seq_len, D_MODEL), ssm_new.reshape(nseq, SSM_HEADS, SSM_HEAD_DIM, SSM_STATE), conv_new,
            ret_new.reshape(nseq, RET_HEADS, RET_QK_DIM, RET_V_DIM))


def _sample_layer(x, ssm0, conv0, ret0, w, final_gain):
    nseq, seq_len, _ = x.shape
    s_blk = SAMPLE_SEQS_PER_STEP
    assert nseq % s_blk == 0 and seq_len <= CHUNK
    nblk = nseq // s_blk
    kc = SSM_CONV - 1
    x_tm = jnp.transpose(x, (1, 0, 2)).reshape(seq_len * nseq, D_MODEL)
    conv0_tm = jnp.transpose(conv0.reshape(nblk, s_blk, kc, SSM_CONV_DIM), (0, 2, 1, 3)).reshape(
        nblk, kc * s_blk, SSM_CONV_DIM)
    y, ssm_new, conv_new, ret_new = _layer(
        x_tm, float(PAST_LEN), ssm0.reshape(nseq, SSM_HEADS * SSM_HEAD_DIM, SSM_STATE), conv0_tm,
        ret0.reshape(nseq, RET_HEADS * RET_QK_DIM, RET_V_DIM), w, final_gain,
        nblk=nblk, nc=1, q=s_blk * seq_len, s_blk=s_blk, ret_heads=SAMPLE_RET_HEADS_PER_STEP, time_major=True)
    conv_new = jnp.transpose(conv_new.reshape(nblk, kc, s_blk, SSM_CONV_DIM), (0, 2, 1, 3)).reshape(
        nseq, kc, SSM_CONV_DIM)
    return (jnp.transpose(y.reshape(seq_len, nseq, D_MODEL), (1, 0, 2)),
            ssm_new.reshape(nseq, SSM_HEADS, SSM_HEAD_DIM, SSM_STATE), conv_new,
            ret_new.reshape(nseq, RET_HEADS, RET_QK_DIM, RET_V_DIM))


def _prep_weights(norm_ffn1, ffn1_w1, ffn1_w3, ffn1_w2, norm_mix, w_in, conv_w, conv_b, dt_bias, a_log, ssm_d,
                  ssm_norm, ret_norm, w_branch_ssm, w_branch_ret, w_out, norm_ffn2, ffn2_w1, ffn2_w3, ffn2_w2):
    o_z = 0
    o_xbc = o_z + SSM_D_INNER
    o_dt = o_xbc + SSM_CONV_DIM
    o_q = o_dt + SSM_HEADS
    o_k = o_q + RET_QK
    o_v = o_k + RET_QK
    o_rg = o_v + RET_V
    o_ga = o_rg + RET_V
    o_end = o_ga + 2 * D_MODEL
    assert o_end == w_in.shape[1]

    def cols(a, b):
        return w_in[:, a:b]

    assert (o_z, o_xbc, o_dt) == (SSDP_Z, SSDP_XBC, SSDP_DT)
    w_ssd = jnp.pad(cols(o_z, o_q), ((0, 0), (0, DT_PAD - SSM_HEADS))).astype(BF16)
    half = RET_QK_DIM // 2
    head_of_channel = jnp.arange(SSM_D_INNER, dtype=jnp.int32) // SSM_HEAD_DIM
    return {
        "norm_ffn1": norm_ffn1.reshape(1, -1), "ffn1_w1": ffn1_w1.astype(BF16), "ffn1_w3": ffn1_w3.astype(BF16),
        "ffn1_w2": ffn1_w2.astype(BF16),
        "norm_mix": norm_mix.reshape(1, -1), "w_ssd": w_ssd,
        "w_q": cols(o_q, o_k).astype(BF16), "w_k": cols(o_k, o_v).astype(BF16),
        "w_v": cols(o_v, o_rg).astype(BF16), "w_rg": cols(o_rg, o_ga).astype(BF16),
        "w_gates": cols(o_ga, o_end).astype(BF16),
        "conv_w": conv_w, "conv_b": conv_b.reshape(1, -1),
        "dt_bias": jnp.pad(dt_bias, (0, DT_PAD - SSM_HEADS)).reshape(1, -1),
        "a_log": jnp.pad(a_log, (0, DT_PAD - SSM_HEADS)).reshape(1, -1),
        "d_exp": jnp.repeat(ssm_d, SSM_HEAD_DIM).reshape(1, -1),
        "ssm_norm": ssm_norm.reshape(1, -1), "ret_norm": ret_norm.reshape(1, -1),
        "head_expand": (head_of_channel[None, :] == jnp.arange(DT_PAD, dtype=jnp.int32)[:, None]).astype(BF16),
        "rope_inv": (ROPE_BASE ** (-jnp.arange(half, dtype=F32) / half)).reshape(1, half),
        "w_branch_ssm": w_branch_ssm.astype(BF16), "w_branch_ret": w_branch_ret.astype(BF16),
        "w_out": w_out.astype(BF16),
        "norm_ffn2": norm_ffn2.reshape(1, -1), "ffn2_w1": ffn2_w1.astype(BF16), "ffn2_w3": ffn2_w3.astype(BF16),
        "ffn2_w2": ffn2_w2.astype(BF16),
    }


def kernel(x_prompt, x_sample, state_ssm, state_conv, state_ret, norm_ffn1, ffn1_w1, ffn1_w3, ffn1_w2, norm_mix, w_in,
           conv_w, conv_b, dt_bias, a_log, ssm_d, ssm_norm, ret_norm, w_branch_ssm, w_branch_ret, w_out, norm_ffn2,
           ffn2_w1, ffn2_w3, ffn2_w2, norm_final):
    depth = norm_ffn1.shape[0]
    assert depth == 1, "the final RMSNorm is fused into the last layer's second FFN"
    w = _prep_weights(norm_ffn1[0], ffn1_w1[0], ffn1_w3[0], ffn1_w2[0], norm_mix[0], w_in[0], conv_w[0], conv_b[0],
                      dt_bias[0], a_log[0], ssm_d[0], ssm_norm[0], ret_norm[0], w_branch_ssm[0], w_branch_ret[0],
                      w_out[0], norm_ffn2[0], ffn2_w1[0], ffn2_w3[0], ffn2_w2[0])
    final_gain = norm_final.reshape(1, -1)
    yp, ssm_p, conv_p, ret_p = _prompt_layer(x_prompt, w, final_gain)
    ys, ssm_s, conv_s, ret_s = _sample_layer(x_sample, state_ssm[0], state_conv[0], state_ret[0], w, final_gain)
    return (yp, ys, ssm_p[None], conv_p[None], ret_p[None], ssm_s[None], conv_s[None], ret_s[None])
```

```python
import functools
import math

import jax
import jax.numpy as jnp
import numpy as np
from jax import lax
from jax.experimental import pallas as pl
from jax.experimental.pallas import tpu as pltpu

F32 = jnp.float32
BF16 = jnp.bfloat16

D_MODEL = 1024
D_FF = 2816
SSM_D_INNER = 2048
SSM_HEAD_DIM = 64
SSM_HEADS = 32
SSM_GROUPS = 4
SSM_STATE = 128
SSM_CONV = 4
SSM_BC = 2 * SSM_GROUPS * SSM_STATE
SSM_CONV_DIM = SSM_D_INNER + SSM_BC
RET_HEADS = 4
RET_QK_DIM = 256
RET_V_DIM = 512
RET_QK = RET_HEADS * RET_QK_DIM
RET_V = RET_HEADS * RET_V_DIM
ROPE_BASE = 10000.0
PAST_LEN = 16384
CHUNK = 128
NORM_EPS = 1e-6
GATED_NORM_EPS = 1e-5

V7X_SUBLANES = 8
V7X_LANES = 128
V7X_VMEM_BYTES = 64 * 1024 * 1024
VMEM_LIMIT = V7X_VMEM_BYTES - 8 * 1024 * 1024

DT_PAD = V7X_LANES
LOG_DT_FLOOR = -1e30
LOG2_E = math.log2(math.e)

SSDP_Z = 0
SSDP_XBC = SSM_D_INNER
SSDP_DT = SSM_D_INNER + SSM_CONV_DIM
SSDP_DIM = SSDP_DT + DT_PAD


def _rms(x, g, eps):
    return x * lax.rsqrt(jnp.mean(x * x, axis=-1, keepdims=True) + eps) * g


def _sigmoid(x):
    return 0.5 + 0.5 * jnp.tanh(0.5 * x)


def _silu(x):
    half = 0.5 * x
    return half + half * jnp.tanh(half)


def _softplus(x):
    return jnp.maximum(x, 0.0) + jnp.log1p(jnp.exp(-jnp.abs(x)))


def _dot(a, b):
    return jnp.dot(a.astype(BF16), b.astype(BF16), preferred_element_type=F32)


def _dot_nt(a, b):
    return lax.dot_general(a.astype(BF16), b.astype(BF16), (((1,), (1,)), ((), ())), preferred_element_type=F32)


def _dot_tn(a, b):
    return lax.dot_general(a.astype(BF16), b.astype(BF16), (((0,), (0,)), ((), ())), preferred_element_type=F32)


def _dot_split(a, b_f32, passes):
    acc = None
    rem = b_f32
    for _ in range(passes):
        piece = rem.astype(BF16)
        term = jnp.dot(a, piece, preferred_element_type=F32)
        acc = term if acc is None else acc + term
        rem = rem - piece.astype(F32)
    return acc


def _dot_split_lhs(a_f32, b, passes):
    acc = None
    rem = a_f32
    for _ in range(passes):
        piece = rem.astype(BF16)
        term = jnp.dot(piece, b, preferred_element_type=F32)
        acc = term if acc is None else acc + term
        rem = rem - piece.astype(F32)
    return acc


def _transpose_split(eye, x_f32, passes):
    acc = None
    rem = x_f32
    for _ in range(passes):
        piece = rem.astype(BF16)
        term = lax.dot_general(eye, piece, (((1,), (1,)), ((), ())), preferred_element_type=F32)
        acc = term if acc is None else acc + term
        rem = rem - piece.astype(F32)
    return acc


def _const_spec(shape):
    nd = len(shape)
    return pl.BlockSpec(shape, lambda *_: (0,) * nd, pipeline_mode=pl.Buffered(1))


def _params(n_grid_dims, flags=None):
    return pltpu.CompilerParams(dimension_semantics=("arbitrary",) * n_grid_dims, vmem_limit_bytes=VMEM_LIMIT,
                                flags=flags)


def _ffn_kernel(x_ref, g_ref, w1_ref, w3_ref, w2_ref, gf_ref, o_ref, *, final_norm):
    x = x_ref[...]
    xn = _rms(x, g_ref[...], NORM_EPS).astype(BF16)
    a = jnp.dot(xn, w1_ref[...], preferred_element_type=F32)
    b = jnp.dot(xn, w3_ref[...], preferred_element_type=F32)
    gated = (_silu(a) * b).astype(BF16)
    h = x + 0.5 * jnp.dot(gated, w2_ref[...], preferred_element_type=F32)
    if final_norm:
        h = _rms(h, gf_ref[...], NORM_EPS)
    o_ref[...] = h


def _ffn(x, g, w1, w3, w2, gf, *, final_norm, tm):
    t = x.shape[0]
    assert t % tm == 0
    return pl.pallas_call(
        functools.partial(_ffn_kernel, final_norm=final_norm),
        grid=(t // tm,),
        in_specs=[
            pl.BlockSpec((tm, D_MODEL), lambda i: (i, 0)),
            _const_spec((1, D_MODEL)),
            _const_spec((D_MODEL, D_FF)),
            _const_spec((D_MODEL, D_FF)),
            _const_spec((D_FF, D_MODEL)),
            _const_spec((1, D_MODEL)),
        ],
        out_specs=pl.BlockSpec((tm, D_MODEL), lambda i: (i, 0)),
        out_shape=jax.ShapeDtypeStruct((t, D_MODEL), F32),
        compiler_params=_params(1),
        name="ffn_final" if final_norm else "ffn",
    )(x, g, w1, w3, w2, gf)


def _rope_kernel(inv_ref, cos_ref, sin_ref, *, pos0, rows_per_pos):
    rows = cos_ref.shape[0]
    step = lax.broadcasted_iota(jnp.int32, (rows, V7X_LANES), 0) // rows_per_pos
    ang = (pos0 + step.astype(F32)) * inv_ref[...]
    cos_ref[...] = jnp.cos(ang)
    sin_ref[...] = jnp.sin(ang)


def _rope_tables(inv, rows, pos0, rows_per_pos):
    return pl.pallas_call(
        functools.partial(_rope_kernel, pos0=pos0, rows_per_pos=rows_per_pos),
        out_shape=[jax.ShapeDtypeStruct((rows, V7X_LANES), F32)] * 2,
        name="rope",
    )(inv)


def _chunk_rows(ref):
    return ref[...].reshape(-1, ref.shape[-1])


def _row_time_seq(shape, s_blk):
    r = lax.broadcasted_iota(jnp.int32, shape, 0)
    return r // s_blk, r % s_blk


def _pair_mask(q, s_blk):
    r = lax.broadcasted_iota(jnp.int32, (q, q), 0)
    c = lax.broadcasted_iota(jnp.int32, (q, q), 1)
    if s_blk == 1:
        return r >= c
    return jnp.logical_and(r // s_blk >= c // s_blk, r % s_blk == c % s_blk)


def _rows_spec(q, s_blk, nblk, nc, width, time_major, *, ahead=0, lead_axes=0):
    last = nblk * nc - 1

    def chunk_index(args):
        s, c = args[lead_axes], args[lead_axes + 1]
        return jnp.minimum(s * nc + c + ahead, last)

    if time_major:
        assert nc == 1
        return pl.BlockSpec((q // s_blk, s_blk, width), lambda *a: (0, chunk_index(a), 0))
    return pl.BlockSpec((q, width), lambda *a: (chunk_index(a), 0))


def _project(h_ref, g_ref, w_refs):
    u = _rms(_chunk_rows(h_ref), g_ref[...], NORM_EPS).astype(BF16)
    return [jnp.dot(u, w_ref[...], preferred_element_type=F32) for w_ref in w_refs]


def _conv_pad_rows(s_blk):
    need = (SSM_CONV - 1) * s_blk
    return -(-need // V7X_SUBLANES) * V7X_SUBLANES


def _ssd_kernel(hc_ref, hn_ref, gmix_ref, w_ref, h0_ref, conv0_ref, cw_ref, cb_ref, dtb_ref, alog_ref, dexp_ref,
                gn_ref, e_ref, y_ref, hout_ref, convout_ref, xp_scr, zdt_scr, next_scr, *, s_blk, nc):
    s = pl.program_id(0)
    c = pl.program_id(1)
    pad = _conv_pad_rows(s_blk)
    q = xp_scr.shape[0] - pad
    carry = (SSM_CONV - 1) * s_blk
    steps = q // s_blk
    hg = SSM_HEADS // SSM_GROUPS
    gw = hg * SSM_HEAD_DIM
    h_src = h0_ref if nc == 1 else hout_ref

    @pl.when(jnp.logical_and(s == 0, c == 0))
    def _first():
        next_scr[...] = _project(hc_ref, gmix_ref, [w_ref])[0]

    @pl.when(c == 0)
    def _init():
        xp_scr[0:pad, :] = jnp.zeros((pad, SSM_CONV_DIM), F32)
        xp_scr[pad - carry:pad, :] = conv0_ref[0]
        if nc > 1:
            hout_ref[...] = h0_ref[...]

    xp_scr[pad:pad + q, :] = next_scr[:, SSDP_XBC:SSDP_DT]
    zdt_scr[:, 0:SSM_D_INNER] = next_scr[:, SSDP_Z:SSDP_XBC]
    zdt_scr[:, SSM_D_INNER:SSM_D_INNER + DT_PAD] = next_scr[:, SSDP_DT:SSDP_DIM]
    next_scr[...] = _project(hn_ref, gmix_ref, [w_ref])[0]

    xall = xp_scr[...]
    conv = cb_ref[...] + xall[pad:pad + q, :] * cw_ref[SSM_CONV - 1:SSM_CONV, :]
    for j in range(1, SSM_CONV):
        d = j * s_blk
        if d % V7X_SUBLANES == 0:
            back = xall[pad - d:pad - d + q, :]
        else:
            back = pltpu.roll(xall, d, 0)[pad:pad + q, :]
        conv = conv + back * cw_ref[SSM_CONV - 1 - j:SSM_CONV - j, :]
    @pl.when(c == nc - 1)
    def _():
        convout_ref[0] = xp_scr[pad + q - carry:pad + q, :]

    if nc > 1:
        xp_scr[0:pad, :] = xp_scr[q:q + pad, :]
    z_rows = zdt_scr[:, 0:SSM_D_INNER]
    dt_rows = zdt_scr[:, SSM_D_INNER:SSM_D_INNER + DT_PAD]
    xc = _silu(conv)
    xs = xc[:, 0:SSM_D_INNER]
    bm = xc[:, SSM_D_INNER:SSM_D_INNER + SSM_GROUPS * SSM_STATE]
    cm = xc[:, SSM_D_INNER + SSM_GROUPS * SSM_STATE:SSM_CONV_DIM]

    dt = _softplus(dt_rows + dtb_ref[...])
    a = -jnp.exp(alog_ref[...])
    visible = _pair_mask(q, s_blk)
    la = _dot_split(jnp.where(visible, 1.0, 0.0).astype(BF16), dt * (a * LOG2_E), 3)
    lsd = la - jnp.maximum(jnp.log2(dt), LOG_DT_FLOOR)
    eye_r = lax.broadcasted_iota(jnp.int32, (DT_PAD, DT_PAD), 0)
    eye_c = lax.broadcasted_iota(jnp.int32, (DT_PAD, DT_PAD), 1)
    eye = jnp.where(eye_r == eye_c, 1.0, 0.0).astype(BF16)
    la_t = _transpose_split(eye, la, 3)
    lsd_t = _transpose_split(eye, lsd, 3)
    la_last = la[q - s_blk:q, :]
    la_last_rows = la_last if s_blk == 1 else jnp.concatenate([la_last] * steps, axis=0)
    dec_t = jnp.exp2(la_t[:, q - s_blk:q])

    lane = lax.broadcasted_iota(jnp.int32, (q, 2 * SSM_HEAD_DIM), 1)
    first_head = lane < SSM_HEAD_DIM
    y_parts = []
    for g in range(SSM_GROUPS):
        cg = cm[:, g * SSM_STATE:(g + 1) * SSM_STATE]
        bg = bm[:, g * SSM_STATE:(g + 1) * SSM_STATE]
        cbg = _dot_nt(cg, bg)
        for pair in range(hg // 2):
            h0 = g * hg + 2 * pair
            ws = []
            for h in (h0, h0 + 1):
                seg = la[:, h:h + 1] - lsd_t[h:h + 1, :]
                ws.append(cbg * jnp.exp2(jnp.where(visible, seg, -jnp.inf)))
            xpair = xs[:, h0 * SSM_HEAD_DIM:(h0 + 2) * SSM_HEAD_DIM]
            rhs = jnp.concatenate([jnp.where(first_head, xpair, 0.0), jnp.where(first_head, 0.0, xpair)], axis=0)
            y_parts.append(_dot(jnp.concatenate(ws, axis=1), rhs))
    y = jnp.concatenate(y_parts, axis=1)

    _, seq_bc = _row_time_seq((q, SSM_BC // 2), s_blk)
    _, seq_x = _row_time_seq((q, SSM_D_INNER), s_blk)
    y_state = None
    for b in range(s_blk):
        cm_b = (cm if s_blk == 1 else jnp.where(seq_bc == b, cm, 0.0)).astype(BF16)
        part = jnp.concatenate(
            [_dot_nt(cm_b[:, g * SSM_STATE:(g + 1) * SSM_STATE], h_src[b, g * gw:(g + 1) * gw, :])
             for g in range(SSM_GROUPS)], axis=1)
        y_state = part if y_state is None else y_state + part

    e = e_ref[...]
    y = y + y_state * _dot_split_lhs(jnp.exp2(la), e, 2) + xs * dexp_ref[...]
    y = y * _silu(z_rows)
    y = _rms(y, gn_ref[...], GATED_NORM_EPS)
    y_ref[...] = y.reshape(y_ref.shape).astype(y_ref.dtype)

    tail = jnp.exp2(la_last_rows - lsd)
    xt = xs * _dot_split_lhs(tail, e, 2)
    bm_bf = bm.astype(BF16)
    for b in range(s_blk):
        xt_b = (xt if s_blk == 1 else jnp.where(seq_x == b, xt, 0.0)).astype(BF16)
        dec_b = jnp.broadcast_to(dec_t[:, b:b + 1], (DT_PAD, SSM_STATE))
        for g in range(SSM_GROUPS):
            upd = _dot_tn(xt_b[:, g * gw:(g + 1) * gw], bm_bf[:, g * SSM_STATE:(g + 1) * SSM_STATE])
            for hh in range(hg):
                h = g * hg + hh
                r0 = h * SSM_HEAD_DIM
                hout_ref[b, r0:r0 + SSM_HEAD_DIM, :] = (
                    h_src[b, r0:r0 + SSM_HEAD_DIM, :] * dec_b[h:h + 1, :]
                    + upd[hh * SSM_HEAD_DIM:(hh + 1) * SSM_HEAD_DIM, :])


def _ssd(h, gmix, w, h0, conv0, cw, cb, dtb, alog, dexp, gn, e, *, nblk, nc, q, s_blk, time_major):
    rows = SSM_HEADS * SSM_HEAD_DIM
    carry = (SSM_CONV - 1) * s_blk
    per_block0 = h0.shape[0] == nblk * s_blk
    assert per_block0 or (h0.shape[0] == s_blk and conv0.shape[0] == 1)
    spec = functools.partial(_rows_spec, q, s_blk, nblk, nc, time_major=time_major)
    y_shape = (q // s_blk, nblk * s_blk, SSM_D_INNER) if time_major else (nblk * nc * q, SSM_D_INNER)
    return pl.pallas_call(
        functools.partial(_ssd_kernel, s_blk=s_blk, nc=nc),
        grid=(nblk, nc),
        in_specs=[
            spec(D_MODEL),
            spec(D_MODEL, ahead=1),
            _const_spec((1, D_MODEL)),
            _const_spec((D_MODEL, SSDP_DIM)),
            pl.BlockSpec((s_blk, rows, SSM_STATE), lambda s, c: (s if per_block0 else 0, 0, 0)),
            pl.BlockSpec((1, carry, SSM_CONV_DIM), lambda s, c: (s if per_block0 else 0, 0, 0)),
            _const_spec((SSM_CONV, SSM_CONV_DIM)),
            _const_spec((1, SSM_CONV_DIM)),
            _const_spec((1, DT_PAD)),
            _const_spec((1, DT_PAD)),
            _const_spec((1, SSM_D_INNER)),
            _const_spec((1, SSM_D_INNER)),
            _const_spec((DT_PAD, SSM_D_INNER)),
        ],
        out_specs=[
            spec(SSM_D_INNER),
            pl.BlockSpec((s_blk, rows, SSM_STATE), lambda s, c: (s, 0, 0)),
            pl.BlockSpec((1, carry, SSM_CONV_DIM), lambda s, c: (s, 0, 0)),
        ],
        out_shape=[
            jax.ShapeDtypeStruct(y_shape, BF16),
            jax.ShapeDtypeStruct((nblk * s_blk, rows, SSM_STATE), F32),
            jax.ShapeDtypeStruct((nblk, carry, SSM_CONV_DIM), F32),
        ],
        scratch_shapes=[
            pltpu.VMEM((_conv_pad_rows(s_blk) + q, SSM_CONV_DIM), F32),
            pltpu.VMEM((q, SSM_D_INNER + DT_PAD), F32),
            pltpu.VMEM((q, SSDP_DIM), F32),
        ],
        compiler_params=_params(2),
        name="ssd",
    )(h, h, gmix, w, h0, conv0, cw, cb, dtb, alog, dexp, gn, e)


def _ret_log_decay(h):
    return float(np.log1p(-np.exp2(-5.0 - h)))


def _ret_project(h_ref, gmix_ref, w_refs, dst):
    pq, pk, pv, pg = _project(h_ref, gmix_ref, w_refs)
    dst[0][...] = pq
    dst[1][...] = pk
    dst[2][...] = pv.astype(BF16)
    dst[3][...] = pg


def _ret_kernel(*refs, s_blk, nc, n_heads):
    io_refs, set0, set1 = refs[:-8], refs[-8:-4], refs[-4:]
    hc_ref, _, gmix_ref = io_refs[0:3]
    w_refs = io_refs[3:7]
    s = pl.program_id(1)
    c = pl.program_id(2)

    @pl.when(jnp.logical_and(s == 0, c == 0))
    def _first():
        _ret_project(hc_ref, gmix_ref, w_refs, set0)

    parity = (s * nc + c) % 2

    @pl.when(parity == 0)
    def _even():
        _ret_step(*io_refs, set0, set1, s_blk=s_blk, nc=nc, n_heads=n_heads)

    @pl.when(parity == 1)
    def _odd():
        _ret_step(*io_refs, set1, set0, s_blk=s_blk, nc=nc, n_heads=n_heads)


def _ret_step(hc_ref, hn_ref, gmix_ref, wq_ref, wk_ref, wv_ref, wg_ref, cos_ref, sin_ref, s0_ref, gn_ref,
              y_ref, sout_ref, cur_set, next_set, *, s_blk, nc, n_heads):
    q_scr, k_scr, v_scr, g_scr = cur_set
    hb = pl.program_id(0)
    c = pl.program_id(2)
    q = cos_ref.shape[0]
    steps = q // s_blk
    half = RET_QK_DIM // 2
    s_src = s0_ref if nc == 1 else sout_ref

    if nc > 1:
        @pl.when(c == 0)
        def _init():
            sout_ref[...] = s0_ref[...]

    _ret_project(hn_ref, gmix_ref, [wq_ref, wk_ref, wv_ref, wg_ref], next_set)

    qq = q_scr[...]
    kk = k_scr[...]
    vv = v_scr[...]
    rg = g_scr[...]
    cos = cos_ref[...]
    sin = sin_ref[...]
    visible = _pair_mask(q, s_blk)
    r = lax.broadcasted_iota(jnp.int32, (q, q), 0)
    cc = lax.broadcasted_iota(jnp.int32, (q, q), 1)
    rel = (r // s_blk - cc // s_blk).astype(F32)
    t_qk, seq_qk = _row_time_seq((q, RET_QK_DIM), s_blk)
    t_qk = t_qk.astype(F32)

    def rot(t, h):
        t1 = t[:, h * RET_QK_DIM:h * RET_QK_DIM + half]
        t2 = t[:, h * RET_QK_DIM + half:(h + 1) * RET_QK_DIM]
        return jnp.concatenate([t1 * cos - t2 * sin, t1 * sin + t2 * cos], axis=1)

    y_parts = []
    for h in range(n_heads):
        if n_heads == RET_HEADS:
            lg = _ret_log_decay(h)
            chunk_dec = math.exp(steps * lg)
        else:
            assert n_heads == 1
            lg = jnp.float32(_ret_log_decay(0))
            chunk_dec = jnp.float32(math.exp(steps * _ret_log_decay(0)))
            for hh in range(1, RET_HEADS):
                lg = jnp.where(hb == hh, jnp.float32(_ret_log_decay(hh)), lg)
                chunk_dec = jnp.where(hb == hh, jnp.float32(math.exp(steps * _ret_log_decay(hh))), chunk_dec)
        dmat = jnp.where(visible, jnp.exp(jnp.maximum(rel, 0.0) * lg), 0.0)
        q_dec = jnp.exp((t_qk + 1.0) * lg)
        k_dec = jnp.exp((steps - 1.0 - t_qk) * lg)
        qr = rot(qq, h)
        kr = rot(kk, h) * (RET_QK_DIM ** -0.5)
        vh = vv[:, h * RET_V_DIM:(h + 1) * RET_V_DIM]
        scores = _dot_nt(qr, kr) * dmat
        y = _dot(scores, vh)
        qd = qr * q_dec
        kd = kr * k_dec
        r0 = h * RET_QK_DIM
        for b in range(s_blk):
            qd_b = (qd if s_blk == 1 else jnp.where(seq_qk == b, qd, 0.0)).astype(BF16)
            kd_b = (kd if s_blk == 1 else jnp.where(seq_qk == b, kd, 0.0)).astype(BF16)
            sh = s_src[b, r0:r0 + RET_QK_DIM, :]
            y = y + _dot(qd_b, sh)
            sout_ref[b, r0:r0 + RET_QK_DIM, :] = sh * chunk_dec + _dot_tn(kd_b, vh)
        g = gn_ref[:, h * RET_V_DIM:(h + 1) * RET_V_DIM]
        y = _rms(y, g, NORM_EPS) * _silu(rg[:, h * RET_V_DIM:(h + 1) * RET_V_DIM])
        y_parts.append(y)
    y = y_parts[0] if n_heads == 1 else jnp.concatenate(y_parts, axis=1)
    y_ref[...] = y.reshape(y_ref.shape).astype(y_ref.dtype)


def _ret(h, gmix, wq, wk, wv, wg, cos, sin, s0, gn, *, nblk, nc, q, s_blk, n_heads, time_major):
    nhb = RET_HEADS // n_heads
    qk_w = n_heads * RET_QK_DIM
    v_w = n_heads * RET_V_DIM
    per_block0 = s0.shape[0] == nblk * s_blk
    assert per_block0 or s0.shape[0] == s_blk
    rows = functools.partial(_rows_spec, q, s_blk, nblk, nc, D_MODEL, time_major, lead_axes=1)

    def head_cols(nrows, width):
        if nhb == 1:
            return _const_spec((nrows, width))
        return pl.BlockSpec((nrows, width), lambda hb, s, c: (0, hb))

    if time_major:
        y_spec = pl.BlockSpec((q // s_blk, s_blk, v_w), lambda hb, s, c: (0, s, hb))
        y_shape = (q // s_blk, nblk * s_blk, RET_V)
    else:
        y_spec = pl.BlockSpec((q, v_w), lambda hb, s, c: (s * nc + c, hb))
        y_shape = (nblk * nc * q, RET_V)
    return pl.pallas_call(
        functools.partial(_ret_kernel, s_blk=s_blk, nc=nc, n_heads=n_heads),
        grid=(nhb, nblk, nc),
        in_specs=[
            rows(),
            rows(ahead=1),
            _const_spec((1, D_MODEL)),
            head_cols(D_MODEL, qk_w),
            head_cols(D_MODEL, qk_w),
            head_cols(D_MODEL, v_w),
            head_cols(D_MODEL, v_w),
            pl.BlockSpec((q, V7X_LANES), lambda hb, s, c: (c, 0)),
            pl.BlockSpec((q, V7X_LANES), lambda hb, s, c: (c, 0)),
            pl.BlockSpec((s_blk, qk_w, RET_V_DIM), lambda hb, s, c: (s if per_block0 else 0, hb, 0)),
            head_cols(1, v_w),
        ],
        out_specs=[
            y_spec,
            pl.BlockSpec((s_blk, qk_w, RET_V_DIM), lambda hb, s, c: (s, hb, 0)),
        ],
        out_shape=[
            jax.ShapeDtypeStruct(y_shape, BF16),
            jax.ShapeDtypeStruct((nblk * s_blk, RET_HEADS * RET_QK_DIM, RET_V_DIM), F32),
        ],
        scratch_shapes=[
            pltpu.VMEM((q, qk_w), F32),
            pltpu.VMEM((q, qk_w), F32),
            pltpu.VMEM((q, v_w), BF16),
            pltpu.VMEM((q, v_w), F32),
        ] * 2,
        compiler_params=_params(3),
        name="ret",
    )(h, h, gmix, wq, wk, wv, wg, cos, sin, s0, gn)


def _merge_kernel(h_ref, ys_ref, yr_ref, g_ref, wg_ref, wbs_ref, wbr_ref, wo_ref, o_ref):
    h = h_ref[...]
    u = _rms(h, g_ref[...], NORM_EPS).astype(BF16)
    gates = jnp.dot(u, wg_ref[...], preferred_element_type=F32)
    branch_ssm = jnp.dot(ys_ref[...], wbs_ref[...], preferred_element_type=F32)
    branch_ret = jnp.dot(yr_ref[...], wbr_ref[...], preferred_element_type=F32)
    merged = _sigmoid(gates[:, 0:D_MODEL]) * branch_ssm + _sigmoid(gates[:, D_MODEL:2 * D_MODEL]) * branch_ret
    o_ref[...] = h + jnp.dot(merged.astype(BF16), wo_ref[...], preferred_element_type=F32)


def _merge(h, ys, yr, g, wg, wbs, wbr, wo, *, tm):
    t = h.shape[0]
    assert t % tm == 0
    return pl.pallas_call(
        _merge_kernel,
        grid=(t // tm,),
        in_specs=[
            pl.BlockSpec((tm, D_MODEL), lambda i: (i, 0)),
            pl.BlockSpec((tm, SSM_D_INNER), lambda i: (i, 0)),
            pl.BlockSpec((tm, RET_V), lambda i: (i, 0)),
            _const_spec((1, D_MODEL)),
            _const_spec((D_MODEL, 2 * D_MODEL)),
            _const_spec((SSM_D_INNER, D_MODEL)),
            _const_spec((RET_V, D_MODEL)),
            _const_spec((D_MODEL, D_MODEL)),
        ],
        out_specs=pl.BlockSpec((tm, D_MODEL), lambda i: (i, 0)),
        out_shape=jax.ShapeDtypeStruct((t, D_MODEL), F32),
        compiler_params=_params(1),
        name="merge",
    )(h, ys, yr, g, wg, wbs, wbr, wo)


FFN_TILE = 512
SAMPLE_SEQS_PER_STEP = V7X_SUBLANES
SAMPLE_RET_HEADS_PER_STEP = 1


def _layer(x2, pos0, ssm0, conv0, ret0, w, final_gain, *, nblk, nc, q, s_blk, ret_heads, time_major):
    t = x2.shape[0]
    tm_ffn = min(FFN_TILE, t)
    steps = q // s_blk
    nseq = nblk * s_blk

    h = _ffn(x2, w["norm_ffn1"], w["ffn1_w1"], w["ffn1_w3"], w["ffn1_w2"], final_gain, final_norm=False, tm=tm_ffn)
    h_rows = h.reshape(steps, nseq, D_MODEL) if time_major else h

    y_ssm, ssm_new, conv_new = _ssd(
        h_rows, w["norm_mix"], w["w_ssd"], ssm0, conv0, w["conv_w"], w["conv_b"], w["dt_bias"], w["a_log"],
        w["d_exp"], w["ssm_norm"], w["head_expand"], nblk=nblk, nc=nc, q=q, s_blk=s_blk, time_major=time_major)
    cos, sin = _rope_tables(w["rope_inv"], nc * q, pos0, s_blk)
    y_ret, ret_new = _ret(h_rows, w["norm_mix"], w["w_q"], w["w_k"], w["w_v"], w["w_rg"], cos, sin, ret0,
                          w["ret_norm"], nblk=nblk, nc=nc, q=q, s_blk=s_blk, n_heads=ret_heads,
                          time_major=time_major)

    h = _merge(h, y_ssm.reshape(t, SSM_D_INNER), y_ret.reshape(t, RET_V), w["norm_mix"], w["w_gates"],
               w["w_branch_ssm"], w["w_branch_ret"], w["w_out"], tm=tm_ffn)
    y = _ffn(h, w["norm_ffn2"], w["ffn2_w1"], w["ffn2_w3"], w["ffn2_w2"], final_gain, final_norm=True, tm=tm_ffn)
    return y, ssm_new, conv_new, ret_new


def _prompt_layer(x, w, final_gain):
    nseq, seq_len, _ = x.shape
    q = math.gcd(seq_len, CHUNK)
    y, ssm_new, conv_new, ret_new = _layer(
        x.reshape(nseq * seq_len, D_MODEL), 0.0,
        jnp.zeros((1, SSM_HEADS * SSM_HEAD_DIM, SSM_STATE), F32),
        jnp.zeros((1, SSM_CONV - 1, SSM_CONV_DIM), F32),
        jnp.zeros((1, RET_HEADS * RET_QK_DIM, RET_V_DIM), F32),
        w, final_gain, nblk=nseq, nc=seq_len // q, q=q, s_blk=1, ret_heads=RET_HEADS, time_major=False)
    return (y.reshape(nseq, seq_len, D_MODEL), ssm_new.reshape(nseq, SSM_HEADS, SSM_HEAD_DIM, SSM_STATE), conv_new,
            ret_new.reshape(nseq, RET_HEADS, RET_QK_DIM, RET_V_DIM))


def _sample_layer(x, ssm0, conv0, ret0, w, final_gain):
    nseq, seq_len, _ = x.shape
    s_blk = SAMPLE_SEQS_PER_STEP
    assert nseq % s_blk == 0 and seq_len <= CHUNK
    nblk = nseq // s_blk
    kc = SSM_CONV - 1
    x_tm = jnp.transpose(x, (1, 0, 2)).reshape(seq_len * nseq, D_MODEL)
    conv0_tm = jnp.transpose(conv0.reshape(nblk, s_blk, kc, SSM_CONV_DIM), (0, 2, 1, 3)).reshape(
        nblk, kc * s_blk, SSM_CONV_DIM)
    y, ssm_new, conv_new, ret_new = _layer(
        x_tm, float(PAST_LEN), ssm0.reshape(nseq, SSM_HEADS * SSM_HEAD_DIM, SSM_STATE), conv0_tm,
        ret0.reshape(nseq, RET_HEADS * RET_QK_DIM, RET_V_DIM), w, final_gain,
        nblk=nblk, nc=1, q=s_blk * seq_len, s_blk=s_blk, ret_heads=SAMPLE_RET_HEADS_PER_STEP, time_major=True)
    conv_new = jnp.transpose(conv_new.reshape(nblk, kc, s_blk, SSM_CONV_DIM), (0, 2, 1, 3)).reshape(
        nseq, kc, SSM_CONV_DIM)
    return (jnp.transpose(y.reshape(seq_len, nseq, D_MODEL), (1, 0, 2)),
            ssm_new.reshape(nseq, SSM_HEADS, SSM_HEAD_DIM, SSM_STATE), conv_new,
            ret_new.reshape(nseq, RET_HEADS, RET_QK_DIM, RET_V_DIM))


def _prep_weights(norm_ffn1, ffn1_w1, ffn1_w3, ffn1_w2, norm_mix, w_in, conv_w, conv_b, dt_bias, a_log, ssm_d,
                  ssm_norm, ret_norm, w_branch_ssm, w_branch_ret, w_out, norm_ffn2, ffn2_w1, ffn2_w3, ffn2_w2):
    o_z = 0
    o_xbc = o_z + SSM_D_INNER
    o_dt = o_xbc + SSM_CONV_DIM
    o_q = o_dt + SSM_HEADS
    o_k = o_q + RET_QK
    o_v = o_k + RET_QK
    o_rg = o_v + RET_V
    o_ga = o_rg + RET_V
    o_end = o_ga + 2 * D_MODEL
    assert o_end == w_in.shape[1]

    w_in_bf = w_in.astype(BF16)

    def cols(a, b):
        return w_in_bf[:, a:b]

    assert (o_z, o_xbc, o_dt) == (SSDP_Z, SSDP_XBC, SSDP_DT) and SSDP_DIM <= o_end
    half = RET_QK_DIM // 2
    head_of_channel = jnp.arange(SSM_D_INNER, dtype=jnp.int32) // SSM_HEAD_DIM
    return {
        "norm_ffn1": norm_ffn1.reshape(1, -1), "ffn1_w1": ffn1_w1.astype(BF16), "ffn1_w3": ffn1_w3.astype(BF16),
        "ffn1_w2": ffn1_w2.astype(BF16),
        "norm_mix": norm_mix.reshape(1, -1), "w_ssd": w_in_bf,
        "w_q": cols(o_q, o_k), "w_k": cols(o_k, o_v), "w_v": cols(o_v, o_rg), "w_rg": cols(o_rg, o_ga),
        "w_gates": cols(o_ga, o_end),
        "conv_w": conv_w, "conv_b": conv_b.reshape(1, -1),
        "dt_bias": jnp.pad(dt_bias, (0, DT_PAD - SSM_HEADS)).reshape(1, -1),
        "a_log": jnp.pad(a_log, (0, DT_PAD - SSM_HEADS)).reshape(1, -1),
        "d_exp": jnp.repeat(ssm_d, SSM_HEAD_DIM).reshape(1, -1),
        "ssm_norm": ssm_norm.reshape(1, -1), "ret_norm": ret_norm.reshape(1, -1),
        "head_expand": (head_of_channel[None, :] == jnp.arange(DT_PAD, dtype=jnp.int32)[:, None]).astype(BF16),
        "rope_inv": (ROPE_BASE ** (-jnp.arange(half, dtype=F32) / half)).reshape(1, half),
        "w_branch_ssm": w_branch_ssm.astype(BF16), "w_branch_ret": w_branch_ret.astype(BF16),
        "w_out": w_out.astype(BF16),
        "norm_ffn2": norm_ffn2.reshape(1, -1), "ffn2_w1": ffn2_w1.astype(BF16), "ffn2_w3": ffn2_w3.astype(BF16),
        "ffn2_w2": ffn2_w2.astype(BF16),
    }


def kernel(x_prompt, x_sample, state_ssm, state_conv, state_ret, norm_ffn1, ffn1_w1, ffn1_w3, ffn1_w2, norm_mix, w_in,
           conv_w, conv_b, dt_bias, a_log, ssm_d, ssm_norm, ret_norm, w_branch_ssm, w_branch_ret, w_out, norm_ffn2,
           ffn2_w1, ffn2_w3, ffn2_w2, norm_final):
    depth = norm_ffn1.shape[0]
    assert depth == 1, "the final RMSNorm is fused into the last layer's second FFN"
    w = _prep_weights(norm_ffn1[0], ffn1_w1[0], ffn1_w3[0], ffn1_w2[0], norm_mix[0], w_in[0], conv_w[0], conv_b[0],
                      dt_bias[0], a_log[0], ssm_d[0], ssm_norm[0], ret_norm[0], w_branch_ssm[0], w_branch_ret[0],
                      w_out[0], norm_ffn2[0], ffn2_w1[0], ffn2_w3[0], ffn2_w2[0])
    final_gain = norm_final.reshape(1, -1)
    yp, ssm_p, conv_p, ret_p = _prompt_layer(x_prompt, w, final_gain)
    ys, ssm_s, conv_s, ret_s = _sample_layer(x_sample, state_ssm[0], state_conv[0], state_ret[0], w, final_gain)
    return (yp, ys, ssm_p[None], conv_p[None], ret_p[None], ssm_s[None], conv_s[None], ret_s[None])
```

```python
import functools
import math

import jax
import jax.numpy as jnp
import numpy as np
from jax import lax
from jax.experimental import pallas as pl
from jax.experimental.pallas import tpu as pltpu

F32 = jnp.float32
BF16 = jnp.bfloat16

D_MODEL = 1024
D_FF = 2816
SSM_D_INNER = 2048
SSM_HEAD_DIM = 64
SSM_HEADS = 32
SSM_GROUPS = 4
SSM_STATE = 128
SSM_CONV = 4
SSM_BC = 2 * SSM_GROUPS * SSM_STATE
SSM_CONV_DIM = SSM_D_INNER + SSM_BC
RET_HEADS = 4
RET_QK_DIM = 256
RET_V_DIM = 512
RET_QK = RET_HEADS * RET_QK_DIM
RET_V = RET_HEADS * RET_V_DIM
ROPE_BASE = 10000.0
PAST_LEN = 16384
CHUNK = 128
NORM_EPS = 1e-6
GATED_NORM_EPS = 1e-5

V7X_SUBLANES = 8
V7X_LANES = 128
V7X_VMEM_BYTES = 64 * 1024 * 1024
VMEM_LIMIT = V7X_VMEM_BYTES - 8 * 1024 * 1024

DT_PAD = V7X_LANES
LOG_DT_FLOOR = -1e30
LOG2_E = math.log2(math.e)

SSDP_Z = 0
SSDP_XBC = SSM_D_INNER
SSDP_DT = SSM_D_INNER + SSM_CONV_DIM
SSDP_DIM = SSDP_DT + DT_PAD


def _rms(x, g, eps):
    return x * lax.rsqrt(jnp.mean(x * x, axis=-1, keepdims=True) + eps) * g


def _sigmoid(x):
    return 0.5 + 0.5 * jnp.tanh(0.5 * x)


def _silu(x):
    half = 0.5 * x
    return half + half * jnp.tanh(half)


def _softplus(x):
    return jnp.maximum(x, 0.0) + jnp.log1p(jnp.exp(-jnp.abs(x)))


def _dot(a, b):
    return jnp.dot(a.astype(BF16), b.astype(BF16), preferred_element_type=F32)


def _dot_nt(a, b):
    return lax.dot_general(a.astype(BF16), b.astype(BF16), (((1,), (1,)), ((), ())), preferred_element_type=F32)


def _dot_tn(a, b):
    return lax.dot_general(a.astype(BF16), b.astype(BF16), (((0,), (0,)), ((), ())), preferred_element_type=F32)


def _dot_split(a, b_f32, passes):
    acc = None
    rem = b_f32
    for _ in range(passes):
        piece = rem.astype(BF16)
        term = jnp.dot(a, piece, preferred_element_type=F32)
        acc = term if acc is None else acc + term
        rem = rem - piece.astype(F32)
    return acc


def _dot_split_lhs(a_f32, b, passes):
    acc = None
    rem = a_f32
    for _ in range(passes):
        piece = rem.astype(BF16)
        term = jnp.dot(piece, b, preferred_element_type=F32)
        acc = term if acc is None else acc + term
        rem = rem - piece.astype(F32)
    return acc


def _transpose_split(eye, x_f32, passes):
    acc = None
    rem = x_f32
    for _ in range(passes):
        piece = rem.astype(BF16)
        term = lax.dot_general(eye, piece, (((1,), (1,)), ((), ())), preferred_element_type=F32)
        acc = term if acc is None else acc + term
        rem = rem - piece.astype(F32)
    return acc


def _const_spec(shape):
    nd = len(shape)
    return pl.BlockSpec(shape, lambda *_: (0,) * nd, pipeline_mode=pl.Buffered(1))


def _params(n_grid_dims, flags=None):
    return pltpu.CompilerParams(dimension_semantics=("arbitrary",) * n_grid_dims, vmem_limit_bytes=VMEM_LIMIT,
                                flags=flags)


def _ffn_kernel(x_ref, g_ref, w1_ref, w3_ref, w2_ref, gf_ref, o_ref, *, final_norm):
    x = x_ref[...]
    xn = _rms(x, g_ref[...], NORM_EPS).astype(BF16)
    a = jnp.dot(xn, w1_ref[...], preferred_element_type=F32)
    b = jnp.dot(xn, w3_ref[...], preferred_element_type=F32)
    gated = (_silu(a) * b).astype(BF16)
    h = x + 0.5 * jnp.dot(gated, w2_ref[...], preferred_element_type=F32)
    if final_norm:
        h = _rms(h, gf_ref[...], NORM_EPS)
    o_ref[...] = h


def _ffn(x, g, w1, w3, w2, gf, *, final_norm, tm):
    t = x.shape[0]
    assert t % tm == 0
    return pl.pallas_call(
        functools.partial(_ffn_kernel, final_norm=final_norm),
        grid=(t // tm,),
        in_specs=[
            pl.BlockSpec((tm, D_MODEL), lambda i: (i, 0)),
            _const_spec((1, D_MODEL)),
            _const_spec((D_MODEL, D_FF)),
            _const_spec((D_MODEL, D_FF)),
            _const_spec((D_FF, D_MODEL)),
            _const_spec((1, D_MODEL)),
        ],
        out_specs=pl.BlockSpec((tm, D_MODEL), lambda i: (i, 0)),
        out_shape=jax.ShapeDtypeStruct((t, D_MODEL), F32),
        compiler_params=_params(1),
        name="ffn_final" if final_norm else "ffn",
    )(x, g, w1, w3, w2, gf)


def _rope_kernel(inv_ref, cos_ref, sin_ref, *, pos0, rows_per_pos):
    rows = cos_ref.shape[0]
    step = lax.broadcasted_iota(jnp.int32, (rows, V7X_LANES), 0) // rows_per_pos
    ang = (pos0 + step.astype(F32)) * inv_ref[...]
    cos_ref[...] = jnp.cos(ang)
    sin_ref[...] = jnp.sin(ang)


def _rope_tables(inv, rows, pos0, rows_per_pos):
    return pl.pallas_call(
        functools.partial(_rope_kernel, pos0=pos0, rows_per_pos=rows_per_pos),
        out_shape=[jax.ShapeDtypeStruct((rows, V7X_LANES), F32)] * 2,
        name="rope",
    )(inv)


def _chunk_rows(ref):
    return ref[...].reshape(-1, ref.shape[-1])


def _row_time_seq(shape, s_blk):
    r = lax.broadcasted_iota(jnp.int32, shape, 0)
    return r // s_blk, r % s_blk


def _pair_mask(q, s_blk):
    r = lax.broadcasted_iota(jnp.int32, (q, q), 0)
    c = lax.broadcasted_iota(jnp.int32, (q, q), 1)
    if s_blk == 1:
        return r >= c
    return jnp.logical_and(r // s_blk >= c // s_blk, r % s_blk == c % s_blk)


def _rows_spec(q, s_blk, nblk, nc, width, time_major, *, ahead=0, lead_axes=0):
    last = nblk * nc - 1

    def chunk_index(args):
        s, c = args[lead_axes], args[lead_axes + 1]
        return jnp.minimum(s * nc + c + ahead, last)

    if time_major:
        assert nc == 1
        return pl.BlockSpec((q // s_blk, s_blk, width), lambda *a: (0, chunk_index(a), 0))
    return pl.BlockSpec((q, width), lambda *a: (chunk_index(a), 0))


def _project(h_ref, g_ref, w_refs):
    u = _rms(_chunk_rows(h_ref), g_ref[...], NORM_EPS).astype(BF16)
    return [jnp.dot(u, w_ref[...], preferred_element_type=F32) for w_ref in w_refs]


def _conv_pad_rows(s_blk):
    need = (SSM_CONV - 1) * s_blk
    return -(-need // V7X_SUBLANES) * V7X_SUBLANES


def _ssd_kernel(hc_ref, hn_ref, gmix_ref, w_ref, h0_ref, conv0_ref, cw_ref, cb_ref, dtb_ref, alog_ref, dexp_ref,
                gn_ref, e_ref, y_ref, hout_ref, convout_ref, xp_scr, zdt_scr, next_scr, *, s_blk, nc):
    s = pl.program_id(0)
    c = pl.program_id(1)
    pad = _conv_pad_rows(s_blk)
    q = xp_scr.shape[0] - pad
    carry = (SSM_CONV - 1) * s_blk
    steps = q // s_blk
    hg = SSM_HEADS // SSM_GROUPS
    gw = hg * SSM_HEAD_DIM
    h_src = h0_ref if nc == 1 else hout_ref

    @pl.when(jnp.logical_and(s == 0, c == 0))
    def _first():
        next_scr[...] = _project(hc_ref, gmix_ref, [w_ref])[0]

    @pl.when(c == 0)
    def _init():
        xp_scr[0:pad, :] = jnp.zeros((pad, SSM_CONV_DIM), F32)
        xp_scr[pad - carry:pad, :] = conv0_ref[0]
        if nc > 1:
            hout_ref[...] = h0_ref[...]

    xp_scr[pad:pad + q, :] = next_scr[:, SSDP_XBC:SSDP_DT]
    zdt_scr[:, 0:SSM_D_INNER] = next_scr[:, SSDP_Z:SSDP_XBC]
    zdt_scr[:, SSM_D_INNER:SSM_D_INNER + DT_PAD] = next_scr[:, SSDP_DT:SSDP_DIM]
    next_scr[...] = _project(hn_ref, gmix_ref, [w_ref])[0]

    z_gate = _silu(zdt_scr[:, 0:SSM_D_INNER])

    xall = xp_scr[...]
    conv = cb_ref[...] + xall[pad:pad + q, :] * cw_ref[SSM_CONV - 1:SSM_CONV, :]
    for j in range(1, SSM_CONV):
        d = j * s_blk
        if d % V7X_SUBLANES == 0:
            back = xall[pad - d:pad - d + q, :]
        else:
            back = pltpu.roll(xall, d, 0)[pad:pad + q, :]
        conv = conv + back * cw_ref[SSM_CONV - 1 - j:SSM_CONV - j, :]
    @pl.when(c == nc - 1)
    def _():
        convout_ref[0] = xp_scr[pad + q - carry:pad + q, :]

    if nc > 1:
        xp_scr[0:pad, :] = xp_scr[q:q + pad, :]
    dt_rows = zdt_scr[:, SSM_D_INNER:SSM_D_INNER + DT_PAD]
    xc = _silu(conv)
    xs = xc[:, 0:SSM_D_INNER]
    bm = xc[:, SSM_D_INNER:SSM_D_INNER + SSM_GROUPS * SSM_STATE]
    cm = xc[:, SSM_D_INNER + SSM_GROUPS * SSM_STATE:SSM_CONV_DIM]

    dt = _softplus(dt_rows + dtb_ref[...])
    a = -jnp.exp(alog_ref[...])
    visible = _pair_mask(q, s_blk)
    la = _dot_split(jnp.where(visible, 1.0, 0.0).astype(BF16), dt * (a * LOG2_E), 3)
    lsd = la - jnp.maximum(jnp.log2(dt), LOG_DT_FLOOR)
    eye_r = lax.broadcasted_iota(jnp.int32, (DT_PAD, DT_PAD), 0)
    eye_c = lax.broadcasted_iota(jnp.int32, (DT_PAD, DT_PAD), 1)
    eye = jnp.where(eye_r == eye_c, 1.0, 0.0).astype(BF16)
    la_t = _transpose_split(eye, la, 3)
    lsd_t = _transpose_split(eye, lsd, 3)
    la_last = la[q - s_blk:q, :]
    la_last_rows = la_last if s_blk == 1 else jnp.concatenate([la_last] * steps, axis=0)
    dec_t = jnp.exp2(la_t[:, q - s_blk:q])

    lane = lax.broadcasted_iota(jnp.int32, (q, 2 * SSM_HEAD_DIM), 1)
    first_head = lane < SSM_HEAD_DIM
    y_parts = []
    for g in range(SSM_GROUPS):
        cg = cm[:, g * SSM_STATE:(g + 1) * SSM_STATE]
        bg = bm[:, g * SSM_STATE:(g + 1) * SSM_STATE]
        cbg = _dot_nt(cg, bg)
        for pair in range(hg // 2):
            h0 = g * hg + 2 * pair
            ws = []
            for h in (h0, h0 + 1):
                seg = la[:, h:h + 1] - lsd_t[h:h + 1, :]
                ws.append(cbg * jnp.exp2(jnp.where(visible, seg, -jnp.inf)))
            xpair = xs[:, h0 * SSM_HEAD_DIM:(h0 + 2) * SSM_HEAD_DIM]
            rhs = jnp.concatenate([jnp.where(first_head, xpair, 0.0), jnp.where(first_head, 0.0, xpair)], axis=0)
            y_parts.append(_dot(jnp.concatenate(ws, axis=1), rhs))
    y = jnp.concatenate(y_parts, axis=1)

    _, seq_b = _row_time_seq((q, SSM_STATE), s_blk)
    y_state_parts = []
    for g in range(SSM_GROUPS):
        cg = cm[:, g * SSM_STATE:(g + 1) * SSM_STATE]
        if s_blk == 1:
            c_seqs, h_seqs = cg, h_src[0, g * gw:(g + 1) * gw, :]
        else:
            c_seqs = jnp.concatenate([jnp.where(seq_b == b, cg, 0.0) for b in range(s_blk)], axis=1)
            h_seqs = jnp.concatenate([h_src[b, g * gw:(g + 1) * gw, :].astype(BF16) for b in range(s_blk)], axis=1)
        y_state_parts.append(_dot_nt(c_seqs, h_seqs))
    y_state = jnp.concatenate(y_state_parts, axis=1)

    e = e_ref[...]
    y = y + y_state * _dot_split_lhs(jnp.exp2(la), e, 2) + xs * dexp_ref[...]
    y = y * z_gate
    y = _rms(y, gn_ref[...], GATED_NORM_EPS)
    y_ref[...] = y.reshape(y_ref.shape).astype(y_ref.dtype)

    tail = jnp.exp2(la_last_rows - lsd)
    xt = (xs * _dot_split_lhs(tail, e, 2)).astype(BF16)
    dec =[jnp.broadcast_to(dec_t[:, b:b + 1], (DT_PAD, SSM_STATE)) for b in range(s_blk)]
    for g in range(SSM_GROUPS):
        bg = bm[:, g * SSM_STATE:(g + 1) * SSM_STATE]
        bg_seqs = bg if s_blk == 1 else jnp.concatenate([jnp.where(seq_b == b, bg, 0.0) for b in range(s_blk)], axis=1)
        upd = _dot_tn(xt[:, g * gw:(g + 1) * gw], bg_seqs)
        for b in range(s_blk):
            for hh in range(hg):
                h = g * hg + hh
                r0 = h * SSM_HEAD_DIM
                hout_ref[b, r0:r0 + SSM_HEAD_DIM, :] = (
                    h_src[b, r0:r0 + SSM_HEAD_DIM, :] * dec[b][h:h + 1, :]
                    + upd[hh * SSM_HEAD_DIM:(hh + 1) * SSM_HEAD_DIM, b * SSM_STATE:(b + 1) * SSM_STATE])


def _ssd(h, gmix, w, h0, conv0, cw, cb, dtb, alog, dexp, gn, e, *, nblk, nc, q, s_blk, time_major):
    rows = SSM_HEADS * SSM_HEAD_DIM
    carry = (SSM_CONV - 1) * s_blk
    per_block0 = h0.shape[0] == nblk * s_blk
    assert per_block0 or (h0.shape[0] == s_blk and conv0.shape[0] == 1)
    spec = functools.partial(_rows_spec, q, s_blk, nblk, nc, time_major=time_major)
    y_shape = (q // s_blk, nblk * s_blk, SSM_D_INNER) if time_major else (nblk * nc * q, SSM_D_INNER)
    return pl.pallas_call(
        functools.partial(_ssd_kernel, s_blk=s_blk, nc=nc),
        grid=(nblk, nc),
        in_specs=[
            spec(D_MODEL),
            spec(D_MODEL, ahead=1),
            _const_spec((1, D_MODEL)),
            _const_spec((D_MODEL, SSDP_DIM)),
            pl.BlockSpec((s_blk, rows, SSM_STATE), lambda s, c: (s if per_block0 else 0, 0, 0)),
            pl.BlockSpec((1, carry, SSM_CONV_DIM), lambda s, c: (s if per_block0 else 0, 0, 0)),
            _const_spec((SSM_CONV, SSM_CONV_DIM)),
            _const_spec((1, SSM_CONV_DIM)),
            _const_spec((1, DT_PAD)),
            _const_spec((1, DT_PAD)),
            _const_spec((1, SSM_D_INNER)),
            _const_spec((1, SSM_D_INNER)),
            _const_spec((DT_PAD, SSM_D_INNER)),
        ],
        out_specs=[
            spec(SSM_D_INNER),
            pl.BlockSpec((s_blk, rows, SSM_STATE), lambda s, c: (s, 0, 0)),
            pl.BlockSpec((1, carry, SSM_CONV_DIM), lambda s, c: (s, 0, 0)),
        ],
        out_shape=[
            jax.ShapeDtypeStruct(y_shape, BF16),
            jax.ShapeDtypeStruct((nblk * s_blk, rows, SSM_STATE), F32),
            jax.ShapeDtypeStruct((nblk, carry, SSM_CONV_DIM), F32),
        ],
        scratch_shapes=[
            pltpu.VMEM((_conv_pad_rows(s_blk) + q, SSM_CONV_DIM), F32),
            pltpu.VMEM((q, SSM_D_INNER + DT_PAD), F32),
            pltpu.VMEM((q, SSDP_DIM), F32),
        ],
        compiler_params=_params(2),
        name="ssd",
    )(h, h, gmix, w, h0, conv0, cw, cb, dtb, alog, dexp, gn, e)


def _ret_log_decay(h):
    return float(np.log1p(-np.exp2(-5.0 - h)))


def _ret_project(h_ref, gmix_ref, w_refs, dst):
    pq, pk, pv, pg = _project(h_ref, gmix_ref, w_refs)
    dst[0][...] = pq
    dst[1][...] = pk
    dst[2][...] = pv.astype(BF16)
    dst[3][...] = pg


def _ret_kernel(*refs, s_blk, nc, n_heads):
    io_refs, set0, set1 = refs[:-8], refs[-8:-4], refs[-4:]
    hc_ref, _, gmix_ref = io_refs[0:3]
    w_refs = io_refs[3:7]
    s = pl.program_id(1)
    c = pl.program_id(2)

    @pl.when(jnp.logical_and(s == 0, c == 0))
    def _first():
        _ret_project(hc_ref, gmix_ref, w_refs, set0)

    parity = (s * nc + c) % 2

    @pl.when(parity == 0)
    def _even():
        _ret_step(*io_refs, set0, set1, s_blk=s_blk, nc=nc, n_heads=n_heads)

    @pl.when(parity == 1)
    def _odd():
        _ret_step(*io_refs, set1, set0, s_blk=s_blk, nc=nc, n_heads=n_heads)


def _ret_step(hc_ref, hn_ref, gmix_ref, wq_ref, wk_ref, wv_ref, wg_ref, cos_ref, sin_ref, s0_ref, gn_ref,
              y_ref, sout_ref, cur_set, next_set, *, s_blk, nc, n_heads):
    q_scr, k_scr, v_scr, g_scr = cur_set
    hb = pl.program_id(0)
    c = pl.program_id(2)
    q = cos_ref.shape[0]
    steps = q // s_blk
    half = RET_QK_DIM // 2
    s_src = s0_ref if nc == 1 else sout_ref

    if nc > 1:
        @pl.when(c == 0)
        def _init():
            sout_ref[...] = s0_ref[...]

    _ret_project(hn_ref, gmix_ref, [wq_ref, wk_ref, wv_ref, wg_ref], next_set)

    qq = q_scr[...]
    kk = k_scr[...]
    vv = v_scr[...]
    rg = g_scr[...]
    cos = cos_ref[...]
    sin = sin_ref[...]
    visible = _pair_mask(q, s_blk)
    r = lax.broadcasted_iota(jnp.int32, (q, q), 0)
    cc = lax.broadcasted_iota(jnp.int32, (q, q), 1)
    rel = (r // s_blk - cc // s_blk).astype(F32)
    t_qk, seq_qk = _row_time_seq((q, RET_QK_DIM), s_blk)
    t_qk = t_qk.astype(F32)

    def rot(t, h):
        t1 = t[:, h * RET_QK_DIM:h * RET_QK_DIM + half]
        t2 = t[:, h * RET_QK_DIM + half:(h + 1) * RET_QK_DIM]
        return jnp.concatenate([t1 * cos - t2 * sin, t1 * sin + t2 * cos], axis=1)

    y_parts = []
    for h in range(n_heads):
        if n_heads == RET_HEADS:
            lg = _ret_log_decay(h)
            chunk_dec = math.exp(steps * lg)
        else:
            assert n_heads == 1
            lg = jnp.float32(_ret_log_decay(0))
            chunk_dec = jnp.float32(math.exp(steps * _ret_log_decay(0)))
            for hh in range(1, RET_HEADS):
                lg = jnp.where(hb == hh, jnp.float32(_ret_log_decay(hh)), lg)
                chunk_dec = jnp.where(hb == hh, jnp.float32(math.exp(steps * _ret_log_decay(hh))), chunk_dec)
        dmat = jnp.where(visible, jnp.exp(jnp.maximum(rel, 0.0) * lg), 0.0)
        q_dec = jnp.exp((t_qk + 1.0) * lg)
        k_dec = jnp.exp((steps - 1.0 - t_qk) * lg)
        qr = rot(qq, h)
        kr = rot(kk, h) * (RET_QK_DIM ** -0.5)
        vh = vv[:, h * RET_V_DIM:(h + 1) * RET_V_DIM]
        scores = _dot_nt(qr, kr) * dmat
        y = _dot(scores, vh)
        qd = qr * q_dec
        kd = kr * k_dec
        r0 = h * RET_QK_DIM
        for b in range(s_blk):
            qd_b = (qd if s_blk == 1 else jnp.where(seq_qk == b, qd, 0.0)).astype(BF16)
            kd_b = (kd if s_blk == 1 else jnp.where(seq_qk == b, kd, 0.0)).astype(BF16)
            sh = s_src[b, r0:r0 + RET_QK_DIM, :]
            y = y + _dot(qd_b, sh)
            sout_ref[b, r0:r0 + RET_QK_DIM, :] = sh * chunk_dec + _dot_tn(kd_b, vh)
        g = gn_ref[:, h * RET_V_DIM:(h + 1) * RET_V_DIM]
        y = _rms(y, g, NORM_EPS) * _silu(rg[:, h * RET_V_DIM:(h + 1) * RET_V_DIM])
        y_parts.append(y)
    y = y_parts[0] if n_heads == 1 else jnp.concatenate(y_parts, axis=1)
    y_ref[...] = y.reshape(y_ref.shape).astype(y_ref.dtype)


def _ret(h, gmix, wq, wk, wv, wg, cos, sin, s0, gn, *, nblk, nc, q, s_blk, n_heads, time_major):
    nhb = RET_HEADS // n_heads
    qk_w = n_heads * RET_QK_DIM
    v_w = n_heads * RET_V_DIM
    per_block0 = s0.shape[0] == nblk * s_blk
    assert per_block0 or s0.shape[0] == s_blk
    rows = functools.partial(_rows_spec, q, s_blk, nblk, nc, D_MODEL, time_major, lead_axes=1)

    def head_cols(nrows, width):
        if nhb == 1:
            return _const_spec((nrows, width))
        return pl.BlockSpec((nrows, width), lambda hb, s, c: (0, hb))

    if time_major:
        y_spec = pl.BlockSpec((q // s_blk, s_blk, v_w), lambda hb, s, c: (0, s, hb))
        y_shape = (q // s_blk, nblk * s_blk, RET_V)
    else:
        y_spec = pl.BlockSpec((q, v_w), lambda hb, s, c: (s * nc + c, hb))
        y_shape = (nblk * nc * q, RET_V)
    return pl.pallas_call(
        functools.partial(_ret_kernel, s_blk=s_blk, nc=nc, n_heads=n_heads),
        grid=(nhb, nblk, nc),
        in_specs=[
            rows(),
            rows(ahead=1),
            _const_spec((1, D_MODEL)),
            head_cols(D_MODEL, qk_w),
            head_cols(D_MODEL, qk_w),
            head_cols(D_MODEL, v_w),
            head_cols(D_MODEL, v_w),
            pl.BlockSpec((q, V7X_LANES), lambda hb, s, c: (c, 0)),
            pl.BlockSpec((q, V7X_LANES), lambda hb, s, c: (c, 0)),
            pl.BlockSpec((s_blk, qk_w, RET_V_DIM), lambda hb, s, c: (s if per_block0 else 0, hb, 0)),
            head_cols(1, v_w),
        ],
        out_specs=[
            y_spec,
            pl.BlockSpec((s_blk, qk_w, RET_V_DIM), lambda hb, s, c: (s, hb, 0)),
        ],
        out_shape=[
            jax.ShapeDtypeStruct(y_shape, BF16),
            jax.ShapeDtypeStruct((nblk * s_blk, RET_HEADS * RET_QK_DIM, RET_V_DIM), F32),
        ],
        scratch_shapes=[
            pltpu.VMEM((q, qk_w), F32),
            pltpu.VMEM((q, qk_w), F32),
            pltpu.VMEM((q, v_w), BF16),
            pltpu.VMEM((q, v_w), F32),
        ] * 2,
        compiler_params=_params(3),
        name="ret",
    )(h, h, gmix, wq, wk, wv, wg, cos, sin, s0, gn)


def _merge_kernel(h_ref, ys_ref, yr_ref, g_ref, wg_ref, wbs_ref, wbr_ref, wo_ref, o_ref):
    h = h_ref[...]
    u = _rms(h, g_ref[...], NORM_EPS).astype(BF16)
    gates = jnp.dot(u, wg_ref[...], preferred_element_type=F32)
    branch_ssm = jnp.dot(ys_ref[...], wbs_ref[...], preferred_element_type=F32)
    branch_ret = jnp.dot(yr_ref[...], wbr_ref[...], preferred_element_type=F32)
    merged = _sigmoid(gates[:, 0:D_MODEL]) * branch_ssm + _sigmoid(gates[:, D_MODEL:2 * D_MODEL]) * branch_ret
    o_ref[...] = h + jnp.dot(merged.astype(BF16), wo_ref[...], preferred_element_type=F32)


def _merge(h, ys, yr, g, wg, wbs, wbr, wo, *, tm):
    t = h.shape[0]
    assert t % tm == 0
    return pl.pallas_call(
        _merge_kernel,
        grid=(t // tm,),
        in_specs=[
            pl.BlockSpec((tm, D_MODEL), lambda i: (i, 0)),
            pl.BlockSpec((tm, SSM_D_INNER), lambda i: (i, 0)),
            pl.BlockSpec((tm, RET_V), lambda i: (i, 0)),
            _const_spec((1, D_MODEL)),
            _const_spec((D_MODEL, 2 * D_MODEL)),
            _const_spec((SSM_D_INNER, D_MODEL)),
            _const_spec((RET_V, D_MODEL)),
            _const_spec((D_MODEL, D_MODEL)),
        ],
        out_specs=pl.BlockSpec((tm, D_MODEL), lambda i: (i, 0)),
        out_shape=jax.ShapeDtypeStruct((t, D_MODEL), F32),
        compiler_params=_params(1),
        name="merge",
    )(h, ys, yr, g, wg, wbs, wbr, wo)


FFN_TILE = 512
SAMPLE_SEQS_PER_STEP = V7X_SUBLANES
SAMPLE_RET_HEADS_PER_STEP = 1


def _layer(x2, pos0, ssm0, conv0, ret0, w, final_gain, *, nblk, nc, q, s_blk, ret_heads, time_major):
    t = x2.shape[0]
    tm_ffn = min(FFN_TILE, t)
    steps = q // s_blk
    nseq = nblk * s_blk

    h = _ffn(x2, w["norm_ffn1"], w["ffn1_w1"], w["ffn1_w3"], w["ffn1_w2"], final_gain, final_norm=False, tm=tm_ffn)
    h_rows = h.reshape(steps, nseq, D_MODEL) if time_major else h

    y_ssm, ssm_new, conv_new = _ssd(
        h_rows, w["norm_mix"], w["w_ssd"], ssm0, conv0, w["conv_w"], w["conv_b"], w["dt_bias"], w["a_log"],
        w["d_exp"], w["ssm_norm"], w["head_expand"], nblk=nblk, nc=nc, q=q, s_blk=s_blk, time_major=time_major)
    cos, sin = _rope_tables(w["rope_inv"], nc * q, pos0, s_blk)
    y_ret, ret_new = _ret(h_rows, w["norm_mix"], w["w_q"], w["w_k"], w["w_v"], w["w_rg"], cos, sin, ret0,
                          w["ret_norm"], nblk=nblk, nc=nc, q=q, s_blk=s_blk, n_heads=ret_heads,
                          time_major=time_major)

    h = _merge(h, y_ssm.reshape(t, SSM_D_INNER), y_ret.reshape(t, RET_V), w["norm_mix"], w["w_gates"],
               w["w_branch_ssm"], w["w_branch_ret"], w["w_out"], tm=tm_ffn)
    y = _ffn(h, w["norm_ffn2"], w["ffn2_w1"], w["ffn2_w3"], w["ffn2_w2"], final_gain, final_norm=True, tm=tm_ffn)
    return y, ssm_new, conv_new, ret_new


def _prompt_layer(x, w, final_gain):
    nseq, seq_len, _ = x.shape
    q = math.gcd(seq_len, CHUNK)
    y, ssm_new, conv_new, ret_new = _layer(
        x.reshape(nseq * seq_len, D_MODEL), 0.0,
        jnp.zeros((1, SSM_HEADS * SSM_HEAD_DIM, SSM_STATE), F32),
        jnp.zeros((1, SSM_CONV - 1, SSM_CONV_DIM), F32),
        jnp.zeros((1, RET_HEADS * RET_QK_DIM, RET_V_DIM), F32),
        w, final_gain, nblk=nseq, nc=seq_len // q, q=q, s_blk=1, ret_heads=RET_HEADS, time_major=False)
    return (y.reshape(nseq, seq_len, D_MODEL), ssm_new.reshape(nseq, SSM_HEADS, SSM_HEAD_DIM, SSM_STATE), conv_new,
            ret_new.reshape(nseq, RET_HEADS, RET_QK_DIM, RET_V_DIM))


def _sample_layer(x, ssm0, conv0, ret0, w, final_gain):
    nseq, seq_len, _ = x.shape
    s_blk = SAMPLE_SEQS_PER_STEP
    assert nseq % s_blk == 0 and seq_len <= CHUNK
    nblk = nseq // s_blk
    kc = SSM_CONV - 1
    x_tm = jnp.transpose(x, (1, 0, 2)).reshape(seq_len * nseq, D_MODEL)
    conv0_tm = jnp.transpose(conv0.reshape(nblk, s_blk, kc, SSM_CONV_DIM), (0, 2, 1, 3)).reshape(
        nblk, kc * s_blk, SSM_CONV_DIM)
    y, ssm_new, conv_new, ret_new = _layer(
        x_tm, float(PAST_LEN), ssm0.reshape(nseq, SSM_HEADS * SSM_HEAD_DIM, SSM_STATE), conv0_tm,
        ret0.reshape(nseq, RET_HEADS * RET_QK_DIM, RET_V_DIM), w, final_gain,
        nblk=nblk, nc=1, q=s_blk * seq_len, s_blk=s_blk, ret_heads=SAMPLE_RET_HEADS_PER_STEP, time_major=True)
    conv_new = jnp.transpose(conv_new.reshape(nblk, kc, s_blk, SSM_CONV_DIM), (0, 2, 1, 3)).reshape(
        nseq, kc, SSM_CONV_DIM)
    return (jnp.transpose(y.reshape(seq_len, nseq, D_MODEL), (1, 0, 2)),
            ssm_new.reshape(nseq, SSM_HEADS, SSM_HEAD_DIM, SSM_STATE), conv_new,
            ret_new.reshape(nseq, RET_HEADS, RET_QK_DIM, RET_V_DIM))


def _prep_weights(norm_ffn1, ffn1_w1, ffn1_w3, ffn1_w2, norm_mix, w_in, conv_w, conv_b, dt_bias, a_log, ssm_d,
                  ssm_norm, ret_norm, w_branch_ssm, w_branch_ret, w_out, norm_ffn2, ffn2_w1, ffn2_w3, ffn2_w2):
    o_z = 0
    o_xbc = o_z + SSM_D_INNER
    o_dt = o_xbc + SSM_CONV_DIM
    o_q = o_dt + SSM_HEADS
    o_k = o_q + RET_QK
    o_v = o_k + RET_QK
    o_rg = o_v + RET_V
    o_ga = o_rg + RET_V
    o_end = o_ga + 2 * D_MODEL
    assert o_end == w_in.shape[1]

    w_in_bf = w_in.astype(BF16)

    def cols(a, b):
        return w_in_bf[:, a:b]

    assert (o_z, o_xbc, o_dt) == (SSDP_Z, SSDP_XBC, SSDP_DT) and SSDP_DIM <= o_end
    half = RET_QK_DIM // 2
    head_of_channel = jnp.arange(SSM_D_INNER, dtype=jnp.int32) // SSM_HEAD_DIM
    return {
        "norm_ffn1": norm_ffn1.reshape(1, -1), "ffn1_w1": ffn1_w1.astype(BF16), "ffn1_w3": ffn1_w3.astype(BF16),
        "ffn1_w2": ffn1_w2.astype(BF16),
        "norm_mix": norm_mix.reshape(1, -1), "w_ssd": w_in_bf,
        "w_q": cols(o_q, o_k), "w_k": cols(o_k, o_v), "w_v": cols(o_v, o_rg), "w_rg": cols(o_rg, o_ga),
        "w_gates": cols(o_ga, o_end),
        "conv_w": conv_w, "conv_b": conv_b.reshape(1, -1),
        "dt_bias": jnp.pad(dt_bias, (0, DT_PAD - SSM_HEADS)).reshape(1, -1),
        "a_log": jnp.pad(a_log, (0, DT_PAD - SSM_HEADS)).reshape(1, -1),
        "d_exp": jnp.repeat(ssm_d, SSM_HEAD_DIM).reshape(1, -1),
        "ssm_norm": ssm_norm.reshape(1, -1), "ret_norm": ret_norm.reshape(1, -1),
        "head_expand": (head_of_channel[None, :] == jnp.arange(DT_PAD, dtype=jnp.int32)[:, None]).astype(BF16),
        "rope_inv": (ROPE_BASE ** (-jnp.arange(half, dtype=F32) / half)).reshape(1, half),
        "w_branch_ssm": w_branch_ssm.astype(BF16), "w_branch_ret": w_branch_ret.astype(BF16),
        "w_out": w_out.astype(BF16),
        "norm_ffn2": norm_ffn2.reshape(1, -1), "ffn2_w1": ffn2_w1.astype(BF16), "ffn2_w3": ffn2_w3.astype(BF16),
        "ffn2_w2": ffn2_w2.astype(BF16),
    }


def kernel(x_prompt, x_sample, state_ssm, state_conv, state_ret, norm_ffn1, ffn1_w1, ffn1_w3, ffn1_w2, norm_mix, w_in,
           conv_w, conv_b, dt_bias, a_log, ssm_d, ssm_norm, ret_norm, w_branch_ssm, w_branch_ret, w_out, norm_ffn2,
           ffn2_w1, ffn2_w3, ffn2_w2, norm_final):
    depth = norm_ffn1.shape[0]
    assert depth == 1, "the final RMSNorm is fused into the last layer's second FFN"
    w = _prep_weights(norm_ffn1[0], ffn1_w1[0], ffn1_w3[0], ffn1_w2[0], norm_mix[0], w_in[0], conv_w[0], conv_b[0],
                      dt_bias[0], a_log[0], ssm_d[0], ssm_norm[0], ret_norm[0], w_branch_ssm[0], w_branch_ret[0],
                      w_out[0], norm_ffn2[0], ffn2_w1[0], ffn2_w3[0], ffn2_w2[0])
    final_gain = norm_final.reshape(1, -1)
    yp, ssm_p, conv_p, ret_p = _prompt_layer(x_prompt, w, final_gain)
    ys, ssm_s, conv_s, ret_s = _sample_layer(x_sample, state_ssm[0], state_conv[0], state_ret[0], w, final_gain)
    return (yp, ys, ssm_p[None], conv_p[None], ret_p[None], ssm_s[None], conv_s[None], ret_s[None])
```

```python
import functools
import math

import jax
import jax.numpy as jnp
import numpy as np
from jax import lax
from jax.experimental import pallas as pl
from jax.experimental.pallas import tpu as pltpu

F32 = jnp.float32
BF16 = jnp.bfloat16

D_MODEL = 1024
D_FF = 2816
SSM_D_INNER = 2048
SSM_HEAD_DIM = 64
SSM_HEADS = 32
SSM_GROUPS = 4
SSM_STATE = 128
SSM_CONV = 4
SSM_BC = 2 * SSM_GROUPS * SSM_STATE
SSM_CONV_DIM = SSM_D_INNER + SSM_BC
RET_HEADS = 4
RET_QK_DIM = 256
RET_V_DIM = 512
RET_QK = RET_HEADS * RET_QK_DIM
RET_V = RET_HEADS * RET_V_DIM
ROPE_BASE = 10000.0
PAST_LEN = 16384
CHUNK = 128
NORM_EPS = 1e-6
GATED_NORM_EPS = 1e-5

V7X_SUBLANES = 8
V7X_LANES = 128
V7X_VMEM_BYTES = 64 * 1024 * 1024
VMEM_LIMIT = V7X_VMEM_BYTES - 8 * 1024 * 1024

DT_PAD = V7X_LANES
LOG_DT_FLOOR = -1e30
LOG2_E = math.log2(math.e)

SSDP_Z = 0
SSDP_XBC = SSM_D_INNER
SSDP_DT = SSM_D_INNER + SSM_CONV_DIM
SSDP_DIM = SSDP_DT + DT_PAD


def _rms(x, g, eps):
    return x * lax.rsqrt(jnp.mean(x * x, axis=-1, keepdims=True) + eps) * g


def _sigmoid(x):
    return 0.5 + 0.5 * jnp.tanh(0.5 * x)


def _silu(x):
    half = 0.5 * x
    return half + half * jnp.tanh(half)


def _softplus(x):
    return jnp.maximum(x, 0.0) + jnp.log1p(jnp.exp(-jnp.abs(x)))


def _dot(a, b):
    return jnp.dot(a.astype(BF16), b.astype(BF16), preferred_element_type=F32)


def _dot_nt(a, b):
    return lax.dot_general(a.astype(BF16), b.astype(BF16), (((1,), (1,)), ((), ())), preferred_element_type=F32)


def _dot_tn(a, b):
    return lax.dot_general(a.astype(BF16), b.astype(BF16), (((0,), (0,)), ((), ())), preferred_element_type=F32)


def _dot_split(a, b_f32, passes):
    acc = None
    rem = b_f32
    for _ in range(passes):
        piece = rem.astype(BF16)
        term = jnp.dot(a, piece, preferred_element_type=F32)
        acc = term if acc is None else acc + term
        rem = rem - piece.astype(F32)
    return acc


def _dot_split_lhs(a_f32, b, passes):
    acc = None
    rem = a_f32
    for _ in range(passes):
        piece = rem.astype(BF16)
        term = jnp.dot(piece, b, preferred_element_type=F32)
        acc = term if acc is None else acc + term
        rem = rem - piece.astype(F32)
    return acc


def _transpose_split(eye, x_f32, passes):
    acc = None
    rem = x_f32
    for _ in range(passes):
        piece = rem.astype(BF16)
        term = lax.dot_general(eye, piece, (((1,), (1,)), ((), ())), preferred_element_type=F32)
        acc = term if acc is None else acc + term
        rem = rem - piece.astype(F32)
    return acc


def _const_spec(shape):
    nd = len(shape)
    return pl.BlockSpec(shape, lambda *_: (0,) * nd, pipeline_mode=pl.Buffered(1))


def _params(n_grid_dims, flags=None):
    return pltpu.CompilerParams(dimension_semantics=("arbitrary",) * n_grid_dims, vmem_limit_bytes=VMEM_LIMIT,
                                flags=flags)


def _ffn_kernel(x_ref, g_ref, w1_ref, w3_ref, w2_ref, gf_ref, o_ref, *, final_norm):
    x = x_ref[...]
    xn = _rms(x, g_ref[...], NORM_EPS).astype(BF16)
    a = jnp.dot(xn, w1_ref[...], preferred_element_type=F32)
    b = jnp.dot(xn, w3_ref[...], preferred_element_type=F32)
    gated = (_silu(a) * b).astype(BF16)
    h = x + 0.5 * jnp.dot(gated, w2_ref[...], preferred_element_type=F32)
    if final_norm:
        h = _rms(h, gf_ref[...], NORM_EPS)
    o_ref[...] = h


def _ffn(x, g, w1, w3, w2, gf, *, final_norm, tm):
    t = x.shape[0]
    assert t % tm == 0
    return pl.pallas_call(
        functools.partial(_ffn_kernel, final_norm=final_norm),
        grid=(t // tm,),
        in_specs=[
            pl.BlockSpec((tm, D_MODEL), lambda i: (i, 0)),
            _const_spec((1, D_MODEL)),
            _const_spec((D_MODEL, D_FF)),
            _const_spec((D_MODEL, D_FF)),
            _const_spec((D_FF, D_MODEL)),
            _const_spec((1, D_MODEL)),
        ],
        out_specs=pl.BlockSpec((tm, D_MODEL), lambda i: (i, 0)),
        out_shape=jax.ShapeDtypeStruct((t, D_MODEL), F32),
        compiler_params=_params(1),
        name="ffn_final" if final_norm else "ffn",
    )(x, g, w1, w3, w2, gf)


def _rope_kernel(inv_ref, cos_ref, sin_ref, *, pos0, rows_per_pos):
    rows = cos_ref.shape[0]
    step = lax.broadcasted_iota(jnp.int32, (rows, V7X_LANES), 0) // rows_per_pos
    ang = (pos0 + step.astype(F32)) * inv_ref[...]
    cos_ref[...] = jnp.cos(ang)
    sin_ref[...] = jnp.sin(ang)


def _rope_tables(inv, rows, pos0, rows_per_pos):
    return pl.pallas_call(
        functools.partial(_rope_kernel, pos0=pos0, rows_per_pos=rows_per_pos),
        out_shape=[jax.ShapeDtypeStruct((rows, V7X_LANES), F32)] * 2,
        name="rope",
    )(inv)


def _chunk_rows(ref):
    return ref[...].reshape(-1, ref.shape[-1])


def _row_time_seq(shape, s_blk):
    r = lax.broadcasted_iota(jnp.int32, shape, 0)
    return r // s_blk, r % s_blk


def _pair_mask(q, s_blk):
    r = lax.broadcasted_iota(jnp.int32, (q, q), 0)
    c = lax.broadcasted_iota(jnp.int32, (q, q), 1)
    if s_blk == 1:
        return r >= c
    return jnp.logical_and(r // s_blk >= c // s_blk, r % s_blk == c % s_blk)


def _rows_spec(q, s_blk, nblk, nc, width, time_major, *, ahead=0, lead_axes=0):
    last = nblk * nc - 1

    def chunk_index(args):
        s, c = args[lead_axes], args[lead_axes + 1]
        return jnp.minimum(s * nc + c + ahead, last)

    if time_major:
        assert nc == 1
        return pl.BlockSpec((q // s_blk, s_blk, width), lambda *a: (0, chunk_index(a), 0))
    return pl.BlockSpec((q, width), lambda *a: (chunk_index(a), 0))


def _project(h_ref, g_ref, w_refs):
    u = _rms(_chunk_rows(h_ref), g_ref[...], NORM_EPS).astype(BF16)
    return [jnp.dot(u, w_ref[...], preferred_element_type=F32) for w_ref in w_refs]


def _conv_pad_rows(s_blk):
    need = (SSM_CONV - 1) * s_blk
    return -(-need // V7X_SUBLANES) * V7X_SUBLANES


def _ssd_kernel(hc_ref, hn_ref, gmix_ref, w_ref, h0_ref, conv0_ref, cw_ref, cb_ref, dtb_ref, alog_ref, dexp_ref,
                gn_ref, e_ref, y_ref, hout_ref, convout_ref, xp_scr, zdt_scr, next_scr, *, s_blk, nc):
    s = pl.program_id(0)
    c = pl.program_id(1)
    pad = _conv_pad_rows(s_blk)
    q = xp_scr.shape[0] - pad
    carry = (SSM_CONV - 1) * s_blk
    steps = q // s_blk
    hg = SSM_HEADS // SSM_GROUPS
    gw = hg * SSM_HEAD_DIM
    h_src = h0_ref if nc == 1 else hout_ref

    @pl.when(jnp.logical_and(s == 0, c == 0))
    def _first():
        next_scr[...] = _project(hc_ref, gmix_ref, [w_ref])[0]

    @pl.when(c == 0)
    def _init():
        xp_scr[0:pad, :] = jnp.zeros((pad, SSM_CONV_DIM), F32)
        xp_scr[pad - carry:pad, :] = conv0_ref[0]
        if nc > 1:
            hout_ref[...] = h0_ref[...]

    xp_scr[pad:pad + q, :] = next_scr[:, SSDP_XBC:SSDP_DT]
    zdt_scr[:, 0:SSM_D_INNER] = next_scr[:, SSDP_Z:SSDP_XBC]
    zdt_scr[:, SSM_D_INNER:SSM_D_INNER + DT_PAD] = next_scr[:, SSDP_DT:SSDP_DIM]
    next_scr[...] = _project(hn_ref, gmix_ref, [w_ref])[0]

    z_gate = _silu(zdt_scr[:, 0:SSM_D_INNER])

    xall = xp_scr[...]
    conv = cb_ref[...] + xall[pad:pad + q, :] * cw_ref[SSM_CONV - 1:SSM_CONV, :]
    for j in range(1, SSM_CONV):
        d = j * s_blk
        if d % V7X_SUBLANES == 0:
            back = xall[pad - d:pad - d + q, :]
        else:
            back = pltpu.roll(xall, d, 0)[pad:pad + q, :]
        conv = conv + back * cw_ref[SSM_CONV - 1 - j:SSM_CONV - j, :]
    @pl.when(c == nc - 1)
    def _():
        convout_ref[0] = xp_scr[pad + q - carry:pad + q, :]

    if nc > 1:
        xp_scr[0:pad, :] = xp_scr[q:q + pad, :]
    dt_rows = zdt_scr[:, SSM_D_INNER:SSM_D_INNER + DT_PAD]
    xc = _silu(conv)
    xs = xc[:, 0:SSM_D_INNER]
    bm = xc[:, SSM_D_INNER:SSM_D_INNER + SSM_GROUPS * SSM_STATE]
    cm = xc[:, SSM_D_INNER + SSM_GROUPS * SSM_STATE:SSM_CONV_DIM]

    dt = _softplus(dt_rows + dtb_ref[...])
    a = -jnp.exp(alog_ref[...])
    visible = _pair_mask(q, s_blk)
    la = _dot_split(jnp.where(visible, 1.0, 0.0).astype(BF16), dt * (a * LOG2_E), 3)
    lsd = la - jnp.maximum(jnp.log2(dt), LOG_DT_FLOOR)
    eye_r = lax.broadcasted_iota(jnp.int32, (DT_PAD, DT_PAD), 0)
    eye_c = lax.broadcasted_iota(jnp.int32, (DT_PAD, DT_PAD), 1)
    eye = jnp.where(eye_r == eye_c, 1.0, 0.0).astype(BF16)
    la_t = _transpose_split(eye, la, 3)
    lsd_t = _transpose_split(eye, lsd, 3)
    la_last = la[q - s_blk:q, :]
    la_last_rows = la_last if s_blk == 1 else jnp.concatenate([la_last] * steps, axis=0)
    dec_t = jnp.exp2(la_t[:, q - s_blk:q])

    lane = lax.broadcasted_iota(jnp.int32, (q, 2 * SSM_HEAD_DIM), 1)
    first_head = lane < SSM_HEAD_DIM
    y_parts = []
    for g in range(SSM_GROUPS):
        cg = cm[:, g * SSM_STATE:(g + 1) * SSM_STATE]
        bg = bm[:, g * SSM_STATE:(g + 1) * SSM_STATE]
        cbg = _dot_nt(cg, bg)
        for pair in range(hg // 2):
            h0 = g * hg + 2 * pair
            ws = []
            for h in (h0, h0 + 1):
                seg = la[:, h:h + 1] - lsd_t[h:h + 1, :]
                ws.append(cbg * jnp.exp2(jnp.where(visible, seg, -jnp.inf)))
            xpair = xs[:, h0 * SSM_HEAD_DIM:(h0 + 2) * SSM_HEAD_DIM]
            rhs = jnp.concatenate([jnp.where(first_head, xpair, 0.0), jnp.where(first_head, 0.0, xpair)], axis=0)
            y_parts.append(_dot(jnp.concatenate(ws, axis=1), rhs))
    y = jnp.concatenate(y_parts, axis=1)

    _, seq_b = _row_time_seq((q, SSM_STATE), s_blk)
    y_state_parts = []
    for g in range(SSM_GROUPS):
        cg = cm[:, g * SSM_STATE:(g + 1) * SSM_STATE]
        if s_blk == 1:
            c_seqs, h_seqs = cg, h_src[0, g * gw:(g + 1) * gw, :]
        else:
            c_seqs = jnp.concatenate([jnp.where(seq_b == b, cg, 0.0) for b in range(s_blk)], axis=1)
            h_seqs = jnp.concatenate([h_src[b, g * gw:(g + 1) * gw, :].astype(BF16) for b in range(s_blk)], axis=1)
        y_state_parts.append(_dot_nt(c_seqs, h_seqs))
    y_state = jnp.concatenate(y_state_parts, axis=1)

    e = e_ref[...]
    y = y + y_state * _dot_split_lhs(jnp.exp2(la), e, 2) + xs * dexp_ref[...]
    y = y * z_gate
    y = _rms(y, gn_ref[...], GATED_NORM_EPS)
    y_ref[...] = y.reshape(y_ref.shape).astype(y_ref.dtype)

    tail = jnp.exp2(la_last_rows - lsd)
    xt = (xs * _dot_split_lhs(tail, e, 2)).astype(BF16)
    dec =[jnp.broadcast_to(dec_t[:, b:b + 1], (DT_PAD, SSM_STATE)) for b in range(s_blk)]
    for g in range(SSM_GROUPS):
        bg = bm[:, g * SSM_STATE:(g + 1) * SSM_STATE]
        bg_seqs = bg if s_blk == 1 else jnp.concatenate([jnp.where(seq_b == b, bg, 0.0) for b in range(s_blk)], axis=1)
        upd = _dot_tn(xt[:, g * gw:(g + 1) * gw], bg_seqs)
        for b in range(s_blk):
            for hh in range(hg):
                h = g * hg + hh
                r0 = h * SSM_HEAD_DIM
                hout_ref[b, r0:r0 + SSM_HEAD_DIM, :] = (
                    h_src[b, r0:r0 + SSM_HEAD_DIM, :] * dec[b][h:h + 1, :]
                    + upd[hh * SSM_HEAD_DIM:(hh + 1) * SSM_HEAD_DIM, b * SSM_STATE:(b + 1) * SSM_STATE])


def _ssd(h, gmix, w, h0, conv0, cw, cb, dtb, alog, dexp, gn, e, *, nblk, nc, q, s_blk, time_major):
    rows = SSM_HEADS * SSM_HEAD_DIM
    carry = (SSM_CONV - 1) * s_blk
    per_block0 = h0.shape[0] == nblk * s_blk
    assert per_block0 or (h0.shape[0] == s_blk and conv0.shape[0] == 1)
    spec = functools.partial(_rows_spec, q, s_blk, nblk, nc, time_major=time_major)
    y_shape = (q // s_blk, nblk * s_blk, SSM_D_INNER) if time_major else (nblk * nc * q, SSM_D_INNER)
    return pl.pallas_call(
        functools.partial(_ssd_kernel, s_blk=s_blk, nc=nc),
        grid=(nblk, nc),
        in_specs=[
            spec(D_MODEL),
            spec(D_MODEL, ahead=1),
            _const_spec((1, D_MODEL)),
            _const_spec((D_MODEL, SSDP_DIM)),
            pl.BlockSpec((s_blk, rows, SSM_STATE), lambda s, c: (s if per_block0 else 0, 0, 0)),
            pl.BlockSpec((1, carry, SSM_CONV_DIM), lambda s, c: (s if per_block0 else 0, 0, 0)),
            _const_spec((SSM_CONV, SSM_CONV_DIM)),
            _const_spec((1, SSM_CONV_DIM)),
            _const_spec((1, DT_PAD)),
            _const_spec((1, DT_PAD)),
            _const_spec((1, SSM_D_INNER)),
            _const_spec((1, SSM_D_INNER)),
            _const_spec((DT_PAD, SSM_D_INNER)),
        ],
        out_specs=[
            spec(SSM_D_INNER),
            pl.BlockSpec((s_blk, rows, SSM_STATE), lambda s, c: (s, 0, 0)),
            pl.BlockSpec((1, carry, SSM_CONV_DIM), lambda s, c: (s, 0, 0)),
        ],
        out_shape=[
            jax.ShapeDtypeStruct(y_shape, BF16),
            jax.ShapeDtypeStruct((nblk * s_blk, rows, SSM_STATE), F32),
            jax.ShapeDtypeStruct((nblk, carry, SSM_CONV_DIM), F32),
        ],
        scratch_shapes=[
            pltpu.VMEM((_conv_pad_rows(s_blk) + q, SSM_CONV_DIM), F32),
            pltpu.VMEM((q, SSM_D_INNER + DT_PAD), F32),
            pltpu.VMEM((q, SSDP_DIM), F32),
        ],
        compiler_params=_params(2),
        name="ssd",
    )(h, h, gmix, w, h0, conv0, cw, cb, dtb, alog, dexp, gn, e)


def _ret_log_decay(h):
    return float(np.log1p(-np.exp2(-5.0 - h)))


def _ret_project(h_ref, gmix_ref, w_refs, dst):
    pq, pk, pv, pg = _project(h_ref, gmix_ref, w_refs)
    dst[0][...] = pq
    dst[1][...] = pk
    dst[2][...] = pv.astype(BF16)
    dst[3][...] = pg


def _ret_kernel(*refs, s_blk, nc, n_heads):
    io_refs, set0, set1 = refs[:-8], refs[-8:-4], refs[-4:]
    hc_ref, _, gmix_ref = io_refs[0:3]
    w_refs = io_refs[3:7]
    s = pl.program_id(1)
    c = pl.program_id(2)

    @pl.when(jnp.logical_and(s == 0, c == 0))
    def _first():
        _ret_project(hc_ref, gmix_ref, w_refs, set0)

    parity = (s * nc + c) % 2

    @pl.when(parity == 0)
    def _even():
        _ret_step(*io_refs, set0, set1, s_blk=s_blk, nc=nc, n_heads=n_heads)

    @pl.when(parity == 1)
    def _odd():
        _ret_step(*io_refs, set1, set0, s_blk=s_blk, nc=nc, n_heads=n_heads)


def _ret_step(hc_ref, hn_ref, gmix_ref, wq_ref, wk_ref, wv_ref, wg_ref, cos_ref, sin_ref, dmat_ref, qdec_ref, kdec_ref,
              s0_ref, gn_ref, y_ref, sout_ref, cur_set, next_set, *, s_blk, nc, n_heads):
    q_scr, k_scr, v_scr, g_scr = cur_set
    hb = pl.program_id(0)
    c = pl.program_id(2)
    q = cos_ref.shape[0]
    steps = q // s_blk
    half = RET_QK_DIM // 2
    s_src = s0_ref if nc == 1 else sout_ref

    if nc > 1:
        @pl.when(c == 0)
        def _init():
            sout_ref[...] = s0_ref[...]

    _ret_project(hn_ref, gmix_ref, [wq_ref, wk_ref, wv_ref, wg_ref], next_set)

    qq = q_scr[...]
    kk = k_scr[...]
    vv = v_scr[...]
    rg = g_scr[...]
    cos = cos_ref[...]
    sin = sin_ref[...]
    _, seq_qk = _row_time_seq((q, RET_QK_DIM), s_blk)

    def rot(t, h):
        t1 = t[:, h * RET_QK_DIM:h * RET_QK_DIM + half]
        t2 = t[:, h * RET_QK_DIM + half:(h + 1) * RET_QK_DIM]
        return jnp.concatenate([t1 * cos - t2 * sin, t1 * sin + t2 * cos], axis=1)

    y_parts = []
    for h in range(n_heads):
        if n_heads == RET_HEADS:
            chunk_dec = math.exp(steps * _ret_log_decay(h))
        else:
            head = hb * n_heads + h
            chunk_dec = jnp.float32(math.exp(steps * _ret_log_decay(0)))
            for hh in range(1, RET_HEADS):
                chunk_dec = jnp.where(head == hh, jnp.float32(math.exp(steps * _ret_log_decay(hh))), chunk_dec)
        dmat = dmat_ref[h]
        q_dec = jnp.concatenate([qdec_ref[h]] * 2, axis=1)
        k_dec = jnp.concatenate([kdec_ref[h]] * 2, axis=1)
        qr = rot(qq, h)
        kr = rot(kk, h) * (RET_QK_DIM ** -0.5)
        vh = vv[:, h * RET_V_DIM:(h + 1) * RET_V_DIM]
        scores = _dot_nt(qr, kr) * dmat
        y = _dot(scores, vh)
        qd = qr * q_dec
        kd = kr * k_dec
        r0 = h * RET_QK_DIM
        for b in range(s_blk):
            qd_b = (qd if s_blk == 1 else jnp.where(seq_qk == b, qd, 0.0)).astype(BF16)
            kd_b = (kd if s_blk == 1 else jnp.where(seq_qk == b, kd, 0.0)).astype(BF16)
            sh = s_src[b, r0:r0 + RET_QK_DIM, :]
            y = y + _dot(qd_b, sh)
            sout_ref[b, r0:r0 + RET_QK_DIM, :] = sh * chunk_dec + _dot_tn(kd_b, vh)
        g = gn_ref[:, h * RET_V_DIM:(h + 1) * RET_V_DIM]
        y = _rms(y, g, NORM_EPS) * _silu(rg[:, h * RET_V_DIM:(h + 1) * RET_V_DIM])
        y_parts.append(y)
    y = y_parts[0] if n_heads == 1 else jnp.concatenate(y_parts, axis=1)
    y_ref[...] = y.reshape(y_ref.shape).astype(y_ref.dtype)


def _ret_decay_tables(q, s_blk):
    t = np.arange(q) // s_blk
    seq = np.arange(q) % s_blk
    steps = q // s_blk
    rel = t[:, None] - t[None, :]
    visible = (rel >= 0) & (seq[:, None] == seq[None, :])
    dmat, qdec, kdec = [], [], []
    for h in range(RET_HEADS):
        lg = _ret_log_decay(h)
        dmat.append(np.where(visible, np.exp(np.maximum(rel, 0) * lg), 0.0))
        qdec.append(np.broadcast_to(np.exp((t + 1.0) * lg)[:, None], (q, V7X_LANES)))
        kdec.append(np.broadcast_to(np.exp((steps - 1.0 - t) * lg)[:, None], (q, V7X_LANES)))
    return tuple(jnp.asarray(np.stack(a), F32) for a in (dmat, qdec, kdec))


def _ret(h, gmix, wq, wk, wv, wg, cos, sin, s0, gn, *, nblk, nc, q, s_blk, n_heads, time_major):
    dmat, qdec, kdec = _ret_decay_tables(q, s_blk)
    nhb = RET_HEADS // n_heads
    qk_w = n_heads * RET_QK_DIM
    v_w = n_heads * RET_V_DIM
    per_block0 = s0.shape[0] == nblk * s_blk
    assert per_block0 or s0.shape[0] == s_blk
    rows = functools.partial(_rows_spec, q, s_blk, nblk, nc, D_MODEL, time_major, lead_axes=1)

    def head_cols(nrows, width):
        if nhb == 1:
            return _const_spec((nrows, width))
        return pl.BlockSpec((nrows, width), lambda hb, s, c: (0, hb))

    if time_major:
        y_spec = pl.BlockSpec((q // s_blk, s_blk, v_w), lambda hb, s, c: (0, s, hb))
        y_shape = (q // s_blk, nblk * s_blk, RET_V)
    else:
        y_spec = pl.BlockSpec((q, v_w), lambda hb, s, c: (s * nc + c, hb))
        y_shape = (nblk * nc * q, RET_V)
    return pl.pallas_call(
        functools.partial(_ret_kernel, s_blk=s_blk, nc=nc, n_heads=n_heads),
        grid=(nhb, nblk, nc),
        in_specs=[
            rows(),
            rows(ahead=1),
            _const_spec((1, D_MODEL)),
            head_cols(D_MODEL, qk_w),
            head_cols(D_MODEL, qk_w),
            head_cols(D_MODEL, v_w),
            head_cols(D_MODEL, v_w),
            pl.BlockSpec((q, V7X_LANES), lambda hb, s, c: (c, 0)),
            pl.BlockSpec((q, V7X_LANES), lambda hb, s, c: (c, 0)),
            pl.BlockSpec((n_heads, q, q), lambda hb, s, c: (hb, 0, 0)),
            pl.BlockSpec((n_heads, q, V7X_LANES), lambda hb, s, c: (hb, 0, 0)),
            pl.BlockSpec((n_heads, q, V7X_LANES), lambda hb, s, c: (hb, 0, 0)),
            pl.BlockSpec((s_blk, qk_w, RET_V_DIM), lambda hb, s, c: (s if per_block0 else 0, hb, 0)),
            head_cols(1, v_w),
        ],
        out_specs=[
            y_spec,
            pl.BlockSpec((s_blk, qk_w, RET_V_DIM), lambda hb, s, c: (s, hb, 0)),
        ],
        out_shape=[
            jax.ShapeDtypeStruct(y_shape, BF16),
            jax.ShapeDtypeStruct((nblk * s_blk, RET_HEADS * RET_QK_DIM, RET_V_DIM), F32),
        ],
        scratch_shapes=[
            pltpu.VMEM((q, qk_w), F32),
            pltpu.VMEM((q, qk_w), F32),
            pltpu.VMEM((q, v_w), BF16),
            pltpu.VMEM((q, v_w), F32),
        ] * 2,
        compiler_params=_params(3),
        name="ret",
    )(h, h, gmix, wq, wk, wv, wg, cos, sin, dmat, qdec, kdec, s0, gn)


def _merge_kernel(h_ref, ys_ref, yr_ref, g_ref, wg_ref, wbs_ref, wbr_ref, wo_ref, o_ref):
    h = h_ref[...]
    u = _rms(h, g_ref[...], NORM_EPS).astype(BF16)
    gates = jnp.dot(u, wg_ref[...], preferred_element_type=F32)
    branch_ssm = jnp.dot(ys_ref[...], wbs_ref[...], preferred_element_type=F32)
    branch_ret = jnp.dot(yr_ref[...], wbr_ref[...], preferred_element_type=F32)
    merged = _sigmoid(gates[:, 0:D_MODEL]) * branch_ssm + _sigmoid(gates[:, D_MODEL:2 * D_MODEL]) * branch_ret
    o_ref[...] = h + jnp.dot(merged.astype(BF16), wo_ref[...], preferred_element_type=F32)


def _merge(h, ys, yr, g, wg, wbs, wbr, wo, *, tm):
    t = h.shape[0]
    assert t % tm == 0
    return pl.pallas_call(
        _merge_kernel,
        grid=(t // tm,),
        in_specs=[
            pl.BlockSpec((tm, D_MODEL), lambda i: (i, 0)),
            pl.BlockSpec((tm, SSM_D_INNER), lambda i: (i, 0)),
            pl.BlockSpec((tm, RET_V), lambda i: (i, 0)),
            _const_spec((1, D_MODEL)),
            _const_spec((D_MODEL, 2 * D_MODEL)),
            _const_spec((SSM_D_INNER, D_MODEL)),
            _const_spec((RET_V, D_MODEL)),
            _const_spec((D_MODEL, D_MODEL)),
        ],
        out_specs=pl.BlockSpec((tm, D_MODEL), lambda i: (i, 0)),
        out_shape=jax.ShapeDtypeStruct((t, D_MODEL), F32),
        compiler_params=_params(1),
        name="merge",
    )(h, ys, yr, g, wg, wbs, wbr, wo)


FFN_TILE = 512
SAMPLE_SEQS_PER_STEP = V7X_SUBLANES
SAMPLE_RET_HEADS_PER_STEP = 2


def _layer(x2, pos0, ssm0, conv0, ret0, w, final_gain, *, nblk, nc, q, s_blk, ret_heads, time_major):
    t = x2.shape[0]
    tm_ffn = min(FFN_TILE, t)
    steps = q // s_blk
    nseq = nblk * s_blk

    h = _ffn(x2, w["norm_ffn1"], w["ffn1_w1"], w["ffn1_w3"], w["ffn1_w2"], final_gain, final_norm=False, tm=tm_ffn)
    h_rows = h.reshape(steps, nseq, D_MODEL) if time_major else h

    y_ssm, ssm_new, conv_new = _ssd(
        h_rows, w["norm_mix"], w["w_ssd"], ssm0, conv0, w["conv_w"], w["conv_b"], w["dt_bias"], w["a_log"],
        w["d_exp"], w["ssm_norm"], w["head_expand"], nblk=nblk, nc=nc, q=q, s_blk=s_blk, time_major=time_major)
    cos, sin = _rope_tables(w["rope_inv"], nc * q, pos0, s_blk)
    y_ret, ret_new = _ret(h_rows, w["norm_mix"], w["w_q"], w["w_k"], w["w_v"], w["w_rg"], cos, sin, ret0,
                          w["ret_norm"], nblk=nblk, nc=nc, q=q, s_blk=s_blk, n_heads=ret_heads,
                          time_major=time_major)

    h = _merge(h, y_ssm.reshape(t, SSM_D_INNER), y_ret.reshape(t, RET_V), w["norm_mix"], w["w_gates"],
               w["w_branch_ssm"], w["w_branch_ret"], w["w_out"], tm=tm_ffn)
    y = _ffn(h, w["norm_ffn2"], w["ffn2_w1"], w["ffn2_w3"], w["ffn2_w2"], final_gain, final_norm=True, tm=tm_ffn)
    return y, ssm_new, conv_new, ret_new


def _prompt_layer(x, w, final_gain):
    nseq, seq_len, _ = x.shape
    q = math.gcd(seq_len, CHUNK)
    y, ssm_new, conv_new, ret_new = _layer(
        x.reshape(nseq * seq_len, D_MODEL), 0.0,
        jnp.zeros((1, SSM_HEADS * SSM_HEAD_DIM, SSM_STATE), F32),
        jnp.zeros((1, SSM_CONV - 1, SSM_CONV_DIM), F32),
        jnp.zeros((1, RET_HEADS * RET_QK_DIM, RET_V_DIM), F32),
        w, final_gain, nblk=nseq, nc=seq_len // q, q=q, s_blk=1, ret_heads=RET_HEADS, time_major=False)
    return (y.reshape(nseq, seq_len, D_MODEL), ssm_new.reshape(nseq, SSM_HEADS, SSM_HEAD_DIM, SSM_STATE), conv_new,
            ret_new.reshape(nseq, RET_HEADS, RET_QK_DIM, RET_V_DIM))


def _sample_layer(x, ssm0, conv0, ret0, w, final_gain):
    nseq, seq_len, _ = x.shape
    s_blk = SAMPLE_SEQS_PER_STEP
    assert nseq % s_blk == 0 and seq_len <= CHUNK
    nblk = nseq // s_blk
    kc = SSM_CONV - 1
    x_tm = jnp.transpose(x, (1, 0, 2)).reshape(seq_len * nseq, D_MODEL)
    conv0_tm = jnp.transpose(conv0.reshape(nblk, s_blk, kc, SSM_CONV_DIM), (0, 2, 1, 3)).reshape(
        nblk, kc * s_blk, SSM_CONV_DIM)
    y, ssm_new, conv_new, ret_new = _layer(
        x_tm, float(PAST_LEN), ssm0.reshape(nseq, SSM_HEADS * SSM_HEAD_DIM, SSM_STATE), conv0_tm,
        ret0.reshape(nseq, RET_HEADS * RET_QK_DIM, RET_V_DIM), w, final_gain,
        nblk=nblk, nc=1, q=s_blk * seq_len, s_blk=s_blk, ret_heads=SAMPLE_RET_HEADS_PER_STEP, time_major=True)
    conv_new = jnp.transpose(conv_new.reshape(nblk, kc, s_blk, SSM_CONV_DIM), (0, 2, 1, 3)).reshape(
        nseq, kc, SSM_CONV_DIM)
    return (jnp.transpose(y.reshape(seq_len, nseq, D_MODEL), (1, 0, 2)),
            ssm_new.reshape(nseq, SSM_HEADS, SSM_HEAD_DIM, SSM_STATE), conv_new,
            ret_new.reshape(nseq, RET_HEADS, RET_QK_DIM, RET_V_DIM))


def _prep_weights(norm_ffn1, ffn1_w1, ffn1_w3, ffn1_w2, norm_mix, w_in, conv_w, conv_b, dt_bias, a_log, ssm_d,
                  ssm_norm, ret_norm, w_branch_ssm, w_branch_ret, w_out, norm_ffn2, ffn2_w1, ffn2_w3, ffn2_w2):
    o_z = 0
    o_xbc = o_z + SSM_D_INNER
    o_dt = o_xbc + SSM_CONV_DIM
    o_q = o_dt + SSM_HEADS
    o_k = o_q + RET_QK
    o_v = o_k + RET_QK
    o_rg = o_v + RET_V
    o_ga = o_rg + RET_V
    o_end = o_ga + 2 * D_MODEL
    assert o_end == w_in.shape[1]

    w_in_bf = w_in.astype(BF16)

    def cols(a, b):
        return w_in_bf[:, a:b]

    assert (o_z, o_xbc, o_dt) == (SSDP_Z, SSDP_XBC, SSDP_DT) and SSDP_DIM <= o_end
    half = RET_QK_DIM // 2
    head_of_channel = jnp.arange(SSM_D_INNER, dtype=jnp.int32) // SSM_HEAD_DIM
    return {
        "norm_ffn1": norm_ffn1.reshape(1, -1), "ffn1_w1": ffn1_w1.astype(BF16), "ffn1_w3": ffn1_w3.astype(BF16),
        "ffn1_w2": ffn1_w2.astype(BF16),
        "norm_mix": norm_mix.reshape(1, -1), "w_ssd": w_in_bf,
        "w_q": cols(o_q, o_k), "w_k": cols(o_k, o_v), "w_v": cols(o_v, o_rg), "w_rg": cols(o_rg, o_ga),
        "w_gates": cols(o_ga, o_end),
        "conv_w": conv_w, "conv_b": conv_b.reshape(1, -1),
        "dt_bias": jnp.pad(dt_bias, (0, DT_PAD - SSM_HEADS)).reshape(1, -1),
        "a_log": jnp.pad(a_log, (0, DT_PAD - SSM_HEADS)).reshape(1, -1),
        "d_exp": jnp.repeat(ssm_d, SSM_HEAD_DIM).reshape(1, -1),
        "ssm_norm": ssm_norm.reshape(1, -1), "ret_norm": ret_norm.reshape(1, -1),
        "head_expand": (head_of_channel[None, :] == jnp.arange(DT_PAD, dtype=jnp.int32)[:, None]).astype(BF16),
        "rope_inv": (ROPE_BASE ** (-jnp.arange(half, dtype=F32) / half)).reshape(1, half),
        "w_branch_ssm": w_branch_ssm.astype(BF16), "w_branch_ret": w_branch_ret.astype(BF16),
        "w_out": w_out.astype(BF16),
        "norm_ffn2": norm_ffn2.reshape(1, -1), "ffn2_w1": ffn2_w1.astype(BF16), "ffn2_w3": ffn2_w3.astype(BF16),
        "ffn2_w2": ffn2_w2.astype(BF16),
    }


def kernel(x_prompt, x_sample, state_ssm, state_conv, state_ret, norm_ffn1, ffn1_w1, ffn1_w3, ffn1_w2, norm_mix, w_in,
           conv_w, conv_b, dt_bias, a_log, ssm_d, ssm_norm, ret_norm, w_branch_ssm, w_branch_ret, w_out, norm_ffn2,
           ffn2_w1, ffn2_w3, ffn2_w2, norm_final):
    depth = norm_ffn1.shape[0]
    assert depth == 1, "the final RMSNorm is fused into the last layer's second FFN"
    w = _prep_weights(norm_ffn1[0], ffn1_w1[0], ffn1_w3[0], ffn1_w2[0], norm_mix[0], w_in[0], conv_w[0], conv_b[0],
                      dt_bias[0], a_log[0], ssm_d[0], ssm_norm[0], ret_norm[0], w_branch_ssm[0], w_branch_ret[0],
                      w_out[0], norm_ffn2[0], ffn2_w1[0], ffn2_w3[0], ffn2_w2[0])
    final_gain = norm_final.reshape(1, -1)
    yp, ssm_p, conv_p, ret_p = _prompt_layer(x_prompt, w, final_gain)
    ys, ssm_s, conv_s, ret_s = _sample_layer(x_sample, state_ssm[0], state_conv[0], state_ret[0], w, final_gain)
    return (yp, ys, ssm_p[None], conv_p[None], ret_p[None], ssm_s[None], conv_s[None], ret_s[None])
```

```python
import functools
import math

import jax
import jax.numpy as jnp
import numpy as np
from jax import lax
from jax.experimental import pallas as pl
from jax.experimental.pallas import tpu as pltpu

F32 = jnp.float32
BF16 = jnp.bfloat16

D_MODEL = 1024
D_FF = 2816
SSM_D_INNER = 2048
SSM_HEAD_DIM = 64
SSM_HEADS = 32
SSM_GROUPS = 4
SSM_STATE = 128
SSM_CONV = 4
SSM_BC = 2 * SSM_GROUPS * SSM_STATE
SSM_CONV_DIM = SSM_D_INNER + SSM_BC
RET_HEADS = 4
RET_QK_DIM = 256
RET_V_DIM = 512
RET_QK = RET_HEADS * RET_QK_DIM
RET_V = RET_HEADS * RET_V_DIM
ROPE_BASE = 10000.0
PAST_LEN = 16384
CHUNK = 128
NORM_EPS = 1e-6
GATED_NORM_EPS = 1e-5

V7X_SUBLANES = 8
V7X_LANES = 128
V7X_VMEM_BYTES = 64 * 1024 * 1024
VMEM_LIMIT = V7X_VMEM_BYTES - 8 * 1024 * 1024

DT_PAD = V7X_LANES
LOG_DT_FLOOR = -1e30
LOG2_E = math.log2(math.e)

SSDP_Z = 0
SSDP_XBC = SSM_D_INNER
SSDP_DT = SSM_D_INNER + SSM_CONV_DIM
SSDP_DIM = SSDP_DT + DT_PAD


def _rms(x, g, eps):
    return x * lax.rsqrt(jnp.mean(x * x, axis=-1, keepdims=True) + eps) * g


def _sigmoid(x):
    return 0.5 + 0.5 * jnp.tanh(0.5 * x)


def _silu(x):
    half = 0.5 * x
    return half + half * jnp.tanh(half)


def _softplus(x):
    return jnp.maximum(x, 0.0) + jnp.log1p(jnp.exp(-jnp.abs(x)))


def _dot(a, b):
    return jnp.dot(a.astype(BF16), b.astype(BF16), preferred_element_type=F32)


def _dot_nt(a, b):
    return lax.dot_general(a.astype(BF16), b.astype(BF16), (((1,), (1,)), ((), ())), preferred_element_type=F32)


def _dot_tn(a, b):
    return lax.dot_general(a.astype(BF16), b.astype(BF16), (((0,), (0,)), ((), ())), preferred_element_type=F32)


def _dot_split(a, b_f32, passes):
    acc = None
    rem = b_f32
    for _ in range(passes):
        piece = rem.astype(BF16)
        term = jnp.dot(a, piece, preferred_element_type=F32)
        acc = term if acc is None else acc + term
        rem = rem - piece.astype(F32)
    return acc


def _dot_split_lhs(a_f32, b, passes):
    acc = None
    rem = a_f32
    for _ in range(passes):
        piece = rem.astype(BF16)
        term = jnp.dot(piece, b, preferred_element_type=F32)
        acc = term if acc is None else acc + term
        rem = rem - piece.astype(F32)
    return acc


def _transpose_split(eye, x_f32, passes):
    acc = None
    rem = x_f32
    for _ in range(passes):
        piece = rem.astype(BF16)
        term = lax.dot_general(eye, piece, (((1,), (1,)), ((), ())), preferred_element_type=F32)
        acc = term if acc is None else acc + term
        rem = rem - piece.astype(F32)
    return acc


def _const_spec(shape):
    nd = len(shape)
    return pl.BlockSpec(shape, lambda *_: (0,) * nd, pipeline_mode=pl.Buffered(1))


def _params(n_grid_dims, flags=None):
    return pltpu.CompilerParams(dimension_semantics=("arbitrary",) * n_grid_dims, vmem_limit_bytes=VMEM_LIMIT,
                                flags=flags)


def _ffn_kernel(x_ref, g_ref, w1_ref, w3_ref, w2_ref, gf_ref, o_ref, *, final_norm):
    x = x_ref[...]
    xn = _rms(x, g_ref[...], NORM_EPS).astype(BF16)
    a = jnp.dot(xn, w1_ref[...], preferred_element_type=F32)
    b = jnp.dot(xn, w3_ref[...], preferred_element_type=F32)
    gated = (_silu(a) * b).astype(BF16)
    h = x + 0.5 * jnp.dot(gated, w2_ref[...], preferred_element_type=F32)
    if final_norm:
        h = _rms(h, gf_ref[...], NORM_EPS)
    o_ref[...] = h


def _ffn(x, g, w1, w3, w2, gf, *, final_norm, tm):
    t = x.shape[0]
    assert t % tm == 0
    return pl.pallas_call(
        functools.partial(_ffn_kernel, final_norm=final_norm),
        grid=(t // tm,),
        in_specs=[
            pl.BlockSpec((tm, D_MODEL), lambda i: (i, 0)),
            _const_spec((1, D_MODEL)),
            _const_spec((D_MODEL, D_FF)),
            _const_spec((D_MODEL, D_FF)),
            _const_spec((D_FF, D_MODEL)),
            _const_spec((1, D_MODEL)),
        ],
        out_specs=pl.BlockSpec((tm, D_MODEL), lambda i: (i, 0)),
        out_shape=jax.ShapeDtypeStruct((t, D_MODEL), F32),
        compiler_params=_params(1),
        name="ffn_final" if final_norm else "ffn",
    )(x, g, w1, w3, w2, gf)


def _rope_kernel(inv_ref, cos_ref, sin_ref, *, pos0, rows_per_pos):
    rows = cos_ref.shape[0]
    step = lax.broadcasted_iota(jnp.int32, (rows, V7X_LANES), 0) // rows_per_pos
    ang = (pos0 + step.astype(F32)) * inv_ref[...]
    cos_ref[...] = jnp.cos(ang)
    sin_ref[...] = jnp.sin(ang)


def _rope_tables(inv, rows, pos0, rows_per_pos):
    return pl.pallas_call(
        functools.partial(_rope_kernel, pos0=pos0, rows_per_pos=rows_per_pos),
        out_shape=[jax.ShapeDtypeStruct((rows, V7X_LANES), F32)] * 2,
        name="rope",
    )(inv)


def _chunk_rows(ref):
    return ref[...].reshape(-1, ref.shape[-1])


def _row_time_seq(shape, s_blk):
    r = lax.broadcasted_iota(jnp.int32, shape, 0)
    return r // s_blk, r % s_blk


def _pair_mask(q, s_blk):
    r = lax.broadcasted_iota(jnp.int32, (q, q), 0)
    c = lax.broadcasted_iota(jnp.int32, (q, q), 1)
    if s_blk == 1:
        return r >= c
    return jnp.logical_and(r // s_blk >= c // s_blk, r % s_blk == c % s_blk)


def _rows_spec(q, s_blk, nblk, nc, width, time_major, *, ahead=0, lead_axes=0, chunks_per_step=1):
    last = nblk * nc - 1

    def chunk_index(args):
        s, c = args[lead_axes], args[lead_axes + 1]
        return jnp.minimum((s * (nc // chunks_per_step) + c) * chunks_per_step + ahead, last)

    if time_major:
        assert nc == 1
        return pl.BlockSpec((q // s_blk, s_blk, width), lambda *a: (0, chunk_index(a), 0))
    return pl.BlockSpec((q, width), lambda *a: (chunk_index(a), 0))


def _project(h_ref, g_ref, w_refs):
    u = _rms(_chunk_rows(h_ref), g_ref[...], NORM_EPS).astype(BF16)
    return [jnp.dot(u, w_ref[...], preferred_element_type=F32) for w_ref in w_refs]


def _conv_pad_rows(s_blk):
    need = (SSM_CONV - 1) * s_blk
    return -(-need // V7X_SUBLANES) * V7X_SUBLANES


def _ssd_kernel(hc_ref, hn1_ref, hn2_ref, *refs, s_blk, nc, chunks_per_step):
    n_in = 11
    y_ref = refs[n_in]
    for k, hn_ref in enumerate((hn1_ref, hn2_ref)[:chunks_per_step]):
        if chunks_per_step == 1:
            y_view = y_ref
        else:
            q = y_ref.shape[0] // chunks_per_step
            y_view = y_ref.at[pl.ds(k * q, q)]
        _ssd_chunk(hc_ref, hn_ref, *refs[:n_in], y_view, *refs[n_in + 1:], s_blk=s_blk, nc=nc,
                   chunks_per_step=chunks_per_step, first_of_step=(k == 0))


def _ssd_chunk(hc_ref, hn_ref, gmix_ref, w_ref, h0_ref, conv0_ref, cw_ref, cb_ref, dtb_ref, alog_ref, dexp_ref,
               gn_ref, e_ref, y_ref, hout_ref, convout_ref, xp_scr, zdt_scr, next_scr, *, s_blk, nc, chunks_per_step,
               first_of_step):
    s = pl.program_id(0)
    c = pl.program_id(1)
    pad = _conv_pad_rows(s_blk)
    q = xp_scr.shape[0] - pad
    carry = (SSM_CONV - 1) * s_blk
    steps = q // s_blk
    hg = SSM_HEADS // SSM_GROUPS
    gw = hg * SSM_HEAD_DIM
    h_src = h0_ref if nc == 1 else hout_ref

    if first_of_step:
        @pl.when(jnp.logical_and(s == 0, c == 0))
        def _first():
            next_scr[...] = _project(hc_ref, gmix_ref, [w_ref])[0]

        @pl.when(c == 0)
        def _init():
            xp_scr[0:pad, :] = jnp.zeros((pad, SSM_CONV_DIM), F32)
            xp_scr[pad - carry:pad, :] = conv0_ref[0]
            if nc > 1:
                hout_ref[...] = h0_ref[...]

    xp_scr[pad:pad + q, :] = next_scr[:, SSDP_XBC:SSDP_DT]
    zdt_scr[:, 0:SSM_D_INNER] = next_scr[:, SSDP_Z:SSDP_XBC]
    zdt_scr[:, SSM_D_INNER:SSM_D_INNER + DT_PAD] = next_scr[:, SSDP_DT:SSDP_DIM]
    next_scr[...] = _project(hn_ref, gmix_ref, [w_ref])[0]

    z_gate = _silu(zdt_scr[:, 0:SSM_D_INNER])

    xall = xp_scr[...]
    conv = cb_ref[...] + xall[pad:pad + q, :] * cw_ref[SSM_CONV - 1:SSM_CONV, :]
    for j in range(1, SSM_CONV):
        d = j * s_blk
        if d % V7X_SUBLANES == 0:
            back = xall[pad - d:pad - d + q, :]
        else:
            back = pltpu.roll(xall, d, 0)[pad:pad + q, :]
        conv = conv + back * cw_ref[SSM_CONV - 1 - j:SSM_CONV - j, :]
    @pl.when(c == nc // chunks_per_step - 1)
    def _():
        convout_ref[0] = xp_scr[pad + q - carry:pad + q, :]

    if nc > 1:
        xp_scr[0:pad, :] = xp_scr[q:q + pad, :]
    dt_rows = zdt_scr[:, SSM_D_INNER:SSM_D_INNER + DT_PAD]
    xc = _silu(conv)
    xs = xc[:, 0:SSM_D_INNER]
    bm = xc[:, SSM_D_INNER:SSM_D_INNER + SSM_GROUPS * SSM_STATE]
    cm = xc[:, SSM_D_INNER + SSM_GROUPS * SSM_STATE:SSM_CONV_DIM]

    dt = _softplus(dt_rows + dtb_ref[...])
    a = -jnp.exp(alog_ref[...])
    visible = _pair_mask(q, s_blk)
    la = _dot_split(jnp.where(visible, 1.0, 0.0).astype(BF16), dt * (a * LOG2_E), 3)
    lsd = la - jnp.maximum(jnp.log2(dt), LOG_DT_FLOOR)
    eye_r = lax.broadcasted_iota(jnp.int32, (DT_PAD, DT_PAD), 0)
    eye_c = lax.broadcasted_iota(jnp.int32, (DT_PAD, DT_PAD), 1)
    eye = jnp.where(eye_r == eye_c, 1.0, 0.0).astype(BF16)
    la_t = _transpose_split(eye, la, 3)
    lsd_t = _transpose_split(eye, lsd, 3)
    la_last = la[q - s_blk:q, :]
    la_last_rows = la_last if s_blk == 1 else jnp.concatenate([la_last] * steps, axis=0)
    dec_t = jnp.exp2(la_t[:, q - s_blk:q])

    lane = lax.broadcasted_iota(jnp.int32, (q, 2 * SSM_HEAD_DIM), 1)
    first_head = lane < SSM_HEAD_DIM
    y_parts = []
    for g in range(SSM_GROUPS):
        cg = cm[:, g * SSM_STATE:(g + 1) * SSM_STATE]
        bg = bm[:, g * SSM_STATE:(g + 1) * SSM_STATE]
        cbg = _dot_nt(cg, bg)
        for pair in range(hg // 2):
            h0 = g * hg + 2 * pair
            ws = []
            for h in (h0, h0 + 1):
                seg = la[:, h:h + 1] - lsd_t[h:h + 1, :]
                ws.append(cbg * jnp.exp2(jnp.where(visible, seg, -jnp.inf)))
            xpair = xs[:, h0 * SSM_HEAD_DIM:(h0 + 2) * SSM_HEAD_DIM]
            rhs = jnp.concatenate([jnp.where(first_head, xpair, 0.0), jnp.where(first_head, 0.0, xpair)], axis=0)
            y_parts.append(_dot(jnp.concatenate(ws, axis=1), rhs))
    y = jnp.concatenate(y_parts, axis=1)

    _, seq_b = _row_time_seq((q, SSM_STATE), s_blk)
    y_state_parts = []
    for g in range(SSM_GROUPS):
        cg = cm[:, g * SSM_STATE:(g + 1) * SSM_STATE]
        if s_blk == 1:
            c_seqs, h_seqs = cg, h_src[0, g * gw:(g + 1) * gw, :]
        else:
            c_seqs = jnp.concatenate([jnp.where(seq_b == b, cg, 0.0) for b in range(s_blk)], axis=1)
            h_seqs = jnp.concatenate([h_src[b, g * gw:(g + 1) * gw, :].astype(BF16) for b in range(s_blk)], axis=1)
        y_state_parts.append(_dot_nt(c_seqs, h_seqs))
    y_state = jnp.concatenate(y_state_parts, axis=1)

    e = e_ref[...]
    y = y + y_state * _dot_split_lhs(jnp.exp2(la), e, 2) + xs * dexp_ref[...]
    y = y * z_gate
    y = _rms(y, gn_ref[...], GATED_NORM_EPS)
    y_ref[...] = y.reshape(y_ref.shape).astype(y_ref.dtype)

    tail = jnp.exp2(la_last_rows - lsd)
    xt = (xs * _dot_split_lhs(tail, e, 2)).astype(BF16)
    dec =[jnp.broadcast_to(dec_t[:, b:b + 1], (DT_PAD, SSM_STATE)) for b in range(s_blk)]
    for g in range(SSM_GROUPS):
        bg = bm[:, g * SSM_STATE:(g + 1) * SSM_STATE]
        bg_seqs = bg if s_blk == 1 else jnp.concatenate([jnp.where(seq_b == b, bg, 0.0) for b in range(s_blk)], axis=1)
        upd = _dot_tn(xt[:, g * gw:(g + 1) * gw], bg_seqs)
        for b in range(s_blk):
            for hh in range(hg):
                h = g * hg + hh
                r0 = h * SSM_HEAD_DIM
                hout_ref[b, r0:r0 + SSM_HEAD_DIM, :] = (
                    h_src[b, r0:r0 + SSM_HEAD_DIM, :] * dec[b][h:h + 1, :]
                    + upd[hh * SSM_HEAD_DIM:(hh + 1) * SSM_HEAD_DIM, b * SSM_STATE:(b + 1) * SSM_STATE])


def _ssd(h, gmix, w, h0, conv0, cw, cb, dtb, alog, dexp, gn, e, *, nblk, nc, q, s_blk, time_major):
    rows = SSM_HEADS * SSM_HEAD_DIM
    carry = (SSM_CONV - 1) * s_blk
    per_block0 = h0.shape[0] == nblk * s_blk
    assert per_block0 or (h0.shape[0] == s_blk and conv0.shape[0] == 1)
    cps = SSD_CHUNKS_PER_STEP if (nc % SSD_CHUNKS_PER_STEP == 0 and not time_major) else 1
    ncs = nc // cps
    spec = functools.partial(_rows_spec, q, s_blk, nblk, nc, time_major=time_major, chunks_per_step=cps)
    if time_major:
        y_spec = spec(SSM_D_INNER)
        y_shape = (q // s_blk, nblk * s_blk, SSM_D_INNER)
    else:
        y_spec = pl.BlockSpec((cps * q, SSM_D_INNER), lambda s, c: (s * ncs + c, 0))
        y_shape = (nblk * nc * q, SSM_D_INNER)
    return pl.pallas_call(
        functools.partial(_ssd_kernel, s_blk=s_blk, nc=nc, chunks_per_step=cps),
        grid=(nblk, ncs),
        in_specs=[
            spec(D_MODEL),
            spec(D_MODEL, ahead=1),
            spec(D_MODEL, ahead=2),
            _const_spec((1, D_MODEL)),
            _const_spec((D_MODEL, SSDP_DIM)),
            pl.BlockSpec((s_blk, rows, SSM_STATE), lambda s, c: (s if per_block0 else 0, 0, 0)),
            pl.BlockSpec((1, carry, SSM_CONV_DIM), lambda s, c: (s if per_block0 else 0, 0, 0)),
            _const_spec((SSM_CONV, SSM_CONV_DIM)),
            _const_spec((1, SSM_CONV_DIM)),
            _const_spec((1, DT_PAD)),
            _const_spec((1, DT_PAD)),
            _const_spec((1, SSM_D_INNER)),
            _const_spec((1, SSM_D_INNER)),
            _const_spec((DT_PAD, SSM_D_INNER)),
        ],
        out_specs=[
            y_spec,
            pl.BlockSpec((s_blk, rows, SSM_STATE), lambda s, c: (s, 0, 0)),
            pl.BlockSpec((1, carry, SSM_CONV_DIM), lambda s, c: (s, 0, 0)),
        ],
        out_shape=[
            jax.ShapeDtypeStruct(y_shape, BF16),
            jax.ShapeDtypeStruct((nblk * s_blk, rows, SSM_STATE), F32),
            jax.ShapeDtypeStruct((nblk, carry, SSM_CONV_DIM), F32),
        ],
        scratch_shapes=[
            pltpu.VMEM((_conv_pad_rows(s_blk) + q, SSM_CONV_DIM), F32),
            pltpu.VMEM((q, SSM_D_INNER + DT_PAD), F32),
            pltpu.VMEM((q, SSDP_DIM), F32),
        ],
        compiler_params=_params(2),
        name="ssd",
    )(h, h, h, gmix, w, h0, conv0, cw, cb, dtb, alog, dexp, gn, e)


def _ret_log_decay(h):
    return float(np.log1p(-np.exp2(-5.0 - h)))


def _ret_project(h_ref, gmix_ref, w_refs, dst):
    pq, pk, pv, pg = _project(h_ref, gmix_ref, w_refs)
    dst[0][...] = pq
    dst[1][...] = pk
    dst[2][...] = pv.astype(BF16)
    dst[3][...] = pg


def _ret_kernel(*refs, s_blk, nc, n_heads, chunks_per_step):
    io_refs, set0, set1 = refs[:-8], refs[-8:-4], refs[-4:]
    hc_ref, hn1_ref, hn2_ref, gmix_ref = io_refs[0:4]
    w_refs = io_refs[4:8]
    cos_ref, sin_ref = io_refs[8:10]
    tables = io_refs[10:13]
    s0_ref, gn_ref, y_ref, sout_ref = io_refs[13:17]
    s = pl.program_id(1)
    c = pl.program_id(2)
    q = cos_ref.shape[0] // chunks_per_step

    @pl.when(jnp.logical_and(s == 0, c == 0))
    def _first():
        _ret_project(hc_ref, gmix_ref, w_refs, set0)

    if nc > 1:
        @pl.when(c == 0)
        def _init():
            sout_ref[...] = s0_ref[...]

    def chunk(k, hn_ref, cur_set, next_set):
        rows = pl.ds(k * q, q)
        y_view = y_ref if chunks_per_step == 1 else y_ref.at[rows]
        _ret_step(hn_ref, gmix_ref, *w_refs, cos_ref.at[rows], sin_ref.at[rows], *tables, s0_ref, gn_ref, y_view,
                  sout_ref, cur_set, next_set, s_blk=s_blk, nc=nc, n_heads=n_heads)

    if chunks_per_step == 2:
        chunk(0, hn1_ref, set0, set1)
        chunk(1, hn2_ref, set1, set0)
    else:
        parity = (s * nc + c) % 2

        @pl.when(parity == 0)
        def _even():
            chunk(0, hn1_ref, set0, set1)

        @pl.when(parity == 1)
        def _odd():
            chunk(0, hn1_ref, set1, set0)


def _ret_step(hn_ref, gmix_ref, wq_ref, wk_ref, wv_ref, wg_ref, cos_ref, sin_ref, dmat_ref, qdec_ref, kdec_ref,
              s0_ref, gn_ref, y_ref, sout_ref, cur_set, next_set, *, s_blk, nc, n_heads):
    q_scr, k_scr, v_scr, g_scr = cur_set
    hb = pl.program_id(0)
    q = cos_ref.shape[0]
    steps = q // s_blk
    half = RET_QK_DIM // 2
    s_src = s0_ref if nc == 1 else sout_ref

    _ret_project(hn_ref, gmix_ref, [wq_ref, wk_ref, wv_ref, wg_ref], next_set)

    qq = q_scr[...]
    kk = k_scr[...]
    vv = v_scr[...]
    rg = g_scr[...]
    cos = cos_ref[...]
    sin = sin_ref[...]
    _, seq_qk = _row_time_seq((q, RET_QK_DIM), s_blk)

    def rot(t, h):
        t1 = t[:, h * RET_QK_DIM:h * RET_QK_DIM + half]
        t2 = t[:, h * RET_QK_DIM + half:(h + 1) * RET_QK_DIM]
        return jnp.concatenate([t1 * cos - t2 * sin, t1 * sin + t2 * cos], axis=1)

    y_parts = []
    for h in range(n_heads):
        if n_heads == RET_HEADS:
            chunk_dec = math.exp(steps * _ret_log_decay(h))
        else:
            head = hb * n_heads + h
            chunk_dec = jnp.float32(math.exp(steps * _ret_log_decay(0)))
            for hh in range(1, RET_HEADS):
                chunk_dec = jnp.where(head == hh, jnp.float32(math.exp(steps * _ret_log_decay(hh))), chunk_dec)
        dmat = dmat_ref[h]
        q_dec = jnp.concatenate([qdec_ref[h]] * 2, axis=1)
        k_dec = jnp.concatenate([kdec_ref[h]] * 2, axis=1)
        qr = rot(qq, h)
        kr = rot(kk, h) * (RET_QK_DIM ** -0.5)
        vh = vv[:, h * RET_V_DIM:(h + 1) * RET_V_DIM]
        scores = _dot_nt(qr, kr) * dmat
        y = _dot(scores, vh)
        qd = qr * q_dec
        kd = kr * k_dec
        r0 = h * RET_QK_DIM
        for b in range(s_blk):
            qd_b = (qd if s_blk == 1 else jnp.where(seq_qk == b, qd, 0.0)).astype(BF16)
            kd_b = (kd if s_blk == 1 else jnp.where(seq_qk == b, kd, 0.0)).astype(BF16)
            sh = s_src[b, r0:r0 + RET_QK_DIM, :]
            y = y + _dot(qd_b, sh)
            sout_ref[b, r0:r0 + RET_QK_DIM, :] = sh * chunk_dec + _dot_tn(kd_b, vh)
        g = gn_ref[:, h * RET_V_DIM:(h + 1) * RET_V_DIM]
        y = _rms(y, g, NORM_EPS) * _silu(rg[:, h * RET_V_DIM:(h + 1) * RET_V_DIM])
        y_parts.append(y)
    y = y_parts[0] if n_heads == 1 else jnp.concatenate(y_parts, axis=1)
    y_ref[...] = y.reshape(y_ref.shape).astype(y_ref.dtype)


def _ret_decay_tables(q, s_blk):
    t = np.arange(q) // s_blk
    seq = np.arange(q) % s_blk
    steps = q // s_blk
    rel = t[:, None] - t[None, :]
    visible = (rel >= 0) & (seq[:, None] == seq[None, :])
    dmat, qdec, kdec = [], [], []
    for h in range(RET_HEADS):
        lg = _ret_log_decay(h)
        dmat.append(np.where(visible, np.exp(np.maximum(rel, 0) * lg), 0.0))
        qdec.append(np.broadcast_to(np.exp((t + 1.0) * lg)[:, None], (q, V7X_LANES)))
        kdec.append(np.broadcast_to(np.exp((steps - 1.0 - t) * lg)[:, None], (q, V7X_LANES)))
    return tuple(jnp.asarray(np.stack(a), F32) for a in (dmat, qdec, kdec))


def _ret(h, gmix, wq, wk, wv, wg, cos, sin, s0, gn, *, nblk, nc, q, s_blk, n_heads, time_major):
    dmat, qdec, kdec = _ret_decay_tables(q, s_blk)
    nhb = RET_HEADS // n_heads
    qk_w = n_heads * RET_QK_DIM
    v_w = n_heads * RET_V_DIM
    per_block0 = s0.shape[0] == nblk * s_blk
    assert per_block0 or s0.shape[0] == s_blk
    cps = RET_CHUNKS_PER_STEP if (nc % RET_CHUNKS_PER_STEP == 0 and not time_major) else 1
    ncs = nc // cps
    rows = functools.partial(_rows_spec, q, s_blk, nblk, nc, D_MODEL, time_major, lead_axes=1, chunks_per_step=cps)

    def head_cols(nrows, width):
        if nhb == 1:
            return _const_spec((nrows, width))
        return pl.BlockSpec((nrows, width), lambda hb, s, c: (0, hb))

    if time_major:
        y_spec = pl.BlockSpec((q // s_blk, s_blk, v_w), lambda hb, s, c: (0, s, hb))
        y_shape = (q // s_blk, nblk * s_blk, RET_V)
    else:
        y_spec = pl.BlockSpec((cps * q, v_w), lambda hb, s, c: (s * ncs + c, hb))
        y_shape = (nblk * nc * q, RET_V)
    return pl.pallas_call(
        functools.partial(_ret_kernel, s_blk=s_blk, nc=nc, n_heads=n_heads, chunks_per_step=cps),
        grid=(nhb, nblk, ncs),
        in_specs=[
            rows(),
            rows(ahead=1),
            rows(ahead=2),
            _const_spec((1, D_MODEL)),
            head_cols(D_MODEL, qk_w),
            head_cols(D_MODEL, qk_w),
            head_cols(D_MODEL, v_w),
            head_cols(D_MODEL, v_w),
            pl.BlockSpec((cps * q, V7X_LANES), lambda hb, s, c: (c, 0)),
            pl.BlockSpec((cps * q, V7X_LANES), lambda hb, s, c: (c, 0)),
            pl.BlockSpec((n_heads, q, q), lambda hb, s, c: (hb, 0, 0)),
            pl.BlockSpec((n_heads, q, V7X_LANES), lambda hb, s, c: (hb, 0, 0)),
            pl.BlockSpec((n_heads, q, V7X_LANES), lambda hb, s, c: (hb, 0, 0)),
            pl.BlockSpec((s_blk, qk_w, RET_V_DIM), lambda hb, s, c: (s if per_block0 else 0, hb, 0)),
            head_cols(1, v_w),
        ],
        out_specs=[
            y_spec,
            pl.BlockSpec((s_blk, qk_w, RET_V_DIM), lambda hb, s, c: (s, hb, 0)),
        ],
        out_shape=[
            jax.ShapeDtypeStruct(y_shape, BF16),
            jax.ShapeDtypeStruct((nblk * s_blk, RET_HEADS * RET_QK_DIM, RET_V_DIM), F32),
        ],
        scratch_shapes=[
            pltpu.VMEM((q, qk_w), F32),
            pltpu.VMEM((q, qk_w), F32),
            pltpu.VMEM((q, v_w), BF16),
            pltpu.VMEM((q, v_w), F32),
        ] * 2,
        compiler_params=_params(3),
        name="ret",
    )(h, h, h, gmix, wq, wk, wv, wg, cos, sin, dmat, qdec, kdec, s0, gn)


def _merge_kernel(h_ref, ys_ref, yr_ref, g_ref, wg_ref, wbs_ref, wbr_ref, wo_ref, o_ref):
    h = h_ref[...]
    u = _rms(h, g_ref[...], NORM_EPS).astype(BF16)
    gates = jnp.dot(u, wg_ref[...], preferred_element_type=F32)
    branch_ssm = jnp.dot(ys_ref[...], wbs_ref[...], preferred_element_type=F32)
    branch_ret = jnp.dot(yr_ref[...], wbr_ref[...], preferred_element_type=F32)
    merged = _sigmoid(gates[:, 0:D_MODEL]) * branch_ssm + _sigmoid(gates[:, D_MODEL:2 * D_MODEL]) * branch_ret
    o_ref[...] = h + jnp.dot(merged.astype(BF16), wo_ref[...], preferred_element_type=F32)


def _merge(h, ys, yr, g, wg, wbs, wbr, wo, *, tm):
    t = h.shape[0]
    assert t % tm == 0
    return pl.pallas_call(
        _merge_kernel,
        grid=(t // tm,),
        in_specs=[
            pl.BlockSpec((tm, D_MODEL), lambda i: (i, 0)),
            pl.BlockSpec((tm, SSM_D_INNER), lambda i: (i, 0)),
            pl.BlockSpec((tm, RET_V), lambda i: (i, 0)),
            _const_spec((1, D_MODEL)),
            _const_spec((D_MODEL, 2 * D_MODEL)),
            _const_spec((SSM_D_INNER, D_MODEL)),
            _const_spec((RET_V, D_MODEL)),
            _const_spec((D_MODEL, D_MODEL)),
        ],
        out_specs=pl.BlockSpec((tm, D_MODEL), lambda i: (i, 0)),
        out_shape=jax.ShapeDtypeStruct((t, D_MODEL), F32),
        compiler_params=_params(1),
        name="merge",
    )(h, ys, yr, g, wg, wbs, wbr, wo)


FFN_TILE = 512
SAMPLE_SEQS_PER_STEP = V7X_SUBLANES
SSD_CHUNKS_PER_STEP = 2
RET_CHUNKS_PER_STEP = 2
SAMPLE_RET_HEADS_PER_STEP = 2


def _layer(x2, pos0, ssm0, conv0, ret0, w, final_gain, *, nblk, nc, q, s_blk, ret_heads, time_major):
    t = x2.shape[0]
    tm_ffn = min(FFN_TILE, t)
    steps = q // s_blk
    nseq = nblk * s_blk

    h = _ffn(x2, w["norm_ffn1"], w["ffn1_w1"], w["ffn1_w3"], w["ffn1_w2"], final_gain, final_norm=False, tm=tm_ffn)
    h_rows = h.reshape(steps, nseq, D_MODEL) if time_major else h

    y_ssm, ssm_new, conv_new = _ssd(
        h_rows, w["norm_mix"], w["w_ssd"], ssm0, conv0, w["conv_w"], w["conv_b"], w["dt_bias"], w["a_log"],
        w["d_exp"], w["ssm_norm"], w["head_expand"], nblk=nblk, nc=nc, q=q, s_blk=s_blk, time_major=time_major)
    cos, sin = _rope_tables(w["rope_inv"], nc * q, pos0, s_blk)
    y_ret, ret_new = _ret(h_rows, w["norm_mix"], w["w_q"], w["w_k"], w["w_v"], w["w_rg"], cos, sin, ret0,
                          w["ret_norm"], nblk=nblk, nc=nc, q=q, s_blk=s_blk, n_heads=ret_heads,
                          time_major=time_major)

    h = _merge(h, y_ssm.reshape(t, SSM_D_INNER), y_ret.reshape(t, RET_V), w["norm_mix"], w["w_gates"],
               w["w_branch_ssm"], w["w_branch_ret"], w["w_out"], tm=tm_ffn)
    y = _ffn(h, w["norm_ffn2"], w["ffn2_w1"], w["ffn2_w3"], w["ffn2_w2"], final_gain, final_norm=True, tm=tm_ffn)
    return y, ssm_new, conv_new, ret_new


def _prompt_layer(x, w, final_gain):
    nseq, seq_len, _ = x.shape
    q = math.gcd(seq_len, CHUNK)
    y, ssm_new, conv_new, ret_new = _layer(
        x.reshape(nseq * seq_len, D_MODEL), 0.0,
        jnp.zeros((1, SSM_HEADS * SSM_HEAD_DIM, SSM_STATE), F32),
        jnp.zeros((1, SSM_CONV - 1, SSM_CONV_DIM), F32),
        jnp.zeros((1, RET_HEADS * RET_QK_DIM, RET_V_DIM), F32),
        w, final_gain, nblk=nseq, nc=seq_len // q, q=q, s_blk=1, ret_heads=RET_HEADS, time_major=False)
    return (y.reshape(nseq, seq_len, D_MODEL), ssm_new.reshape(nseq, SSM_HEADS, SSM_HEAD_DIM, SSM_STATE), conv_new,
            ret_new.reshape(nseq, RET_HEADS, RET_QK_DIM, RET_V_DIM))


def _sample_layer(x, ssm0, conv0, ret0, w, final_gain):
    nseq, seq_len, _ = x.shape
    s_blk = SAMPLE_SEQS_PER_STEP
    assert nseq % s_blk == 0 and seq_len <= CHUNK
    nblk = nseq // s_blk
    kc = SSM_CONV - 1
    x_tm = jnp.transpose(x, (1, 0, 2)).reshape(seq_len * nseq, D_MODEL)
    conv0_tm = jnp.transpose(conv0.reshape(nblk, s_blk, kc, SSM_CONV_DIM), (0, 2, 1, 3)).reshape(
        nblk, kc * s_blk, SSM_CONV_DIM)
    y, ssm_new, conv_new, ret_new = _layer(
        x_tm, float(PAST_LEN), ssm0.reshape(nseq, SSM_HEADS * SSM_HEAD_DIM, SSM_STATE), conv0_tm,
        ret0.reshape(nseq, RET_HEADS * RET_QK_DIM, RET_V_DIM), w, final_gain,
        nblk=nblk, nc=1, q=s_blk * seq_len, s_blk=s_blk, ret_heads=SAMPLE_RET_HEADS_PER_STEP, time_major=True)
    conv_new = jnp.transpose(conv_new.reshape(nblk, kc, s_blk, SSM_CONV_DIM), (0, 2, 1, 3)).reshape(
        nseq, kc, SSM_CONV_DIM)
    return (jnp.transpose(y.reshape(seq_len, nseq, D_MODEL), (1, 0, 2)),
            ssm_new.reshape(nseq, SSM_HEADS, SSM_HEAD_DIM, SSM_STATE), conv_new,
            ret_new.reshape(nseq, RET_HEADS, RET_QK_DIM, RET_V_DIM))


def _prep_weights(norm_ffn1, ffn1_w1, ffn1_w3, ffn1_w2, norm_mix, w_in, conv_w, conv_b, dt_bias, a_log, ssm_d,
                  ssm_norm, ret_norm, w_branch_ssm, w_branch_ret, w_out, norm_ffn2, ffn2_w1, ffn2_w3, ffn2_w2):
    o_z = 0
    o_xbc = o_z + SSM_D_INNER
    o_dt = o_xbc + SSM_CONV_DIM
    o_q = o_dt + SSM_HEADS
    o_k = o_q + RET_QK
    o_v = o_k + RET_QK
    o_rg = o_v + RET_V
    o_ga = o_rg + RET_V
    o_end = o_ga + 2 * D_MODEL
    assert o_end == w_in.shape[1]

    w_in_bf = w_in.astype(BF16)

    def cols(a, b):
        return w_in_bf[:, a:b]

    assert (o_z, o_xbc, o_dt) == (SSDP_Z, SSDP_XBC, SSDP_DT) and SSDP_DIM <= o_end
    half = RET_QK_DIM // 2
    head_of_channel = jnp.arange(SSM_D_INNER, dtype=jnp.int32) // SSM_HEAD_DIM
    return {
        "norm_ffn1": norm_ffn1.reshape(1, -1), "ffn1_w1": ffn1_w1.astype(BF16), "ffn1_w3": ffn1_w3.astype(BF16),
        "ffn1_w2": ffn1_w2.astype(BF16),
        "norm_mix": norm_mix.reshape(1, -1), "w_ssd": w_in_bf,
        "w_q": cols(o_q, o_k), "w_k": cols(o_k, o_v), "w_v": cols(o_v, o_rg), "w_rg": cols(o_rg, o_ga),
        "w_gates": cols(o_ga, o_end),
        "conv_w": conv_w, "conv_b": conv_b.reshape(1, -1),
        "dt_bias": jnp.pad(dt_bias, (0, DT_PAD - SSM_HEADS)).reshape(1, -1),
        "a_log": jnp.pad(a_log, (0, DT_PAD - SSM_HEADS)).reshape(1, -1),
        "d_exp": jnp.repeat(ssm_d, SSM_HEAD_DIM).reshape(1, -1),
        "ssm_norm": ssm_norm.reshape(1, -1), "ret_norm": ret_norm.reshape(1, -1),
        "head_expand": (head_of_channel[None, :] == jnp.arange(DT_PAD, dtype=jnp.int32)[:, None]).astype(BF16),
        "rope_inv": (ROPE_BASE ** (-jnp.arange(half, dtype=F32) / half)).reshape(1, half),
        "w_branch_ssm": w_branch_ssm.astype(BF16), "w_branch_ret": w_branch_ret.astype(BF16),
        "w_out": w_out.astype(BF16),
        "norm_ffn2": norm_ffn2.reshape(1, -1), "ffn2_w1": ffn2_w1.astype(BF16), "ffn2_w3": ffn2_w3.astype(BF16),
        "ffn2_w2": ffn2_w2.astype(BF16),
    }


def kernel(x_prompt, x_sample, state_ssm, state_conv, state_ret, norm_ffn1, ffn1_w1, ffn1_w3, ffn1_w2, norm_mix, w_in,
           conv_w, conv_b, dt_bias, a_log, ssm_d, ssm_norm, ret_norm, w_branch_ssm, w_branch_ret, w_out, norm_ffn2,
           ffn2_w1, ffn2_w3, ffn2_w2, norm_final):
    depth = norm_ffn1.shape[0]
    assert depth == 1, "the final RMSNorm is fused into the last layer's second FFN"
    w = _prep_weights(norm_ffn1[0], ffn1_w1[0], ffn1_w3[0], ffn1_w2[0], norm_mix[0], w_in[0], conv_w[0], conv_b[0],
                      dt_bias[0], a_log[0], ssm_d[0], ssm_norm[0], ret_norm[0], w_branch_ssm[0], w_branch_ret[0],
                      w_out[0], norm_ffn2[0], ffn2_w1[0], ffn2_w3[0], ffn2_w2[0])
    final_gain = norm_final.reshape(1, -1)
    yp, ssm_p, conv_p, ret_p = _prompt_layer(x_prompt, w, final_gain)
    ys, ssm_s, conv_s, ret_s = _sample_layer(x_sample, state_ssm[0], state_conv[0], state_ret[0], w, final_gain)
    return (yp, ys, ssm_p[None], conv_p[None], ret_p[None], ssm_s[None], conv_s[None], ret_s[None])
```

```python
import functools
import math

import jax
import jax.numpy as jnp
import numpy as np
from jax import lax
from jax.experimental import pallas as pl
from jax.experimental.pallas import tpu as pltpu

F32 = jnp.float32
BF16 = jnp.bfloat16

D_MODEL = 1024
D_FF = 2816
SSM_D_INNER = 2048
SSM_HEAD_DIM = 64
SSM_HEADS = 32
SSM_GROUPS = 4
SSM_STATE = 128
SSM_CONV = 4
SSM_BC = 2 * SSM_GROUPS * SSM_STATE
SSM_CONV_DIM = SSM_D_INNER + SSM_BC
RET_HEADS = 4
RET_QK_DIM = 256
RET_V_DIM = 512
RET_QK = RET_HEADS * RET_QK_DIM
RET_V = RET_HEADS * RET_V_DIM
ROPE_BASE = 10000.0
PAST_LEN = 16384
CHUNK = 128
NORM_EPS = 1e-6
GATED_NORM_EPS = 1e-5

V7X_SUBLANES = 8
V7X_LANES = 128
V7X_VMEM_BYTES = 64 * 1024 * 1024
VMEM_LIMIT = V7X_VMEM_BYTES - 8 * 1024 * 1024

DT_PAD = V7X_LANES
LOG_DT_FLOOR = -1e30
LOG2_E = math.log2(math.e)

SSDP_Z = 0
SSDP_XBC = SSM_D_INNER
SSDP_DT = SSM_D_INNER + SSM_CONV_DIM
SSDP_DIM = SSDP_DT + DT_PAD


def _rms(x, g, eps):
    return x * lax.rsqrt(jnp.mean(x * x, axis=-1, keepdims=True) + eps) * g


def _sigmoid(x):
    return 0.5 + 0.5 * jnp.tanh(0.5 * x)


def _silu(x):
    half = 0.5 * x
    return half + half * jnp.tanh(half)


def _softplus(x):
    return jnp.maximum(x, 0.0) + jnp.log1p(jnp.exp(-jnp.abs(x)))


def _dot(a, b):
    return jnp.dot(a.astype(BF16), b.astype(BF16), preferred_element_type=F32)


def _dot_nt(a, b):
    return lax.dot_general(a.astype(BF16), b.astype(BF16), (((1,), (1,)), ((), ())), preferred_element_type=F32)


def _dot_tn(a, b):
    return lax.dot_general(a.astype(BF16), b.astype(BF16), (((0,), (0,)), ((), ())), preferred_element_type=F32)


def _dot_split(a, b_f32, passes):
    acc = None
    rem = b_f32
    for _ in range(passes):
        piece = rem.astype(BF16)
        term = jnp.dot(a, piece, preferred_element_type=F32)
        acc = term if acc is None else acc + term
        rem = rem - piece.astype(F32)
    return acc


def _dot_split_lhs(a_f32, b, passes):
    acc = None
    rem = a_f32
    for _ in range(passes):
        piece = rem.astype(BF16)
        term = jnp.dot(piece, b, preferred_element_type=F32)
        acc = term if acc is None else acc + term
        rem = rem - piece.astype(F32)
    return acc


def _transpose_split(eye, x_f32, passes):
    acc = None
    rem = x_f32
    for _ in range(passes):
        piece = rem.astype(BF16)
        term = lax.dot_general(eye, piece, (((1,), (1,)), ((), ())), preferred_element_type=F32)
        acc = term if acc is None else acc + term
        rem = rem - piece.astype(F32)
    return acc


def _const_spec(shape):
    nd = len(shape)
    return pl.BlockSpec(shape, lambda *_: (0,) * nd, pipeline_mode=pl.Buffered(1))


def _params(n_grid_dims, flags=None):
    return pltpu.CompilerParams(dimension_semantics=("arbitrary",) * n_grid_dims, vmem_limit_bytes=VMEM_LIMIT,
                                flags=flags)


def _ffn_kernel(x_ref, g_ref, w1_ref, w3_ref, w2_ref, gf_ref, o_ref, *, final_norm):
    x = x_ref[...]
    xn = _rms(x, g_ref[...], NORM_EPS).astype(BF16)
    a = jnp.dot(xn, w1_ref[...], preferred_element_type=F32)
    b = jnp.dot(xn, w3_ref[...], preferred_element_type=F32)
    gated = (_silu(a) * b).astype(BF16)
    h = x + 0.5 * jnp.dot(gated, w2_ref[...], preferred_element_type=F32)
    if final_norm:
        h = _rms(h, gf_ref[...], NORM_EPS)
    o_ref[...] = h


def _ffn(x, g, w1, w3, w2, gf, *, final_norm, tm):
    t = x.shape[0]
    assert t % tm == 0
    return pl.pallas_call(
        functools.partial(_ffn_kernel, final_norm=final_norm),
        grid=(t // tm,),
        in_specs=[
            pl.BlockSpec((tm, D_MODEL), lambda i: (i, 0)),
            _const_spec((1, D_MODEL)),
            _const_spec((D_MODEL, D_FF)),
            _const_spec((D_MODEL, D_FF)),
            _const_spec((D_FF, D_MODEL)),
            _const_spec((1, D_MODEL)),
        ],
        out_specs=pl.BlockSpec((tm, D_MODEL), lambda i: (i, 0)),
        out_shape=jax.ShapeDtypeStruct((t, D_MODEL), F32),
        compiler_params=_params(1),
        name="ffn_final" if final_norm else "ffn",
    )(x, g, w1, w3, w2, gf)


def _rope_kernel(inv_ref, cos_ref, sin_ref, *, pos0, rows_per_pos):
    rows = cos_ref.shape[0]
    step = lax.broadcasted_iota(jnp.int32, (rows, V7X_LANES), 0) // rows_per_pos
    ang = (pos0 + step.astype(F32)) * inv_ref[...]
    cos_ref[...] = jnp.cos(ang)
    sin_ref[...] = jnp.sin(ang)


def _rope_tables(inv, rows, pos0, rows_per_pos):
    return pl.pallas_call(
        functools.partial(_rope_kernel, pos0=pos0, rows_per_pos=rows_per_pos),
        out_shape=[jax.ShapeDtypeStruct((rows, V7X_LANES), F32)] * 2,
        name="rope",
    )(inv)


def _chunk_rows(ref):
    return ref[...].reshape(-1, ref.shape[-1])


def _row_time_seq(shape, s_blk):
    r = lax.broadcasted_iota(jnp.int32, shape, 0)
    return r // s_blk, r % s_blk


def _pair_mask(q, s_blk):
    r = lax.broadcasted_iota(jnp.int32, (q, q), 0)
    c = lax.broadcasted_iota(jnp.int32, (q, q), 1)
    if s_blk == 1:
        return r >= c
    return jnp.logical_and(r // s_blk >= c // s_blk, r % s_blk == c % s_blk)


def _rows_spec(q, s_blk, nblk, nc, width, time_major, *, ahead=0, lead_axes=0, chunks_per_step=1):
    last = nblk * nc - 1

    def chunk_index(args):
        s, c = args[lead_axes], args[lead_axes + 1]
        return jnp.minimum((s * (nc // chunks_per_step) + c) * chunks_per_step + ahead, last)

    if time_major:
        assert nc == 1
        return pl.BlockSpec((q // s_blk, s_blk, width), lambda *a: (0, chunk_index(a), 0))
    return pl.BlockSpec((q, width), lambda *a: (chunk_index(a), 0))


def _project(h_ref, g_ref, w_refs):
    u = _rms(_chunk_rows(h_ref), g_ref[...], NORM_EPS).astype(BF16)
    return [jnp.dot(u, w_ref[...], preferred_element_type=F32) for w_ref in w_refs]


def _conv_pad_rows(s_blk):
    need = (SSM_CONV - 1) * s_blk
    return -(-need // V7X_SUBLANES) * V7X_SUBLANES


def _ssd_kernel(hc_ref, *refs, s_blk, nc, chunks_per_step):
    hn_refs, refs = refs[:chunks_per_step], refs[chunks_per_step:]
    n_in = 11
    y_ref = refs[n_in]
    for k, hn_ref in enumerate(hn_refs):
        if chunks_per_step == 1:
            y_view = y_ref
        else:
            q = y_ref.shape[0] // chunks_per_step
            y_view = y_ref.at[pl.ds(k * q, q)]
        _ssd_chunk(hc_ref, hn_ref, *refs[:n_in], y_view, *refs[n_in + 1:], s_blk=s_blk, nc=nc,
                   chunks_per_step=chunks_per_step, first_of_step=(k == 0))


def _ssd_chunk(hc_ref, hn_ref, gmix_ref, w_ref, h0_ref, conv0_ref, cw_ref, cb_ref, dtb_ref, alog_ref, dexp_ref,
               gn_ref, e_ref, y_ref, hout_ref, convout_ref, xp_scr, zdt_scr, next_scr, *, s_blk, nc, chunks_per_step,
               first_of_step):
    s = pl.program_id(0)
    c = pl.program_id(1)
    pad = _conv_pad_rows(s_blk)
    q = xp_scr.shape[0] - pad
    carry = (SSM_CONV - 1) * s_blk
    steps = q // s_blk
    hg = SSM_HEADS // SSM_GROUPS
    gw = hg * SSM_HEAD_DIM
    h_src = h0_ref if nc == 1 else hout_ref

    if first_of_step:
        @pl.when(jnp.logical_and(s == 0, c == 0))
        def _first():
            next_scr[...] = _project(hc_ref, gmix_ref, [w_ref])[0]

        @pl.when(c == 0)
        def _init():
            xp_scr[0:pad, :] = jnp.zeros((pad, SSM_CONV_DIM), F32)
            xp_scr[pad - carry:pad, :] = conv0_ref[0]
            if nc > 1:
                hout_ref[...] = h0_ref[...]

    xp_scr[pad:pad + q, :] = next_scr[:, SSDP_XBC:SSDP_DT]
    zdt_scr[:, 0:SSM_D_INNER] = next_scr[:, SSDP_Z:SSDP_XBC]
    zdt_scr[:, SSM_D_INNER:SSM_D_INNER + DT_PAD] = next_scr[:, SSDP_DT:SSDP_DIM]
    next_scr[...] = _project(hn_ref, gmix_ref, [w_ref])[0]

    z_gate = _silu(zdt_scr[:, 0:SSM_D_INNER])

    xall = xp_scr[...]
    conv = cb_ref[...] + xall[pad:pad + q, :] * cw_ref[SSM_CONV - 1:SSM_CONV, :]
    for j in range(1, SSM_CONV):
        d = j * s_blk
        if d % V7X_SUBLANES == 0:
            back = xall[pad - d:pad - d + q, :]
        else:
            back = pltpu.roll(xall, d, 0)[pad:pad + q, :]
        conv = conv + back * cw_ref[SSM_CONV - 1 - j:SSM_CONV - j, :]
    @pl.when(c == nc // chunks_per_step - 1)
    def _():
        convout_ref[0] = xp_scr[pad + q - carry:pad + q, :]

    if nc > 1:
        xp_scr[0:pad, :] = xp_scr[q:q + pad, :]
    dt_rows = zdt_scr[:, SSM_D_INNER:SSM_D_INNER + DT_PAD]
    xc = _silu(conv)
    xs = xc[:, 0:SSM_D_INNER]
    bm = xc[:, SSM_D_INNER:SSM_D_INNER + SSM_GROUPS * SSM_STATE]
    cm = xc[:, SSM_D_INNER + SSM_GROUPS * SSM_STATE:SSM_CONV_DIM]

    dt = _softplus(dt_rows + dtb_ref[...])
    a = -jnp.exp(alog_ref[...])
    visible = _pair_mask(q, s_blk)
    la = _dot_split(jnp.where(visible, 1.0, 0.0).astype(BF16), dt * (a * LOG2_E), 3)
    lsd = la - jnp.maximum(jnp.log2(dt), LOG_DT_FLOOR)
    eye_r = lax.broadcasted_iota(jnp.int32, (DT_PAD, DT_PAD), 0)
    eye_c = lax.broadcasted_iota(jnp.int32, (DT_PAD, DT_PAD), 1)
    eye = jnp.where(eye_r == eye_c, 1.0, 0.0).astype(BF16)
    la_t = _transpose_split(eye, la, 3)
    lsd_t = _transpose_split(eye, lsd, 3)
    la_last = la[q - s_blk:q, :]
    la_last_rows = la_last if s_blk == 1 else jnp.concatenate([la_last] * steps, axis=0)
    dec_t = jnp.exp2(la_t[:, q - s_blk:q])

    lane = lax.broadcasted_iota(jnp.int32, (q, 2 * SSM_HEAD_DIM), 1)
    first_head = lane < SSM_HEAD_DIM
    y_parts = []
    for g in range(SSM_GROUPS):
        cg = cm[:, g * SSM_STATE:(g + 1) * SSM_STATE]
        bg = bm[:, g * SSM_STATE:(g + 1) * SSM_STATE]
        cbg = _dot_nt(cg, bg)
        for pair in range(hg // 2):
            h0 = g * hg + 2 * pair
            ws = []
            for h in (h0, h0 + 1):
                seg = la[:, h:h + 1] - lsd_t[h:h + 1, :]
                ws.append(cbg * jnp.exp2(jnp.where(visible, seg, -jnp.inf)))
            xpair = xs[:, h0 * SSM_HEAD_DIM:(h0 + 2) * SSM_HEAD_DIM]
            rhs = jnp.concatenate([jnp.where(first_head, xpair, 0.0), jnp.where(first_head, 0.0, xpair)], axis=0)
            y_parts.append(_dot(jnp.concatenate(ws, axis=1), rhs))
    y = jnp.concatenate(y_parts, axis=1)

    _, seq_b = _row_time_seq((q, SSM_STATE), s_blk)
    y_state_parts = []
    for g in range(SSM_GROUPS):
        cg = cm[:, g * SSM_STATE:(g + 1) * SSM_STATE]
        if s_blk == 1:
            c_seqs, h_seqs = cg, h_src[0, g * gw:(g + 1) * gw, :]
        else:
            c_seqs = jnp.concatenate([jnp.where(seq_b == b, cg, 0.0) for b in range(s_blk)], axis=1)
            h_seqs = jnp.concatenate([h_src[b, g * gw:(g + 1) * gw, :].astype(BF16) for b in range(s_blk)], axis=1)
        y_state_parts.append(_dot_nt(c_seqs, h_seqs))
    y_state = jnp.concatenate(y_state_parts, axis=1)

    e = e_ref[...]
    y = y + y_state * _dot_split_lhs(jnp.exp2(la), e, 2) + xs * dexp_ref[...]
    y = y * z_gate
    y = _rms(y, gn_ref[...], GATED_NORM_EPS)
    y_ref[...] = y.reshape(y_ref.shape).astype(y_ref.dtype)

    tail = jnp.exp2(la_last_rows - lsd)
    xt = (xs * _dot_split_lhs(tail, e, 2)).astype(BF16)
    dec =[jnp.broadcast_to(dec_t[:, b:b + 1], (DT_PAD, SSM_STATE)) for b in range(s_blk)]
    for g in range(SSM_GROUPS):
        bg = bm[:, g * SSM_STATE:(g + 1) * SSM_STATE]
        bg_seqs = bg if s_blk == 1 else jnp.concatenate([jnp.where(seq_b == b, bg, 0.0) for b in range(s_blk)], axis=1)
        upd = _dot_tn(xt[:, g * gw:(g + 1) * gw], bg_seqs)
        for b in range(s_blk):
            for hh in range(hg):
                h = g * hg + hh
                r0 = h * SSM_HEAD_DIM
                hout_ref[b, r0:r0 + SSM_HEAD_DIM, :] = (
                    h_src[b, r0:r0 + SSM_HEAD_DIM, :] * dec[b][h:h + 1, :]
                    + upd[hh * SSM_HEAD_DIM:(hh + 1) * SSM_HEAD_DIM, b * SSM_STATE:(b + 1) * SSM_STATE])


def _ssd(h, gmix, w, h0, conv0, cw, cb, dtb, alog, dexp, gn, e, *, nblk, nc, q, s_blk, time_major):
    rows = SSM_HEADS * SSM_HEAD_DIM
    carry = (SSM_CONV - 1) * s_blk
    per_block0 = h0.shape[0] == nblk * s_blk
    assert per_block0 or (h0.shape[0] == s_blk and conv0.shape[0] == 1)
    cps = SSD_CHUNKS_PER_STEP if (nc % SSD_CHUNKS_PER_STEP == 0 and not time_major) else 1
    ncs = nc // cps
    spec = functools.partial(_rows_spec, q, s_blk, nblk, nc, time_major=time_major, chunks_per_step=cps)
    if time_major:
        y_spec = spec(SSM_D_INNER)
        y_shape = (q // s_blk, nblk * s_blk, SSM_D_INNER)
    else:
        y_spec = pl.BlockSpec((cps * q, SSM_D_INNER), lambda s, c: (s * ncs + c, 0))
        y_shape = (nblk * nc * q, SSM_D_INNER)
    return pl.pallas_call(
        functools.partial(_ssd_kernel, s_blk=s_blk, nc=nc, chunks_per_step=cps),
        grid=(nblk, ncs),
        in_specs=[spec(D_MODEL, ahead=k) for k in range(cps + 1)] + [
            _const_spec((1, D_MODEL)),
            _const_spec((D_MODEL, SSDP_DIM)),
            pl.BlockSpec((s_blk, rows, SSM_STATE), lambda s, c: (s if per_block0 else 0, 0, 0)),
            pl.BlockSpec((1, carry, SSM_CONV_DIM), lambda s, c: (s if per_block0 else 0, 0, 0)),
            _const_spec((SSM_CONV, SSM_CONV_DIM)),
            _const_spec((1, SSM_CONV_DIM)),
            _const_spec((1, DT_PAD)),
            _const_spec((1, DT_PAD)),
            _const_spec((1, SSM_D_INNER)),
            _const_spec((1, SSM_D_INNER)),
            _const_spec((DT_PAD, SSM_D_INNER)),
        ],
        out_specs=[
            y_spec,
            pl.BlockSpec((s_blk, rows, SSM_STATE), lambda s, c: (s, 0, 0)),
            pl.BlockSpec((1, carry, SSM_CONV_DIM), lambda s, c: (s, 0, 0)),
        ],
        out_shape=[
            jax.ShapeDtypeStruct(y_shape, BF16),
            jax.ShapeDtypeStruct((nblk * s_blk, rows, SSM_STATE), F32),
            jax.ShapeDtypeStruct((nblk, carry, SSM_CONV_DIM), F32),
        ],
        scratch_shapes=[
            pltpu.VMEM((_conv_pad_rows(s_blk) + q, SSM_CONV_DIM), F32),
            pltpu.VMEM((q, SSM_D_INNER + DT_PAD), F32),
            pltpu.VMEM((q, SSDP_DIM), F32),
        ],
        compiler_params=_params(2),
        name="ssd",
    )(*([h] * (cps + 1)), gmix, w, h0, conv0, cw, cb, dtb, alog, dexp, gn, e)


def _ret_log_decay(h):
    return float(np.log1p(-np.exp2(-5.0 - h)))


def _ret_project(h_ref, gmix_ref, w_refs, dst):
    pq, pk, pv, pg = _project(h_ref, gmix_ref, w_refs)
    dst[0][...] = pq
    dst[1][...] = pk
    dst[2][...] = pv.astype(BF16)
    dst[3][...] = pg


def _ret_kernel(*refs, s_blk, nc, n_heads, chunks_per_step):
    io_refs, sets = refs[:-8], (refs[-8:-4], refs[-4:])
    hc_ref, hn_refs, io_refs = io_refs[0], io_refs[1:1 + chunks_per_step], io_refs[1 + chunks_per_step:]
    gmix_ref = io_refs[0]
    w_refs = io_refs[1:5]
    cos_ref, sin_ref = io_refs[5:7]
    tables = io_refs[7:10]
    s0_ref, gn_ref, y_ref, sout_ref = io_refs[10:14]
    s = pl.program_id(1)
    c = pl.program_id(2)
    q = cos_ref.shape[0] // chunks_per_step

    @pl.when(jnp.logical_and(s == 0, c == 0))
    def _first():
        _ret_project(hc_ref, gmix_ref, w_refs, sets[0])

    if nc > 1:
        @pl.when(c == 0)
        def _init():
            sout_ref[...] = s0_ref[...]

    def chunk(k, hn_ref, cur_set, next_set):
        rows = pl.ds(k * q, q)
        y_view = y_ref if chunks_per_step == 1 else y_ref.at[rows]
        _ret_step(hn_ref, gmix_ref, *w_refs, cos_ref.at[rows], sin_ref.at[rows], *tables, s0_ref, gn_ref, y_view,
                  sout_ref, cur_set, next_set, s_blk=s_blk, nc=nc, n_heads=n_heads)

    if chunks_per_step % 2 == 0:
        for k, hn_ref in enumerate(hn_refs):
            chunk(k, hn_ref, sets[k % 2], sets[(k + 1) % 2])
    else:
        assert chunks_per_step == 1
        parity = (s * nc + c) % 2

        @pl.when(parity == 0)
        def _even():
            chunk(0, hn_refs[0], sets[0], sets[1])

        @pl.when(parity == 1)
        def _odd():
            chunk(0, hn_refs[0], sets[1], sets[0])


def _ret_step(hn_ref, gmix_ref, wq_ref, wk_ref, wv_ref, wg_ref, cos_ref, sin_ref, dmat_ref, qdec_ref, kdec_ref,
              s0_ref, gn_ref, y_ref, sout_ref, cur_set, next_set, *, s_blk, nc, n_heads):
    q_scr, k_scr, v_scr, g_scr = cur_set
    hb = pl.program_id(0)
    q = cos_ref.shape[0]
    steps = q // s_blk
    half = RET_QK_DIM // 2
    s_src = s0_ref if nc == 1 else sout_ref

    _ret_project(hn_ref, gmix_ref, [wq_ref, wk_ref, wv_ref, wg_ref], next_set)

    qq = q_scr[...]
    kk = k_scr[...]
    vv = v_scr[...]
    rg = g_scr[...]
    cos = cos_ref[...]
    sin = sin_ref[...]
    _, seq_qk = _row_time_seq((q, RET_QK_DIM), s_blk)

    def rot(t, h):
        t1 = t[:, h * RET_QK_DIM:h * RET_QK_DIM + half]
        t2 = t[:, h * RET_QK_DIM + half:(h + 1) * RET_QK_DIM]
        return jnp.concatenate([t1 * cos - t2 * sin, t1 * sin + t2 * cos], axis=1)

    y_parts = []
    for h in range(n_heads):
        if n_heads == RET_HEADS:
            chunk_dec = math.exp(steps * _ret_log_decay(h))
        else:
            head = hb * n_heads + h
            chunk_dec = jnp.float32(math.exp(steps * _ret_log_decay(0)))
            for hh in range(1, RET_HEADS):
                chunk_dec = jnp.where(head == hh, jnp.float32(math.exp(steps * _ret_log_decay(hh))), chunk_dec)
        dmat = dmat_ref[h]
        q_dec = jnp.concatenate([qdec_ref[h]] * 2, axis=1)
        k_dec = jnp.concatenate([kdec_ref[h]] * 2, axis=1)
        qr = rot(qq, h)
        kr = rot(kk, h) * (RET_QK_DIM ** -0.5)
        vh = vv[:, h * RET_V_DIM:(h + 1) * RET_V_DIM]
        scores = _dot_nt(qr, kr) * dmat
        y = _dot(scores, vh)
        qd = qr * q_dec
        kd = kr * k_dec
        r0 = h * RET_QK_DIM
        for b in range(s_blk):
            qd_b = (qd if s_blk == 1 else jnp.where(seq_qk == b, qd, 0.0)).astype(BF16)
            kd_b = (kd if s_blk == 1 else jnp.where(seq_qk == b, kd, 0.0)).astype(BF16)
            sh = s_src[b, r0:r0 + RET_QK_DIM, :]
            y = y + _dot(qd_b, sh)
            sout_ref[b, r0:r0 + RET_QK_DIM, :] = sh * chunk_dec + _dot_tn(kd_b, vh)
        g = gn_ref[:, h * RET_V_DIM:(h + 1) * RET_V_DIM]
        y = _rms(y, g, NORM_EPS) * _silu(rg[:, h * RET_V_DIM:(h + 1) * RET_V_DIM])
        y_parts.append(y)
    y = y_parts[0] if n_heads == 1 else jnp.concatenate(y_parts, axis=1)
    y_ref[...] = y.reshape(y_ref.shape).astype(y_ref.dtype)


def _ret_decay_tables(q, s_blk):
    t = np.arange(q) // s_blk
    seq = np.arange(q) % s_blk
    steps = q // s_blk
    rel = t[:, None] - t[None, :]
    visible = (rel >= 0) & (seq[:, None] == seq[None, :])
    dmat, qdec, kdec = [], [], []
    for h in range(RET_HEADS):
        lg = _ret_log_decay(h)
        dmat.append(np.where(visible, np.exp(np.maximum(rel, 0) * lg), 0.0))
        qdec.append(np.broadcast_to(np.exp((t + 1.0) * lg)[:, None], (q, V7X_LANES)))
        kdec.append(np.broadcast_to(np.exp((steps - 1.0 - t) * lg)[:, None], (q, V7X_LANES)))
    return tuple(jnp.asarray(np.stack(a), F32) for a in (dmat, qdec, kdec))


def _ret(h, gmix, wq, wk, wv, wg, cos, sin, s0, gn, *, nblk, nc, q, s_blk, n_heads, time_major):
    dmat, qdec, kdec = _ret_decay_tables(q, s_blk)
    nhb = RET_HEADS // n_heads
    qk_w = n_heads * RET_QK_DIM
    v_w = n_heads * RET_V_DIM
    per_block0 = s0.shape[0] == nblk * s_blk
    assert per_block0 or s0.shape[0] == s_blk
    cps = RET_CHUNKS_PER_STEP if (nc % RET_CHUNKS_PER_STEP == 0 and not time_major) else 1
    ncs = nc // cps
    rows = functools.partial(_rows_spec, q, s_blk, nblk, nc, D_MODEL, time_major, lead_axes=1, chunks_per_step=cps)

    def head_cols(nrows, width):
        if nhb == 1:
            return _const_spec((nrows, width))
        return pl.BlockSpec((nrows, width), lambda hb, s, c: (0, hb))

    if time_major:
        y_spec = pl.BlockSpec((q // s_blk, s_blk, v_w), lambda hb, s, c: (0, s, hb))
        y_shape = (q // s_blk, nblk * s_blk, RET_V)
    else:
        y_spec = pl.BlockSpec((cps * q, v_w), lambda hb, s, c: (s * ncs + c, hb))
        y_shape = (nblk * nc * q, RET_V)
    return pl.pallas_call(
        functools.partial(_ret_kernel, s_blk=s_blk, nc=nc, n_heads=n_heads, chunks_per_step=cps),
        grid=(nhb, nblk, ncs),
        in_specs=[rows(ahead=k) for k in range(cps + 1)] + [
            _const_spec((1, D_MODEL)),
            head_cols(D_MODEL, qk_w),
            head_cols(D_MODEL, qk_w),
            head_cols(D_MODEL, v_w),
            head_cols(D_MODEL, v_w),
            pl.BlockSpec((cps * q, V7X_LANES), lambda hb, s, c: (c, 0)),
            pl.BlockSpec((cps * q, V7X_LANES), lambda hb, s, c: (c, 0)),
            pl.BlockSpec((n_heads, q, q), lambda hb, s, c: (hb, 0, 0)),
            pl.BlockSpec((n_heads, q, V7X_LANES), lambda hb, s, c: (hb, 0, 0)),
            pl.BlockSpec((n_heads, q, V7X_LANES), lambda hb, s, c: (hb, 0, 0)),
            pl.BlockSpec((s_blk, qk_w, RET_V_DIM), lambda hb, s, c: (s if per_block0 else 0, hb, 0)),
            head_cols(1, v_w),
        ],
        out_specs=[
            y_spec,
            pl.BlockSpec((s_blk, qk_w, RET_V_DIM), lambda hb, s, c: (s, hb, 0)),
        ],
        out_shape=[
            jax.ShapeDtypeStruct(y_shape, BF16),
            jax.ShapeDtypeStruct((nblk * s_blk, RET_HEADS * RET_QK_DIM, RET_V_DIM), F32),
        ],
        scratch_shapes=[
            pltpu.VMEM((q, qk_w), F32),
            pltpu.VMEM((q, qk_w), F32),
            pltpu.VMEM((q, v_w), BF16),
            pltpu.VMEM((q, v_w), F32),
        ] * 2,
        compiler_params=_params(3),
        name="ret",
    )(*([h] * (cps + 1)), gmix, wq, wk, wv, wg, cos, sin, dmat, qdec, kdec, s0, gn)


def _merge_kernel(h_ref, ys_ref, yr_ref, g_ref, wg_ref, wbs_ref, wbr_ref, wo_ref, o_ref):
    h = h_ref[...]
    u = _rms(h, g_ref[...], NORM_EPS).astype(BF16)
    gates = jnp.dot(u, wg_ref[...], preferred_element_type=F32)
    branch_ssm = jnp.dot(ys_ref[...], wbs_ref[...], preferred_element_type=F32)
    branch_ret = jnp.dot(yr_ref[...], wbr_ref[...], preferred_element_type=F32)
    merged = _sigmoid(gates[:, 0:D_MODEL]) * branch_ssm + _sigmoid(gates[:, D_MODEL:2 * D_MODEL]) * branch_ret
    o_ref[...] = h + jnp.dot(merged.astype(BF16), wo_ref[...], preferred_element_type=F32)


def _merge(h, ys, yr, g, wg, wbs, wbr, wo, *, tm):
    t = h.shape[0]
    assert t % tm == 0
    return pl.pallas_call(
        _merge_kernel,
        grid=(t // tm,),
        in_specs=[
            pl.BlockSpec((tm, D_MODEL), lambda i: (i, 0)),
            pl.BlockSpec((tm, SSM_D_INNER), lambda i: (i, 0)),
            pl.BlockSpec((tm, RET_V), lambda i: (i, 0)),
            _const_spec((1, D_MODEL)),
            _const_spec((D_MODEL, 2 * D_MODEL)),
            _const_spec((SSM_D_INNER, D_MODEL)),
            _const_spec((RET_V, D_MODEL)),
            _const_spec((D_MODEL, D_MODEL)),
        ],
        out_specs=pl.BlockSpec((tm, D_MODEL), lambda i: (i, 0)),
        out_shape=jax.ShapeDtypeStruct((t, D_MODEL), F32),
        compiler_params=_params(1),
        name="merge",
    )(h, ys, yr, g, wg, wbs, wbr, wo)


FFN_TILE = 512
SAMPLE_SEQS_PER_STEP = V7X_SUBLANES
SSD_CHUNKS_PER_STEP = 4
RET_CHUNKS_PER_STEP = 4
SAMPLE_RET_HEADS_PER_STEP = 2


def _layer(x2, pos0, ssm0, conv0, ret0, w, final_gain, *, nblk, nc, q, s_blk, ret_heads, time_major):
    t = x2.shape[0]
    tm_ffn = min(FFN_TILE, t)
    steps = q // s_blk
    nseq = nblk * s_blk

    h = _ffn(x2, w["norm_ffn1"], w["ffn1_w1"], w["ffn1_w3"], w["ffn1_w2"], final_gain, final_norm=False, tm=tm_ffn)
    h_rows = h.reshape(steps, nseq, D_MODEL) if time_major else h

    y_ssm, ssm_new, conv_new = _ssd(
        h_rows, w["norm_mix"], w["w_ssd"], ssm0, conv0, w["conv_w"], w["conv_b"], w["dt_bias"], w["a_log"],
        w["d_exp"], w["ssm_norm"], w["head_expand"], nblk=nblk, nc=nc, q=q, s_blk=s_blk, time_major=time_major)
    cos, sin = _rope_tables(w["rope_inv"], nc * q, pos0, s_blk)
    y_ret, ret_new = _ret(h_rows, w["norm_mix"], w["w_q"], w["w_k"], w["w_v"], w["w_rg"], cos, sin, ret0,
                          w["ret_norm"], nblk=nblk, nc=nc, q=q, s_blk=s_blk, n_heads=ret_heads,
                          time_major=time_major)

    h = _merge(h, y_ssm.reshape(t, SSM_D_INNER), y_ret.reshape(t, RET_V), w["norm_mix"], w["w_gates"],
               w["w_branch_ssm"], w["w_branch_ret"], w["w_out"], tm=tm_ffn)
    y = _ffn(h, w["norm_ffn2"], w["ffn2_w1"], w["ffn2_w3"], w["ffn2_w2"], final_gain, final_norm=True, tm=tm_ffn)
    return y, ssm_new, conv_new, ret_new


def _prompt_layer(x, w, final_gain):
    nseq, seq_len, _ = x.shape
    q = math.gcd(seq_len, CHUNK)
    y, ssm_new, conv_new, ret_new = _layer(
        x.reshape(nseq * seq_len, D_MODEL), 0.0,
        jnp.zeros((1, SSM_HEADS * SSM_HEAD_DIM, SSM_STATE), F32),
        jnp.zeros((1, SSM_CONV - 1, SSM_CONV_DIM), F32),
        jnp.zeros((1, RET_HEADS * RET_QK_DIM, RET_V_DIM), F32),
        w, final_gain, nblk=nseq, nc=seq_len // q, q=q, s_blk=1, ret_heads=RET_HEADS, time_major=False)
    return (y.reshape(nseq, seq_len, D_MODEL), ssm_new.reshape(nseq, SSM_HEADS, SSM_HEAD_DIM, SSM_STATE), conv_new,
            ret_new.reshape(nseq, RET_HEADS, RET_QK_DIM, RET_V_DIM))


def _sample_layer(x, ssm0, conv0, ret0, w, final_gain):
    nseq, seq_len, _ = x.shape
    s_blk = SAMPLE_SEQS_PER_STEP
    assert nseq % s_blk == 0 and seq_len <= CHUNK
    nblk = nseq // s_blk
    kc = SSM_CONV - 1
    x_tm = jnp.transpose(x, (1, 0, 2)).reshape(seq_len * nseq, D_MODEL)
    conv0_tm = jnp.transpose(conv0.reshape(nblk, s_blk, kc, SSM_CONV_DIM), (0, 2, 1, 3)).reshape(
        nblk, kc * s_blk, SSM_CONV_DIM)
    y, ssm_new, conv_new, ret_new = _layer(
        x_tm, float(PAST_LEN), ssm0.reshape(nseq, SSM_HEADS * SSM_HEAD_DIM, SSM_STATE), conv0_tm,
        ret0.reshape(nseq, RET_HEADS * RET_QK_DIM, RET_V_DIM), w, final_gain,
        nblk=nblk, nc=1, q=s_blk * seq_len, s_blk=s_blk, ret_heads=SAMPLE_RET_HEADS_PER_STEP, time_major=True)
    conv_new = jnp.transpose(conv_new.reshape(nblk, kc, s_blk, SSM_CONV_DIM), (0, 2, 1, 3)).reshape(
        nseq, kc, SSM_CONV_DIM)
    return (jnp.transpose(y.reshape(seq_len, nseq, D_MODEL), (1, 0, 2)),
            ssm_new.reshape(nseq, SSM_HEADS, SSM_HEAD_DIM, SSM_STATE), conv_new,
            ret_new.reshape(nseq, RET_HEADS, RET_QK_DIM, RET_V_DIM))


def _prep_weights(norm_ffn1, ffn1_w1, ffn1_w3, ffn1_w2, norm_mix, w_in, conv_w, conv_b, dt_bias, a_log, ssm_d,
                  ssm_norm, ret_norm, w_branch_ssm, w_branch_ret, w_out, norm_ffn2, ffn2_w1, ffn2_w3, ffn2_w2):
    o_z = 0
    o_xbc = o_z + SSM_D_INNER
    o_dt = o_xbc + SSM_CONV_DIM
    o_q = o_dt + SSM_HEADS
    o_k = o_q + RET_QK
    o_v = o_k + RET_QK
    o_rg = o_v + RET_V
    o_ga = o_rg + RET_V
    o_end = o_ga + 2 * D_MODEL
    assert o_end == w_in.shape[1]

    w_in_bf = w_in.astype(BF16)

    def cols(a, b):
        return w_in_bf[:, a:b]

    assert (o_z, o_xbc, o_dt) == (SSDP_Z, SSDP_XBC, SSDP_DT) and SSDP_DIM <= o_end
    half = RET_QK_DIM // 2
    head_of_channel = jnp.arange(SSM_D_INNER, dtype=jnp.int32) // SSM_HEAD_DIM
    return {
        "norm_ffn1": norm_ffn1.reshape(1, -1), "ffn1_w1": ffn1_w1.astype(BF16), "ffn1_w3": ffn1_w3.astype(BF16),
        "ffn1_w2": ffn1_w2.astype(BF16),
        "norm_mix": norm_mix.reshape(1, -1), "w_ssd": w_in_bf,
        "w_q": cols(o_q, o_k), "w_k": cols(o_k, o_v), "w_v": cols(o_v, o_rg), "w_rg": cols(o_rg, o_ga),
        "w_gates": cols(o_ga, o_end),
        "conv_w": conv_w, "conv_b": conv_b.reshape(1, -1),
        "dt_bias": jnp.pad(dt_bias, (0, DT_PAD - SSM_HEADS)).reshape(1, -1),
        "a_log": jnp.pad(a_log, (0, DT_PAD - SSM_HEADS)).reshape(1, -1),
        "d_exp": jnp.repeat(ssm_d, SSM_HEAD_DIM).reshape(1, -1),
        "ssm_norm": ssm_norm.reshape(1, -1), "ret_norm": ret_norm.reshape(1, -1),
        "head_expand": (head_of_channel[None, :] == jnp.arange(DT_PAD, dtype=jnp.int32)[:, None]).astype(BF16),
        "rope_inv": (ROPE_BASE ** (-jnp.arange(half, dtype=F32) / half)).reshape(1, half),
        "w_branch_ssm": w_branch_ssm.astype(BF16), "w_branch_ret": w_branch_ret.astype(BF16),
        "w_out": w_out.astype(BF16),
        "norm_ffn2": norm_ffn2.reshape(1, -1), "ffn2_w1": ffn2_w1.astype(BF16), "ffn2_w3": ffn2_w3.astype(BF16),
        "ffn2_w2": ffn2_w2.astype(BF16),
    }


def kernel(x_prompt, x_sample, state_ssm, state_conv, state_ret, norm_ffn1, ffn1_w1, ffn1_w3, ffn1_w2, norm_mix, w_in,
           conv_w, conv_b, dt_bias, a_log, ssm_d, ssm_norm, ret_norm, w_branch_ssm, w_branch_ret, w_out, norm_ffn2,
           ffn2_w1, ffn2_w3, ffn2_w2, norm_final):
    depth = norm_ffn1.shape[0]
    assert depth == 1, "the final RMSNorm is fused into the last layer's second FFN"
    w = _prep_weights(norm_ffn1[0], ffn1_w1[0], ffn1_w3[0], ffn1_w2[0], norm_mix[0], w_in[0], conv_w[0], conv_b[0],
                      dt_bias[0], a_log[0], ssm_d[0], ssm_norm[0], ret_norm[0], w_branch_ssm[0], w_branch_ret[0],
                      w_out[0], norm_ffn2[0], ffn2_w1[0], ffn2_w3[0], ffn2_w2[0])
    final_gain = norm_final.reshape(1, -1)
    yp, ssm_p, conv_p, ret_p = _prompt_layer(x_prompt, w, final_gain)
    ys, ssm_s, conv_s, ret_s = _sample_layer(x_sample, state_ssm[0], state_conv[0], state_ret[0], w, final_gain)
    return (yp, ys, ssm_p[None], conv_p[None], ret_p[None], ssm_s[None], conv_s[None], ret_s[None])
```

```python
import functools
import math

import jax
import jax.numpy as jnp
import numpy as np
from jax import lax
from jax.experimental import pallas as pl
from jax.experimental.pallas import tpu as pltpu

F32 = jnp.float32
BF16 = jnp.bfloat16

D_MODEL = 1024
D_FF = 2816
SSM_D_INNER = 2048
SSM_HEAD_DIM = 64
SSM_HEADS = 32
SSM_GROUPS = 4
SSM_STATE = 128
SSM_CONV = 4
SSM_BC = 2 * SSM_GROUPS * SSM_STATE
SSM_CONV_DIM = SSM_D_INNER + SSM_BC
RET_HEADS = 4
RET_QK_DIM = 256
RET_V_DIM = 512
RET_QK = RET_HEADS * RET_QK_DIM
RET_V = RET_HEADS * RET_V_DIM
ROPE_BASE = 10000.0
PAST_LEN = 16384
CHUNK = 128
NORM_EPS = 1e-6
GATED_NORM_EPS = 1e-5

V7X_SUBLANES = 8
V7X_LANES = 128
V7X_VMEM_BYTES = 64 * 1024 * 1024
VMEM_LIMIT = V7X_VMEM_BYTES - 8 * 1024 * 1024

DT_PAD = V7X_LANES
LOG_DT_FLOOR = -1e30
LOG2_E = math.log2(math.e)

SSDP_Z = 0
SSDP_XBC = SSM_D_INNER
SSDP_DT = SSM_D_INNER + SSM_CONV_DIM
SSDP_DIM = SSDP_DT + DT_PAD


def _rms(x, g, eps):
    return x * lax.rsqrt(jnp.mean(x * x, axis=-1, keepdims=True) + eps) * g


def _sigmoid(x):
    return 0.5 + 0.5 * jnp.tanh(0.5 * x)


def _silu(x):
    half = 0.5 * x
    return half + half * jnp.tanh(half)


def _softplus(x):
    return jnp.maximum(x, 0.0) + jnp.log1p(jnp.exp(-jnp.abs(x)))


def _dot(a, b):
    return jnp.dot(a.astype(BF16), b.astype(BF16), preferred_element_type=F32)


def _dot_nt(a, b):
    return lax.dot_general(a.astype(BF16), b.astype(BF16), (((1,), (1,)), ((), ())), preferred_element_type=F32)


def _dot_tn(a, b):
    return lax.dot_general(a.astype(BF16), b.astype(BF16), (((0,), (0,)), ((), ())), preferred_element_type=F32)


def _dot_split(a, b_f32, passes):
    acc = None
    rem = b_f32
    for _ in range(passes):
        piece = rem.astype(BF16)
        term = jnp.dot(a, piece, preferred_element_type=F32)
        acc = term if acc is None else acc + term
        rem = rem - piece.astype(F32)
    return acc


def _dot_split_lhs(a_f32, b, passes):
    acc = None
    rem = a_f32
    for _ in range(passes):
        piece = rem.astype(BF16)
        term = jnp.dot(piece, b, preferred_element_type=F32)
        acc = term if acc is None else acc + term
        rem = rem - piece.astype(F32)
    return acc


def _transpose_split(eye, x_f32, passes):
    acc = None
    rem = x_f32
    for _ in range(passes):
        piece = rem.astype(BF16)
        term = lax.dot_general(eye, piece, (((1,), (1,)), ((), ())), preferred_element_type=F32)
        acc = term if acc is None else acc + term
        rem = rem - piece.astype(F32)
    return acc


def _const_spec(shape):
    nd = len(shape)
    return pl.BlockSpec(shape, lambda *_: (0,) * nd, pipeline_mode=pl.Buffered(1))


def _params(n_grid_dims, flags=None):
    return pltpu.CompilerParams(dimension_semantics=("arbitrary",) * n_grid_dims, vmem_limit_bytes=VMEM_LIMIT,
                                flags=flags)


def _ffn_kernel(x_ref, g_ref, w1_ref, w3_ref, w2_ref, gf_ref, o_ref, *, final_norm):
    x = x_ref[...]
    xn = _rms(x, g_ref[...], NORM_EPS).astype(BF16)
    a = jnp.dot(xn, w1_ref[...], preferred_element_type=F32)
    b = jnp.dot(xn, w3_ref[...], preferred_element_type=F32)
    gated = (_silu(a) * b).astype(BF16)
    h = x + 0.5 * jnp.dot(gated, w2_ref[...], preferred_element_type=F32)
    if final_norm:
        h = _rms(h, gf_ref[...], NORM_EPS)
    o_ref[...] = h


def _ffn(x, g, w1, w3, w2, gf, *, final_norm, tm):
    t = x.shape[0]
    assert t % tm == 0
    return pl.pallas_call(
        functools.partial(_ffn_kernel, final_norm=final_norm),
        grid=(t // tm,),
        in_specs=[
            pl.BlockSpec((tm, D_MODEL), lambda i: (i, 0)),
            _const_spec((1, D_MODEL)),
            _const_spec((D_MODEL, D_FF)),
            _const_spec((D_MODEL, D_FF)),
            _const_spec((D_FF, D_MODEL)),
            _const_spec((1, D_MODEL)),
        ],
        out_specs=pl.BlockSpec((tm, D_MODEL), lambda i: (i, 0)),
        out_shape=jax.ShapeDtypeStruct((t, D_MODEL), F32),
        compiler_params=_params(1),
        name="ffn_final" if final_norm else "ffn",
    )(x, g, w1, w3, w2, gf)


def _rope_kernel(inv_ref, cos_ref, sin_ref, *, pos0, rows_per_pos):
    rows = cos_ref.shape[0]
    step = lax.broadcasted_iota(jnp.int32, (rows, V7X_LANES), 0) // rows_per_pos
    ang = (pos0 + step.astype(F32)) * inv_ref[...]
    cos_ref[...] = jnp.cos(ang)
    sin_ref[...] = jnp.sin(ang)


def _rope_tables(inv, rows, pos0, rows_per_pos):
    return pl.pallas_call(
        functools.partial(_rope_kernel, pos0=pos0, rows_per_pos=rows_per_pos),
        out_shape=[jax.ShapeDtypeStruct((rows, V7X_LANES), F32)] * 2,
        name="rope",
    )(inv)


def _chunk_rows(ref):
    return ref[...].reshape(-1, ref.shape[-1])


def _row_time_seq(shape, s_blk):
    r = lax.broadcasted_iota(jnp.int32, shape, 0)
    return r // s_blk, r % s_blk


def _pair_mask(q, s_blk):
    r = lax.broadcasted_iota(jnp.int32, (q, q), 0)
    c = lax.broadcasted_iota(jnp.int32, (q, q), 1)
    if s_blk == 1:
        return r >= c
    return jnp.logical_and(r // s_blk >= c // s_blk, r % s_blk == c % s_blk)


def _rows_spec(q, s_blk, nblk, nc, width, time_major, *, ahead=0, lead_axes=0, chunks_per_step=1):
    last = nblk * nc - 1

    def chunk_index(args):
        s, c = args[lead_axes], args[lead_axes + 1]
        return jnp.minimum((s * (nc // chunks_per_step) + c) * chunks_per_step + ahead, last)

    if time_major:
        assert nc == 1
        return pl.BlockSpec((q // s_blk, s_blk, width), lambda *a: (0, chunk_index(a), 0))
    return pl.BlockSpec((q, width), lambda *a: (chunk_index(a), 0))


def _project(h_ref, g_ref, w_refs):
    u = _rms(_chunk_rows(h_ref), g_ref[...], NORM_EPS).astype(BF16)
    return [jnp.dot(u, w_ref[...], preferred_element_type=F32) for w_ref in w_refs]


def _ssd_projection(rows_ref, g_ref, w_ref):
    if rows_ref.shape[-1] == SSDP_DIM:
        return _chunk_rows(rows_ref)
    return _project(rows_ref, g_ref, [w_ref])[0]


def _proj_kernel(h_ref, g_ref, w_ref, o_ref):
    o_ref[...] = _project(h_ref, g_ref, [w_ref])[0]


def _ssd_project_rows(h, g, w, *, tm):
    t = h.shape[0]
    assert t % tm == 0
    return pl.pallas_call(
        _proj_kernel,
        grid=(t // tm,),
        in_specs=[
            pl.BlockSpec((tm, D_MODEL), lambda i: (i, 0)),
            _const_spec((1, D_MODEL)),
            _const_spec((D_MODEL, SSDP_DIM)),
        ],
        out_specs=pl.BlockSpec((tm, SSDP_DIM), lambda i: (i, 0)),
        out_shape=jax.ShapeDtypeStruct((t, SSDP_DIM), F32),
        compiler_params=_params(1),
        name="ssd_proj",
    )(h, g, w)


def _conv_pad_rows(s_blk):
    need = (SSM_CONV - 1) * s_blk
    return -(-need // V7X_SUBLANES) * V7X_SUBLANES


def _ssd_kernel(hc_ref, *refs, s_blk, nc, chunks_per_step):
    hn_refs, refs = refs[:chunks_per_step], refs[chunks_per_step:]
    n_in = 11
    y_ref = refs[n_in]
    for k, hn_ref in enumerate(hn_refs):
        if chunks_per_step == 1:
            y_view = y_ref
        else:
            q = y_ref.shape[0] // chunks_per_step
            y_view = y_ref.at[pl.ds(k * q, q)]
        _ssd_chunk(hc_ref, hn_ref, *refs[:n_in], y_view, *refs[n_in + 1:], s_blk=s_blk, nc=nc,
                   chunks_per_step=chunks_per_step, first_of_step=(k == 0))


def _ssd_chunk(hc_ref, hn_ref, gmix_ref, w_ref, h0_ref, conv0_ref, cw_ref, cb_ref, dtb_ref, alog_ref, dexp_ref,
               gn_ref, e_ref, y_ref, hout_ref, convout_ref, xp_scr, zdt_scr, next_scr, *, s_blk, nc, chunks_per_step,
               first_of_step):
    s = pl.program_id(0)
    c = pl.program_id(1)
    pad = _conv_pad_rows(s_blk)
    q = xp_scr.shape[0] - pad
    carry = (SSM_CONV - 1) * s_blk
    steps = q // s_blk
    hg = SSM_HEADS // SSM_GROUPS
    gw = hg * SSM_HEAD_DIM
    h_src = h0_ref if nc == 1 else hout_ref

    if first_of_step:
        @pl.when(jnp.logical_and(s == 0, c == 0))
        def _first():
            next_scr[...] = _ssd_projection(hc_ref, gmix_ref, w_ref)

        @pl.when(c == 0)
        def _init():
            xp_scr[0:pad, :] = jnp.zeros((pad, SSM_CONV_DIM), F32)
            xp_scr[pad - carry:pad, :] = conv0_ref[0]
            if nc > 1:
                hout_ref[...] = h0_ref[...]

    xp_scr[pad:pad + q, :] = next_scr[:, SSDP_XBC:SSDP_DT]
    zdt_scr[:, 0:SSM_D_INNER] = next_scr[:, SSDP_Z:SSDP_XBC]
    zdt_scr[:, SSM_D_INNER:SSM_D_INNER + DT_PAD] = next_scr[:, SSDP_DT:SSDP_DIM]
    next_scr[...] = _ssd_projection(hn_ref, gmix_ref, w_ref)

    z_gate = _silu(zdt_scr[:, 0:SSM_D_INNER])

    xall = xp_scr[...]
    conv = cb_ref[...] + xall[pad:pad + q, :] * cw_ref[SSM_CONV - 1:SSM_CONV, :]
    for j in range(1, SSM_CONV):
        d = j * s_blk
        if d % V7X_SUBLANES == 0:
            back = xall[pad - d:pad - d + q, :]
        else:
            back = pltpu.roll(xall, d, 0)[pad:pad + q, :]
        conv = conv + back * cw_ref[SSM_CONV - 1 - j:SSM_CONV - j, :]
    @pl.when(c == nc // chunks_per_step - 1)
    def _():
        convout_ref[0] = xp_scr[pad + q - carry:pad + q, :]

    if nc > 1:
        xp_scr[0:pad, :] = xp_scr[q:q + pad, :]
    dt_rows = zdt_scr[:, SSM_D_INNER:SSM_D_INNER + DT_PAD]
    xc = _silu(conv)
    xs = xc[:, 0:SSM_D_INNER]
    bm = xc[:, SSM_D_INNER:SSM_D_INNER + SSM_GROUPS * SSM_STATE]
    cm = xc[:, SSM_D_INNER + SSM_GROUPS * SSM_STATE:SSM_CONV_DIM]

    dt = _softplus(dt_rows + dtb_ref[...])
    a = -jnp.exp(alog_ref[...])
    visible = _pair_mask(q, s_blk)
    la = _dot_split(jnp.where(visible, 1.0, 0.0).astype(BF16), dt * (a * LOG2_E), 3)
    lsd = la - jnp.maximum(jnp.log2(dt), LOG_DT_FLOOR)
    eye_r = lax.broadcasted_iota(jnp.int32, (DT_PAD, DT_PAD), 0)
    eye_c = lax.broadcasted_iota(jnp.int32, (DT_PAD, DT_PAD), 1)
    eye = jnp.where(eye_r == eye_c, 1.0, 0.0).astype(BF16)
    la_t = _transpose_split(eye, la, 3)
    lsd_t = _transpose_split(eye, lsd, 3)
    la_last = la[q - s_blk:q, :]
    la_last_rows = la_last if s_blk == 1 else jnp.concatenate([la_last] * steps, axis=0)
    dec_t = jnp.exp2(la_t[:, q - s_blk:q])

    lane = lax.broadcasted_iota(jnp.int32, (q, 2 * SSM_HEAD_DIM), 1)
    first_head = lane < SSM_HEAD_DIM
    y_parts = []
    for g in range(SSM_GROUPS):
        cg = cm[:, g * SSM_STATE:(g + 1) * SSM_STATE]
        bg = bm[:, g * SSM_STATE:(g + 1) * SSM_STATE]
        cbg = _dot_nt(cg, bg)
        for pair in range(hg // 2):
            h0 = g * hg + 2 * pair
            ws = []
            for h in (h0, h0 + 1):
                seg = la[:, h:h + 1] - lsd_t[h:h + 1, :]
                ws.append(cbg * jnp.exp2(jnp.where(visible, seg, -jnp.inf)))
            xpair = xs[:, h0 * SSM_HEAD_DIM:(h0 + 2) * SSM_HEAD_DIM]
            rhs = jnp.concatenate([jnp.where(first_head, xpair, 0.0), jnp.where(first_head, 0.0, xpair)], axis=0)
            y_parts.append(_dot(jnp.concatenate(ws, axis=1), rhs))
    y = jnp.concatenate(y_parts, axis=1)

    _, seq_b = _row_time_seq((q, SSM_STATE), s_blk)
    y_state_parts = []
    for g in range(SSM_GROUPS):
        cg = cm[:, g * SSM_STATE:(g + 1) * SSM_STATE]
        if s_blk == 1:
            c_seqs, h_seqs = cg, h_src[0, g * gw:(g + 1) * gw, :]
        else:
            c_seqs = jnp.concatenate([jnp.where(seq_b == b, cg, 0.0) for b in range(s_blk)], axis=1)
            h_seqs = jnp.concatenate([h_src[b, g * gw:(g + 1) * gw, :].astype(BF16) for b in range(s_blk)], axis=1)
        y_state_parts.append(_dot_nt(c_seqs, h_seqs))
    y_state = jnp.concatenate(y_state_parts, axis=1)

    e = e_ref[...]
    y = y + y_state * _dot_split_lhs(jnp.exp2(la), e, 2) + xs * dexp_ref[...]
    y = y * z_gate
    y = _rms(y, gn_ref[...], GATED_NORM_EPS)
    y_ref[...] = y.reshape(y_ref.shape).astype(y_ref.dtype)

    tail = jnp.exp2(la_last_rows - lsd)
    xt = (xs * _dot_split_lhs(tail, e, 2)).astype(BF16)
    dec =[jnp.broadcast_to(dec_t[:, b:b + 1], (DT_PAD, SSM_STATE)) for b in range(s_blk)]
    for g in range(SSM_GROUPS):
        bg = bm[:, g * SSM_STATE:(g + 1) * SSM_STATE]
        bg_seqs = bg if s_blk == 1 else jnp.concatenate([jnp.where(seq_b == b, bg, 0.0) for b in range(s_blk)], axis=1)
        upd = _dot_tn(xt[:, g * gw:(g + 1) * gw], bg_seqs)
        for b in range(s_blk):
            for hh in range(hg):
                h = g * hg + hh
                r0 = h * SSM_HEAD_DIM
                hout_ref[b, r0:r0 + SSM_HEAD_DIM, :] = (
                    h_src[b, r0:r0 + SSM_HEAD_DIM, :] * dec[b][h:h + 1, :]
                    + upd[hh * SSM_HEAD_DIM:(hh + 1) * SSM_HEAD_DIM, b * SSM_STATE:(b + 1) * SSM_STATE])


def _ssd(h, gmix, w, h0, conv0, cw, cb, dtb, alog, dexp, gn, e, *, nblk, nc, q, s_blk, time_major):
    rows = SSM_HEADS * SSM_HEAD_DIM
    carry = (SSM_CONV - 1) * s_blk
    per_block0 = h0.shape[0] == nblk * s_blk
    assert per_block0 or (h0.shape[0] == s_blk and conv0.shape[0] == 1)
    cps = SSD_CHUNKS_PER_STEP if (nc % SSD_CHUNKS_PER_STEP == 0 and not time_major) else 1
    ncs = nc // cps
    spec = functools.partial(_rows_spec, q, s_blk, nblk, nc, time_major=time_major, chunks_per_step=cps)
    if time_major:
        y_spec = spec(SSM_D_INNER)
        y_shape = (q // s_blk, nblk * s_blk, SSM_D_INNER)
    else:
        y_spec = pl.BlockSpec((cps * q, SSM_D_INNER), lambda s, c: (s * ncs + c, 0))
        y_shape = (nblk * nc * q, SSM_D_INNER)
    return pl.pallas_call(
        functools.partial(_ssd_kernel, s_blk=s_blk, nc=nc, chunks_per_step=cps),
        grid=(nblk, ncs),
        in_specs=[spec(h.shape[-1], ahead=k) for k in range(cps + 1)] + [
            _const_spec((1, D_MODEL)),
            _const_spec((D_MODEL, SSDP_DIM) if h.shape[-1] == D_MODEL else (V7X_SUBLANES, V7X_LANES)),
            pl.BlockSpec((s_blk, rows, SSM_STATE), lambda s, c: (s if per_block0 else 0, 0, 0)),
            pl.BlockSpec((1, carry, SSM_CONV_DIM), lambda s, c: (s if per_block0 else 0, 0, 0)),
            _const_spec((SSM_CONV, SSM_CONV_DIM)),
            _const_spec((1, SSM_CONV_DIM)),
            _const_spec((1, DT_PAD)),
            _const_spec((1, DT_PAD)),
            _const_spec((1, SSM_D_INNER)),
            _const_spec((1, SSM_D_INNER)),
            _const_spec((DT_PAD, SSM_D_INNER)),
        ],
        out_specs=[
            y_spec,
            pl.BlockSpec((s_blk, rows, SSM_STATE), lambda s, c: (s, 0, 0)),
            pl.BlockSpec((1, carry, SSM_CONV_DIM), lambda s, c: (s, 0, 0)),
        ],
        out_shape=[
            jax.ShapeDtypeStruct(y_shape, BF16),
            jax.ShapeDtypeStruct((nblk * s_blk, rows, SSM_STATE), F32),
            jax.ShapeDtypeStruct((nblk, carry, SSM_CONV_DIM), F32),
        ],
        scratch_shapes=[
            pltpu.VMEM((_conv_pad_rows(s_blk) + q, SSM_CONV_DIM), F32),
            pltpu.VMEM((q, SSM_D_INNER + DT_PAD), F32),
            pltpu.VMEM((q, SSDP_DIM), F32),
        ],
        compiler_params=_params(2),
        name="ssd",
    )(*([h] * (cps + 1)), gmix, w, h0, conv0, cw, cb, dtb, alog, dexp, gn, e)


def _ret_log_decay(h):
    return float(np.log1p(-np.exp2(-5.0 - h)))


def _ret_project(h_ref, gmix_ref, w_refs, dst):
    pq, pk, pv, pg = _project(h_ref, gmix_ref, w_refs)
    dst[0][...] = pq
    dst[1][...] = pk
    dst[2][...] = pv.astype(BF16)
    dst[3][...] = pg


def _ret_kernel(*refs, s_blk, nc, n_heads, chunks_per_step):
    io_refs, sets = refs[:-8], (refs[-8:-4], refs[-4:])
    hc_ref, hn_refs, io_refs = io_refs[0], io_refs[1:1 + chunks_per_step], io_refs[1 + chunks_per_step:]
    gmix_ref = io_refs[0]
    w_refs = io_refs[1:5]
    cos_ref, sin_ref = io_refs[5:7]
    tables = io_refs[7:10]
    s0_ref, gn_ref, y_ref, sout_ref = io_refs[10:14]
    s = pl.program_id(1)
    c = pl.program_id(2)
    q = cos_ref.shape[0] // chunks_per_step

    @pl.when(jnp.logical_and(s == 0, c == 0))
    def _first():
        _ret_project(hc_ref, gmix_ref, w_refs, sets[0])

    if nc > 1:
        @pl.when(c == 0)
        def _init():
            sout_ref[...] = s0_ref[...]

    def chunk(k, hn_ref, cur_set, next_set):
        rows = pl.ds(k * q, q)
        y_view = y_ref if chunks_per_step == 1 else y_ref.at[rows]
        _ret_step(hn_ref, gmix_ref, *w_refs, cos_ref.at[rows], sin_ref.at[rows], *tables, s0_ref, gn_ref, y_view,
                  sout_ref, cur_set, next_set, s_blk=s_blk, nc=nc, n_heads=n_heads)

    if chunks_per_step % 2 == 0:
        for k, hn_ref in enumerate(hn_refs):
            chunk(k, hn_ref, sets[k % 2], sets[(k + 1) % 2])
    else:
        assert chunks_per_step == 1
        parity = (s * nc + c) % 2

        @pl.when(parity == 0)
        def _even():
            chunk(0, hn_refs[0], sets[0], sets[1])

        @pl.when(parity == 1)
        def _odd():
            chunk(0, hn_refs[0], sets[1], sets[0])


def _ret_step(hn_ref, gmix_ref, wq_ref, wk_ref, wv_ref, wg_ref, cos_ref, sin_ref, dmat_ref, qdec_ref, kdec_ref,
              s0_ref, gn_ref, y_ref, sout_ref, cur_set, next_set, *, s_blk, nc, n_heads):
    q_scr, k_scr, v_scr, g_scr = cur_set
    hb = pl.program_id(0)
    q = cos_ref.shape[0]
    steps = q // s_blk
    half = RET_QK_DIM // 2
    s_src = s0_ref if nc == 1 else sout_ref

    _ret_project(hn_ref, gmix_ref, [wq_ref, wk_ref, wv_ref, wg_ref], next_set)

    qq = q_scr[...]
    kk = k_scr[...]
    vv = v_scr[...]
    rg = g_scr[...]
    cos = cos_ref[...]
    sin = sin_ref[...]
    _, seq_qk = _row_time_seq((q, RET_QK_DIM), s_blk)

    def rot(t, h):
        t1 = t[:, h * RET_QK_DIM:h * RET_QK_DIM + half]
        t2 = t[:, h * RET_QK_DIM + half:(h + 1) * RET_QK_DIM]
        return jnp.concatenate([t1 * cos - t2 * sin, t1 * sin + t2 * cos], axis=1)

    y_parts = []
    for h in range(n_heads):
        if n_heads == RET_HEADS:
            chunk_dec = math.exp(steps * _ret_log_decay(h))
        else:
            head = hb * n_heads + h
            chunk_dec = jnp.float32(math.exp(steps * _ret_log_decay(0)))
            for hh in range(1, RET_HEADS):
                chunk_dec = jnp.where(head == hh, jnp.float32(math.exp(steps * _ret_log_decay(hh))), chunk_dec)
        dmat = dmat_ref[h]
        q_dec = jnp.concatenate([qdec_ref[h]] * 2, axis=1)
        k_dec = jnp.concatenate([kdec_ref[h]] * 2, axis=1)
        qr = rot(qq, h)
        kr = rot(kk, h) * (RET_QK_DIM ** -0.5)
        vh = vv[:, h * RET_V_DIM:(h + 1) * RET_V_DIM]
        scores = _dot_nt(qr, kr) * dmat
        y = _dot(scores, vh)
        qd = qr * q_dec
        kd = kr * k_dec
        r0 = h * RET_QK_DIM
        for b in range(s_blk):
            qd_b = (qd if s_blk == 1 else jnp.where(seq_qk == b, qd, 0.0)).astype(BF16)
            kd_b = (kd if s_blk == 1 else jnp.where(seq_qk == b, kd, 0.0)).astype(BF16)
            sh = s_src[b, r0:r0 + RET_QK_DIM, :]
            y = y + _dot(qd_b, sh)
            sout_ref[b, r0:r0 + RET_QK_DIM, :] = sh * chunk_dec + _dot_tn(kd_b, vh)
        g = gn_ref[:, h * RET_V_DIM:(h + 1) * RET_V_DIM]
        y = _rms(y, g, NORM_EPS) * _silu(rg[:, h * RET_V_DIM:(h + 1) * RET_V_DIM])
        y_parts.append(y)
    y = y_parts[0] if n_heads == 1 else jnp.concatenate(y_parts, axis=1)
    y_ref[...] = y.reshape(y_ref.shape).astype(y_ref.dtype)


def _ret_decay_tables(q, s_blk):
    t = np.arange(q) // s_blk
    seq = np.arange(q) % s_blk
    steps = q // s_blk
    rel = t[:, None] - t[None, :]
    visible = (rel >= 0) & (seq[:, None] == seq[None, :])
    dmat, qdec, kdec = [], [], []
    for h in range(RET_HEADS):
        lg = _ret_log_decay(h)
        dmat.append(np.where(visible, np.exp(np.maximum(rel, 0) * lg), 0.0))
        qdec.append(np.broadcast_to(np.exp((t + 1.0) * lg)[:, None], (q, V7X_LANES)))
        kdec.append(np.broadcast_to(np.exp((steps - 1.0 - t) * lg)[:, None], (q, V7X_LANES)))
    return tuple(jnp.asarray(np.stack(a), F32) for a in (dmat, qdec, kdec))


def _ret(h, gmix, wq, wk, wv, wg, cos, sin, s0, gn, *, nblk, nc, q, s_blk, n_heads, time_major):
    dmat, qdec, kdec = _ret_decay_tables(q, s_blk)
    nhb = RET_HEADS // n_heads
    qk_w = n_heads * RET_QK_DIM
    v_w = n_heads * RET_V_DIM
    per_block0 = s0.shape[0] == nblk * s_blk
    assert per_block0 or s0.shape[0] == s_blk
    cps = RET_CHUNKS_PER_STEP if (nc % RET_CHUNKS_PER_STEP == 0 and not time_major) else 1
    ncs = nc // cps
    rows = functools.partial(_rows_spec, q, s_blk, nblk, nc, D_MODEL, time_major, lead_axes=1, chunks_per_step=cps)

    def head_cols(nrows, width):
        if nhb == 1:
            return _const_spec((nrows, width))
        return pl.BlockSpec((nrows, width), lambda hb, s, c: (0, hb))

    if time_major:
        y_spec = pl.BlockSpec((q // s_blk, s_blk, v_w), lambda hb, s, c: (0, s, hb))
        y_shape = (q // s_blk, nblk * s_blk, RET_V)
    else:
        y_spec = pl.BlockSpec((cps * q, v_w), lambda hb, s, c: (s * ncs + c, hb))
        y_shape = (nblk * nc * q, RET_V)
    return pl.pallas_call(
        functools.partial(_ret_kernel, s_blk=s_blk, nc=nc, n_heads=n_heads, chunks_per_step=cps),
        grid=(nhb, nblk, ncs),
        in_specs=[rows(ahead=k) for k in range(cps + 1)] + [
            _const_spec((1, D_MODEL)),
            head_cols(D_MODEL, qk_w),
            head_cols(D_MODEL, qk_w),
            head_cols(D_MODEL, v_w),
            head_cols(D_MODEL, v_w),
            pl.BlockSpec((cps * q, V7X_LANES), lambda hb, s, c: (c, 0)),
            pl.BlockSpec((cps * q, V7X_LANES), lambda hb, s, c: (c, 0)),
            pl.BlockSpec((n_heads, q, q), lambda hb, s, c: (hb, 0, 0)),
            pl.BlockSpec((n_heads, q, V7X_LANES), lambda hb, s, c: (hb, 0, 0)),
            pl.BlockSpec((n_heads, q, V7X_LANES), lambda hb, s, c: (hb, 0, 0)),
            pl.BlockSpec((s_blk, qk_w, RET_V_DIM), lambda hb, s, c: (s if per_block0 else 0, hb, 0)),
            head_cols(1, v_w),
        ],
        out_specs=[
            y_spec,
            pl.BlockSpec((s_blk, qk_w, RET_V_DIM), lambda hb, s, c: (s, hb, 0)),
        ],
        out_shape=[
            jax.ShapeDtypeStruct(y_shape, BF16),
            jax.ShapeDtypeStruct((nblk * s_blk, RET_HEADS * RET_QK_DIM, RET_V_DIM), F32),
        ],
        scratch_shapes=[
            pltpu.VMEM((q, qk_w), F32),
            pltpu.VMEM((q, qk_w), F32),
            pltpu.VMEM((q, v_w), BF16),
            pltpu.VMEM((q, v_w), F32),
        ] * 2,
        compiler_params=_params(3),
        name="ret",
    )(*([h] * (cps + 1)), gmix, wq, wk, wv, wg, cos, sin, dmat, qdec, kdec, s0, gn)


def _merge_kernel(h_ref, ys_ref, yr_ref, g_ref, wg_ref, wbs_ref, wbr_ref, wo_ref, o_ref):
    h = h_ref[...]
    u = _rms(h, g_ref[...], NORM_EPS).astype(BF16)
    gates = jnp.dot(u, wg_ref[...], preferred_element_type=F32)
    branch_ssm = jnp.dot(ys_ref[...], wbs_ref[...], preferred_element_type=F32)
    branch_ret = jnp.dot(yr_ref[...], wbr_ref[...], preferred_element_type=F32)
    merged = _sigmoid(gates[:, 0:D_MODEL]) * branch_ssm + _sigmoid(gates[:, D_MODEL:2 * D_MODEL]) * branch_ret
    o_ref[...] = h + jnp.dot(merged.astype(BF16), wo_ref[...], preferred_element_type=F32)


def _merge(h, ys, yr, g, wg, wbs, wbr, wo, *, tm):
    t = h.shape[0]
    assert t % tm == 0
    return pl.pallas_call(
        _merge_kernel,
        grid=(t // tm,),
        in_specs=[
            pl.BlockSpec((tm, D_MODEL), lambda i: (i, 0)),
            pl.BlockSpec((tm, SSM_D_INNER), lambda i: (i, 0)),
            pl.BlockSpec((tm, RET_V), lambda i: (i, 0)),
            _const_spec((1, D_MODEL)),
            _const_spec((D_MODEL, 2 * D_MODEL)),
            _const_spec((SSM_D_INNER, D_MODEL)),
            _const_spec((RET_V, D_MODEL)),
            _const_spec((D_MODEL, D_MODEL)),
        ],
        out_specs=pl.BlockSpec((tm, D_MODEL), lambda i: (i, 0)),
        out_shape=jax.ShapeDtypeStruct((t, D_MODEL), F32),
        compiler_params=_params(1),
        name="merge",
    )(h, ys, yr, g, wg, wbs, wbr, wo)


FFN_TILE = 512
SAMPLE_SEQS_PER_STEP = V7X_SUBLANES
SSD_CHUNKS_PER_STEP = 4
RET_CHUNKS_PER_STEP = 4
SAMPLE_RET_HEADS_PER_STEP = 2


def _layer(x2, pos0, ssm0, conv0, ret0, w, final_gain, *, nblk, nc, q, s_blk, ret_heads, time_major):
    t = x2.shape[0]
    tm_ffn = min(FFN_TILE, t)
    steps = q // s_blk
    nseq = nblk * s_blk

    h = _ffn(x2, w["norm_ffn1"], w["ffn1_w1"], w["ffn1_w3"], w["ffn1_w2"], final_gain, final_norm=False, tm=tm_ffn)
    h_rows = h.reshape(steps, nseq, D_MODEL) if time_major else h

    if q < CHUNK:
        ssd_rows = _ssd_project_rows(h, w["norm_mix"], w["w_ssd"], tm=tm_ffn)
        ssd_rows = ssd_rows.reshape(steps, nseq, SSDP_DIM) if time_major else ssd_rows
    else:
        ssd_rows = h_rows
    y_ssm, ssm_new, conv_new = _ssd(
        ssd_rows, w["norm_mix"], w["w_ssd"], ssm0, conv0, w["conv_w"], w["conv_b"], w["dt_bias"], w["a_log"],
        w["d_exp"], w["ssm_norm"], w["head_expand"], nblk=nblk, nc=nc, q=q, s_blk=s_blk, time_major=time_major)
    cos, sin = _rope_tables(w["rope_inv"], nc * q, pos0, s_blk)
    y_ret, ret_new = _ret(h_rows, w["norm_mix"], w["w_q"], w["w_k"], w["w_v"], w["w_rg"], cos, sin, ret0,
                          w["ret_norm"], nblk=nblk, nc=nc, q=q, s_blk=s_blk, n_heads=ret_heads,
                          time_major=time_major)

    h = _merge(h, y_ssm.reshape(t, SSM_D_INNER), y_ret.reshape(t, RET_V), w["norm_mix"], w["w_gates"],
               w["w_branch_ssm"], w["w_branch_ret"], w["w_out"], tm=tm_ffn)
    y = _ffn(h, w["norm_ffn2"], w["ffn2_w1"], w["ffn2_w3"], w["ffn2_w2"], final_gain, final_norm=True, tm=tm_ffn)
    return y, ssm_new, conv_new, ret_new


def _prompt_layer(x, w, final_gain):
    nseq, seq_len, _ = x.shape
    q = math.gcd(seq_len, CHUNK)
    y, ssm_new, conv_new, ret_new = _layer(
        x.reshape(nseq * seq_len, D_MODEL), 0.0,
        jnp.zeros((1, SSM_HEADS * SSM_HEAD_DIM, SSM_STATE), F32),
        jnp.zeros((1, SSM_CONV - 1, SSM_CONV_DIM), F32),
        jnp.zeros((1, RET_HEADS * RET_QK_DIM, RET_V_DIM), F32),
        w, final_gain, nblk=nseq, nc=seq_len // q, q=q, s_blk=1, ret_heads=RET_HEADS, time_major=False)
    return (y.reshape(nseq, seq_len, D_MODEL), ssm_new.reshape(nseq, SSM_HEADS, SSM_HEAD_DIM, SSM_STATE), conv_new,
            ret_new.reshape(nseq, RET_HEADS, RET_QK_DIM, RET_V_DIM))


def _sample_layer(x, ssm0, conv0, ret0, w, final_gain):
    nseq, seq_len, _ = x.shape
    s_blk = SAMPLE_SEQS_PER_STEP
    assert nseq % s_blk == 0 and seq_len <= CHUNK
    nblk = nseq // s_blk
    kc = SSM_CONV - 1
    x_tm = jnp.transpose(x, (1, 0, 2)).reshape(seq_len * nseq, D_MODEL)
    conv0_tm = jnp.transpose(conv0.reshape(nblk, s_blk, kc, SSM_CONV_DIM), (0, 2, 1, 3)).reshape(
        nblk, kc * s_blk, SSM_CONV_DIM)
    y, ssm_new, conv_new, ret_new = _layer(
        x_tm, float(PAST_LEN), ssm0.reshape(nseq, SSM_HEADS * SSM_HEAD_DIM, SSM_STATE), conv0_tm,
        ret0.reshape(nseq, RET_HEADS * RET_QK_DIM, RET_V_DIM), w, final_gain,
        nblk=nblk, nc=1, q=s_blk * seq_len, s_blk=s_blk, ret_heads=SAMPLE_RET_HEADS_PER_STEP, time_major=True)
    conv_new = jnp.transpose(conv_new.reshape(nblk, kc, s_blk, SSM_CONV_DIM), (0, 2, 1, 3)).reshape(
        nseq, kc, SSM_CONV_DIM)
    return (jnp.transpose(y.reshape(seq_len, nseq, D_MODEL), (1, 0, 2)),
            ssm_new.reshape(nseq, SSM_HEADS, SSM_HEAD_DIM, SSM_STATE), conv_new,
            ret_new.reshape(nseq, RET_HEADS, RET_QK_DIM, RET_V_DIM))


def _prep_weights(norm_ffn1, ffn1_w1, ffn1_w3, ffn1_w2, norm_mix, w_in, conv_w, conv_b, dt_bias, a_log, ssm_d,
                  ssm_norm, ret_norm, w_branch_ssm, w_branch_ret, w_out, norm_ffn2, ffn2_w1, ffn2_w3, ffn2_w2):
    o_z = 0
    o_xbc = o_z + SSM_D_INNER
    o_dt = o_xbc + SSM_CONV_DIM
    o_q = o_dt + SSM_HEADS
    o_k = o_q + RET_QK
    o_v = o_k + RET_QK
    o_rg = o_v + RET_V
    o_ga = o_rg + RET_V
    o_end = o_ga + 2 * D_MODEL
    assert o_end == w_in.shape[1]

    w_in_bf = w_in.astype(BF16)

    def cols(a, b):
        return w_in_bf[:, a:b]

    assert (o_z, o_xbc, o_dt) == (SSDP_Z, SSDP_XBC, SSDP_DT) and SSDP_DIM <= o_end
    half = RET_QK_DIM // 2
    head_of_channel = jnp.arange(SSM_D_INNER, dtype=jnp.int32) // SSM_HEAD_DIM
    return {
        "norm_ffn1": norm_ffn1.reshape(1, -1), "ffn1_w1": ffn1_w1.astype(BF16), "ffn1_w3": ffn1_w3.astype(BF16),
        "ffn1_w2": ffn1_w2.astype(BF16),
        "norm_mix": norm_mix.reshape(1, -1), "w_ssd": w_in_bf,
        "w_q": cols(o_q, o_k), "w_k": cols(o_k, o_v), "w_v": cols(o_v, o_rg), "w_rg": cols(o_rg, o_ga),
        "w_gates": cols(o_ga, o_end),
        "conv_w": conv_w, "conv_b": conv_b.reshape(1, -1),
        "dt_bias": jnp.pad(dt_bias, (0, DT_PAD - SSM_HEADS)).reshape(1, -1),
        "a_log": jnp.pad(a_log, (0, DT_PAD - SSM_HEADS)).reshape(1, -1),
        "d_exp": jnp.repeat(ssm_d, SSM_HEAD_DIM).reshape(1, -1),
        "ssm_norm": ssm_norm.reshape(1, -1), "ret_norm": ret_norm.reshape(1, -1),
        "head_expand": (head_of_channel[None, :] == jnp.arange(DT_PAD, dtype=jnp.int32)[:, None]).astype(BF16),
        "rope_inv": (ROPE_BASE ** (-jnp.arange(half, dtype=F32) / half)).reshape(1, half),
        "w_branch_ssm": w_branch_ssm.astype(BF16), "w_branch_ret": w_branch_ret.astype(BF16),
        "w_out": w_out.astype(BF16),
        "norm_ffn2": norm_ffn2.reshape(1, -1), "ffn2_w1": ffn2_w1.astype(BF16), "ffn2_w3": ffn2_w3.astype(BF16),
        "ffn2_w2": ffn2_w2.astype(BF16),
    }


def kernel(x_prompt, x_sample, state_ssm, state_conv, state_ret, norm_ffn1, ffn1_w1, ffn1_w3, ffn1_w2, norm_mix, w_in,
           conv_w, conv_b, dt_bias, a_log, ssm_d, ssm_norm, ret_norm, w_branch_ssm, w_branch_ret, w_out, norm_ffn2,
           ffn2_w1, ffn2_w3, ffn2_w2, norm_final):
    depth = norm_ffn1.shape[0]
    assert depth == 1, "the final RMSNorm is fused into the last layer's second FFN"
    w = _prep_weights(norm_ffn1[0], ffn1_w1[0], ffn1_w3[0], ffn1_w2[0], norm_mix[0], w_in[0], conv_w[0], conv_b[0],
                      dt_bias[0], a_log[0], ssm_d[0], ssm_norm[0], ret_norm[0], w_branch_ssm[0], w_branch_ret[0],
                      w_out[0], norm_ffn2[0], ffn2_w1[0], ffn2_w3[0], ffn2_w2[0])
    final_gain = norm_final.reshape(1, -1)
    yp, ssm_p, conv_p, ret_p = _prompt_layer(x_prompt, w, final_gain)
    ys, ssm_s, conv_s, ret_s = _sample_layer(x_sample, state_ssm[0], state_conv[0], state_ret[0], w, final_gain)
    return (yp, ys, ssm_p[None], conv_p[None], ret_p[None], ssm_s[None], conv_s[None], ret_s[None])
```

```python
import functools
import math

import jax
import jax.numpy as jnp
import numpy as np
from jax import lax
from jax.experimental import pallas as pl
from jax.experimental.pallas import tpu as pltpu

F32 = jnp.float32
BF16 = jnp.bfloat16

D_MODEL = 1024
D_FF = 2816
SSM_D_INNER = 2048
SSM_HEAD_DIM = 64
SSM_HEADS = 32
SSM_GROUPS = 4
SSM_STATE = 128
SSM_CONV = 4
SSM_BC = 2 * SSM_GROUPS * SSM_STATE
SSM_CONV_DIM = SSM_D_INNER + SSM_BC
RET_HEADS = 4
RET_QK_DIM = 256
RET_V_DIM = 512
RET_QK = RET_HEADS * RET_QK_DIM
RET_V = RET_HEADS * RET_V_DIM
ROPE_BASE = 10000.0
PAST_LEN = 16384
CHUNK = 128
NORM_EPS = 1e-6
GATED_NORM_EPS = 1e-5

V7X_SUBLANES = 8
V7X_LANES = 128
V7X_VMEM_BYTES = 64 * 1024 * 1024
VMEM_LIMIT = V7X_VMEM_BYTES - 8 * 1024 * 1024

DT_PAD = V7X_LANES
LOG_DT_FLOOR = -1e30
LOG2_E = math.log2(math.e)

SSDP_Z = 0
SSDP_XBC = SSM_D_INNER
SSDP_DT = SSM_D_INNER + SSM_CONV_DIM
SSDP_DIM = SSDP_DT + DT_PAD


def _rms(x, g, eps):
    return x * lax.rsqrt(jnp.mean(x * x, axis=-1, keepdims=True) + eps) * g


def _sigmoid(x):
    return 0.5 + 0.5 * jnp.tanh(0.5 * x)


def _silu(x):
    half = 0.5 * x
    return half + half * jnp.tanh(half)


def _softplus(x):
    return jnp.maximum(x, 0.0) + jnp.log1p(jnp.exp(-jnp.abs(x)))


def _dot(a, b):
    return jnp.dot(a.astype(BF16), b.astype(BF16), preferred_element_type=F32)


def _dot_nt(a, b):
    return lax.dot_general(a.astype(BF16), b.astype(BF16), (((1,), (1,)), ((), ())), preferred_element_type=F32)


def _dot_tn(a, b):
    return lax.dot_general(a.astype(BF16), b.astype(BF16), (((0,), (0,)), ((), ())), preferred_element_type=F32)


def _dot_split(a, b_f32, passes):
    acc = None
    rem = b_f32
    for _ in range(passes):
        piece = rem.astype(BF16)
        term = jnp.dot(a, piece, preferred_element_type=F32)
        acc = term if acc is None else acc + term
        rem = rem - piece.astype(F32)
    return acc


def _dot_split_lhs(a_f32, b, passes):
    acc = None
    rem = a_f32
    for _ in range(passes):
        piece = rem.astype(BF16)
        term = jnp.dot(piece, b, preferred_element_type=F32)
        acc = term if acc is None else acc + term
        rem = rem - piece.astype(F32)
    return acc


def _transpose_split(eye, x_f32, passes):
    acc = None
    rem = x_f32
    for _ in range(passes):
        piece = rem.astype(BF16)
        term = lax.dot_general(eye, piece, (((1,), (1,)), ((), ())), preferred_element_type=F32)
        acc = term if acc is None else acc + term
        rem = rem - piece.astype(F32)
    return acc


def _const_spec(shape):
    nd = len(shape)
    return pl.BlockSpec(shape, lambda *_: (0,) * nd, pipeline_mode=pl.Buffered(1))


def _params(n_grid_dims):
    return pltpu.CompilerParams(dimension_semantics=("arbitrary",) * n_grid_dims, vmem_limit_bytes=VMEM_LIMIT)


def _ffn_kernel(x_ref, g_ref, w1_ref, w3_ref, w2_ref, gf_ref, o_ref, *, final_norm):
    x = x_ref[...]
    xn = _rms(x, g_ref[...], NORM_EPS).astype(BF16)
    a = jnp.dot(xn, w1_ref[...], preferred_element_type=F32)
    b = jnp.dot(xn, w3_ref[...], preferred_element_type=F32)
    gated = (_silu(a) * b).astype(BF16)
    h = x + 0.5 * jnp.dot(gated, w2_ref[...], preferred_element_type=F32)
    if final_norm:
        h = _rms(h, gf_ref[...], NORM_EPS)
    o_ref[...] = h


def _ffn(x, g, w1, w3, w2, gf, *, final_norm, tm):
    t = x.shape[0]
    assert t % tm == 0
    return pl.pallas_call(
        functools.partial(_ffn_kernel, final_norm=final_norm),
        grid=(t // tm,),
        in_specs=[
            pl.BlockSpec((tm, D_MODEL), lambda i: (i, 0)),
            _const_spec((1, D_MODEL)),
            _const_spec((D_MODEL, D_FF)),
            _const_spec((D_MODEL, D_FF)),
            _const_spec((D_FF, D_MODEL)),
            _const_spec((1, D_MODEL)),
        ],
        out_specs=pl.BlockSpec((tm, D_MODEL), lambda i: (i, 0)),
        out_shape=jax.ShapeDtypeStruct((t, D_MODEL), F32),
        compiler_params=_params(1),
        name="ffn_final" if final_norm else "ffn",
    )(x, g, w1, w3, w2, gf)


def _rope_kernel(inv_ref, cos_ref, sin_ref, *, pos0, rows_per_pos):
    rows = cos_ref.shape[0]
    step = lax.broadcasted_iota(jnp.int32, (rows, V7X_LANES), 0) // rows_per_pos
    ang = (pos0 + step.astype(F32)) * inv_ref[...]
    cos_ref[...] = jnp.cos(ang)
    sin_ref[...] = jnp.sin(ang)


def _rope_tables(inv, rows, pos0, rows_per_pos):
    return pl.pallas_call(
        functools.partial(_rope_kernel, pos0=pos0, rows_per_pos=rows_per_pos),
        out_shape=[jax.ShapeDtypeStruct((rows, V7X_LANES), F32)] * 2,
        name="rope",
    )(inv)


def _chunk_rows(ref):
    return ref[...].reshape(-1, ref.shape[-1])


def _row_time_seq(shape, s_blk):
    r = lax.broadcasted_iota(jnp.int32, shape, 0)
    return r // s_blk, r % s_blk


def _pair_mask(q, s_blk):
    r = lax.broadcasted_iota(jnp.int32, (q, q), 0)
    c = lax.broadcasted_iota(jnp.int32, (q, q), 1)
    if s_blk == 1:
        return r >= c
    return jnp.logical_and(r // s_blk >= c // s_blk, r % s_blk == c % s_blk)


def _rows_spec(q, s_blk, nblk, nc, width, time_major, *, ahead=0, lead_axes=0, chunks_per_step=1):
    last = nblk * nc - 1

    def chunk_index(args):
        s, c = args[lead_axes], args[lead_axes + 1]
        return jnp.minimum((s * (nc // chunks_per_step) + c) * chunks_per_step + ahead, last)

    if time_major:
        assert nc == 1
        return pl.BlockSpec((q // s_blk, s_blk, width), lambda *a: (0, chunk_index(a), 0))
    return pl.BlockSpec((q, width), lambda *a: (chunk_index(a), 0))


def _project(h_ref, g_ref, w_refs):
    u = _rms(_chunk_rows(h_ref), g_ref[...], NORM_EPS).astype(BF16)
    return [jnp.dot(u, w_ref[...], preferred_element_type=F32) for w_ref in w_refs]


def _conv_pad_rows(s_blk):
    need = (SSM_CONV - 1) * s_blk
    return -(-need // V7X_SUBLANES) * V7X_SUBLANES


def _ssd_kernel(hc_ref, *refs, s_blk, nc, chunks_per_step):
    hn_refs, refs = refs[:chunks_per_step], refs[chunks_per_step:]
    n_in = 11
    y_ref = refs[n_in]
    for k, hn_ref in enumerate(hn_refs):
        if chunks_per_step == 1:
            y_view = y_ref
        else:
            q = y_ref.shape[0] // chunks_per_step
            y_view = y_ref.at[pl.ds(k * q, q)]
        _ssd_chunk(hc_ref, hn_ref, *refs[:n_in], y_view, *refs[n_in + 1:], s_blk=s_blk, nc=nc,
                   chunks_per_step=chunks_per_step, first_of_step=(k == 0))


def _ssd_chunk(hc_ref, hn_ref, gmix_ref, w_ref, h0_ref, conv0_ref, cw_ref, cb_ref, dtb_ref, alog_ref, dexp_ref,
               gn_ref, e_ref, y_ref, hout_ref, convout_ref, xp_scr, zdt_scr, next_scr, *, s_blk, nc, chunks_per_step,
               first_of_step):
    s = pl.program_id(0)
    c = pl.program_id(1)
    pad = _conv_pad_rows(s_blk)
    q = xp_scr.shape[0] - pad
    carry = (SSM_CONV - 1) * s_blk
    steps = q // s_blk
    hg = SSM_HEADS // SSM_GROUPS
    gw = hg * SSM_HEAD_DIM
    h_src = h0_ref if nc == 1 else hout_ref

    if first_of_step:
        @pl.when(jnp.logical_and(s == 0, c == 0))
        def _first():
            next_scr[...] = _project(hc_ref, gmix_ref, [w_ref])[0]

        @pl.when(c == 0)
        def _init():
            xp_scr[0:pad, :] = jnp.zeros((pad, SSM_CONV_DIM), F32)
            xp_scr[pad - carry:pad, :] = conv0_ref[0]
            if nc > 1:
                hout_ref[...] = h0_ref[...]

    xp_scr[pad:pad + q, :] = next_scr[:, SSDP_XBC:SSDP_DT]
    zdt_scr[:, 0:SSM_D_INNER] = next_scr[:, SSDP_Z:SSDP_XBC]
    zdt_scr[:, SSM_D_INNER:SSM_D_INNER + DT_PAD] = next_scr[:, SSDP_DT:SSDP_DIM]
    next_scr[...] = _project(hn_ref, gmix_ref, [w_ref])[0]

    z_gate = _silu(zdt_scr[:, 0:SSM_D_INNER])

    xall = xp_scr[...]
    conv = cb_ref[...] + xall[pad:pad + q, :] * cw_ref[SSM_CONV - 1:SSM_CONV, :]
    for j in range(1, SSM_CONV):
        d = j * s_blk
        if d % V7X_SUBLANES == 0:
            back = xall[pad - d:pad - d + q, :]
        else:
            back = pltpu.roll(xall, d, 0)[pad:pad + q, :]
        conv = conv + back * cw_ref[SSM_CONV - 1 - j:SSM_CONV - j, :]
    @pl.when(c == nc // chunks_per_step - 1)
    def _():
        convout_ref[0] = xp_scr[pad + q - carry:pad + q, :]

    if nc > 1:
        xp_scr[0:pad, :] = xp_scr[q:q + pad, :]
    dt_rows = zdt_scr[:, SSM_D_INNER:SSM_D_INNER + DT_PAD]
    xc = _silu(conv)
    xs = xc[:, 0:SSM_D_INNER]
    bm = xc[:, SSM_D_INNER:SSM_D_INNER + SSM_GROUPS * SSM_STATE]
    cm = xc[:, SSM_D_INNER + SSM_GROUPS * SSM_STATE:SSM_CONV_DIM]

    dt = _softplus(dt_rows + dtb_ref[...])
    a = -jnp.exp(alog_ref[...])
    visible = _pair_mask(q, s_blk)
    la = _dot_split(jnp.where(visible, 1.0, 0.0).astype(BF16), dt * (a * LOG2_E), 3)
    lsd = la - jnp.maximum(jnp.log2(dt), LOG_DT_FLOOR)
    eye_r = lax.broadcasted_iota(jnp.int32, (DT_PAD, DT_PAD), 0)
    eye_c = lax.broadcasted_iota(jnp.int32, (DT_PAD, DT_PAD), 1)
    eye = jnp.where(eye_r == eye_c, 1.0, 0.0).astype(BF16)
    la_t = _transpose_split(eye, la, 3)
    lsd_t = _transpose_split(eye, lsd, 3)
    la_last = la[q - s_blk:q, :]
    la_last_rows = la_last if s_blk == 1 else jnp.concatenate([la_last] * steps, axis=0)
    dec_t = jnp.exp2(la_t[:, q - s_blk:q])

    lane = lax.broadcasted_iota(jnp.int32, (q, 2 * SSM_HEAD_DIM), 1)
    first_head = lane < SSM_HEAD_DIM
    y_parts = []
    for g in range(SSM_GROUPS):
        cg = cm[:, g * SSM_STATE:(g + 1) * SSM_STATE]
        bg = bm[:, g * SSM_STATE:(g + 1) * SSM_STATE]
        cbg = _dot_nt(cg, bg)
        for pair in range(hg // 2):
            h0 = g * hg + 2 * pair
            ws = []
            for h in (h0, h0 + 1):
                seg = la[:, h:h + 1] - lsd_t[h:h + 1, :]
                ws.append(cbg * jnp.exp2(jnp.where(visible, seg, -jnp.inf)))
            xpair = xs[:, h0 * SSM_HEAD_DIM:(h0 + 2) * SSM_HEAD_DIM]
            rhs = jnp.concatenate([jnp.where(first_head, xpair, 0.0), jnp.where(first_head, 0.0, xpair)], axis=0)
            y_parts.append(_dot(jnp.concatenate(ws, axis=1), rhs))
    y = jnp.concatenate(y_parts, axis=1)

    _, seq_b = _row_time_seq((q, SSM_STATE), s_blk)
    y_state_parts = []
    for g in range(SSM_GROUPS):
        cg = cm[:, g * SSM_STATE:(g + 1) * SSM_STATE]
        if s_blk == 1:
            c_seqs, h_seqs = cg, h_src[0, g * gw:(g + 1) * gw, :]
        else:
            c_seqs = jnp.concatenate([jnp.where(seq_b == b, cg, 0.0) for b in range(s_blk)], axis=1)
            h_seqs = jnp.concatenate([h_src[b, g * gw:(g + 1) * gw, :].astype(BF16) for b in range(s_blk)], axis=1)
        y_state_parts.append(_dot_nt(c_seqs, h_seqs))
    y_state = jnp.concatenate(y_state_parts, axis=1)

    e = e_ref[...]
    y = y + y_state * _dot_split_lhs(jnp.exp2(la), e, 2) + xs * dexp_ref[...]
    y = y * z_gate
    y = _rms(y, gn_ref[...], GATED_NORM_EPS)
    y_ref[...] = y.reshape(y_ref.shape).astype(y_ref.dtype)

    tail = jnp.exp2(la_last_rows - lsd)
    xt = (xs * _dot_split_lhs(tail, e, 2)).astype(BF16)
    dec =[jnp.broadcast_to(dec_t[:, b:b + 1], (DT_PAD, SSM_STATE)) for b in range(s_blk)]
    for g in range(SSM_GROUPS):
        bg = bm[:, g * SSM_STATE:(g + 1) * SSM_STATE]
        bg_seqs = bg if s_blk == 1 else jnp.concatenate([jnp.where(seq_b == b, bg, 0.0) for b in range(s_blk)], axis=1)
        upd = _dot_tn(xt[:, g * gw:(g + 1) * gw], bg_seqs)
        for b in range(s_blk):
            for hh in range(hg):
                h = g * hg + hh
                r0 = h * SSM_HEAD_DIM
                hout_ref[b, r0:r0 + SSM_HEAD_DIM, :] = (
                    h_src[b, r0:r0 + SSM_HEAD_DIM, :] * dec[b][h:h + 1, :]
                    + upd[hh * SSM_HEAD_DIM:(hh + 1) * SSM_HEAD_DIM, b * SSM_STATE:(b + 1) * SSM_STATE])


def _ssd(h, gmix, w, h0, conv0, cw, cb, dtb, alog, dexp, gn, e, *, nblk, nc, q, s_blk, time_major):
    rows = SSM_HEADS * SSM_HEAD_DIM
    carry = (SSM_CONV - 1) * s_blk
    per_block0 = h0.shape[0] == nblk * s_blk
    assert per_block0 or (h0.shape[0] == s_blk and conv0.shape[0] == 1)
    cps = SSD_CHUNKS_PER_STEP if (nc % SSD_CHUNKS_PER_STEP == 0 and not time_major) else 1
    ncs = nc // cps
    spec = functools.partial(_rows_spec, q, s_blk, nblk, nc, time_major=time_major, chunks_per_step=cps)
    if time_major:
        y_spec = spec(SSM_D_INNER)
        y_shape = (q // s_blk, nblk * s_blk, SSM_D_INNER)
    else:
        y_spec = pl.BlockSpec((cps * q, SSM_D_INNER), lambda s, c: (s * ncs + c, 0))
        y_shape = (nblk * nc * q, SSM_D_INNER)
    return pl.pallas_call(
        functools.partial(_ssd_kernel, s_blk=s_blk, nc=nc, chunks_per_step=cps),
        grid=(nblk, ncs),
        in_specs=[spec(D_MODEL, ahead=k) for k in range(cps + 1)] + [
            _const_spec((1, D_MODEL)),
            _const_spec((D_MODEL, SSDP_DIM)),
            pl.BlockSpec((s_blk, rows, SSM_STATE), lambda s, c: (s if per_block0 else 0, 0, 0)),
            pl.BlockSpec((1, carry, SSM_CONV_DIM), lambda s, c: (s if per_block0 else 0, 0, 0)),
            _const_spec((SSM_CONV, SSM_CONV_DIM)),
            _const_spec((1, SSM_CONV_DIM)),
            _const_spec((1, DT_PAD)),
            _const_spec((1, DT_PAD)),
            _const_spec((1, SSM_D_INNER)),
            _const_spec((1, SSM_D_INNER)),
            _const_spec((DT_PAD, SSM_D_INNER)),
        ],
        out_specs=[
            y_spec,
            pl.BlockSpec((s_blk, rows, SSM_STATE), lambda s, c: (s, 0, 0)),
            pl.BlockSpec((1, carry, SSM_CONV_DIM), lambda s, c: (s, 0, 0)),
        ],
        out_shape=[
            jax.ShapeDtypeStruct(y_shape, BF16),
            jax.ShapeDtypeStruct((nblk * s_blk, rows, SSM_STATE), F32),
            jax.ShapeDtypeStruct((nblk, carry, SSM_CONV_DIM), F32),
        ],
        scratch_shapes=[
            pltpu.VMEM((_conv_pad_rows(s_blk) + q, SSM_CONV_DIM), F32),
            pltpu.VMEM((q, SSM_D_INNER + DT_PAD), F32),
            pltpu.VMEM((q, SSDP_DIM), F32),
        ],
        compiler_params=_params(2),
        name="ssd",
    )(*([h] * (cps + 1)), gmix, w, h0, conv0, cw, cb, dtb, alog, dexp, gn, e)


def _ret_log_decay(h):
    return float(np.log1p(-np.exp2(-5.0 - h)))


def _ret_project(h_ref, gmix_ref, w_refs, dst):
    pq, pk, pv, pg = _project(h_ref, gmix_ref, w_refs)
    dst[0][...] = pq
    dst[1][...] = pk
    dst[2][...] = pv.astype(BF16)
    dst[3][...] = pg


def _ret_kernel(*refs, s_blk, nc, n_heads, chunks_per_step):
    io_refs, sets = refs[:-8], (refs[-8:-4], refs[-4:])
    hc_ref, hn_refs, io_refs = io_refs[0], io_refs[1:1 + chunks_per_step], io_refs[1 + chunks_per_step:]
    gmix_ref = io_refs[0]
    w_refs = io_refs[1:5]
    cos_ref, sin_ref = io_refs[5:7]
    tables = io_refs[7:10]
    s0_ref, gn_ref, y_ref, sout_ref = io_refs[10:14]
    s = pl.program_id(1)
    c = pl.program_id(2)
    q = cos_ref.shape[0] // chunks_per_step

    @pl.when(jnp.logical_and(s == 0, c == 0))
    def _first():
        _ret_project(hc_ref, gmix_ref, w_refs, sets[0])

    if nc > 1:
        @pl.when(c == 0)
        def _init():
            sout_ref[...] = s0_ref[...]

    def chunk(k, hn_ref, cur_set, next_set):
        rows = pl.ds(k * q, q)
        y_view = y_ref if chunks_per_step == 1 else y_ref.at[rows]
        _ret_step(hn_ref, gmix_ref, *w_refs, cos_ref.at[rows], sin_ref.at[rows], *tables, s0_ref, gn_ref, y_view,
                  sout_ref, cur_set, next_set, s_blk=s_blk, nc=nc, n_heads=n_heads)

    if chunks_per_step % 2 == 0:
        for k, hn_ref in enumerate(hn_refs):
            chunk(k, hn_ref, sets[k % 2], sets[(k + 1) % 2])
    else:
        assert chunks_per_step == 1
        parity = (s * nc + c) % 2

        @pl.when(parity == 0)
        def _even():
            chunk(0, hn_refs[0], sets[0], sets[1])

        @pl.when(parity == 1)
        def _odd():
            chunk(0, hn_refs[0], sets[1], sets[0])


def _ret_step(hn_ref, gmix_ref, wq_ref, wk_ref, wv_ref, wg_ref, cos_ref, sin_ref, dmat_ref, qdec_ref, kdec_ref,
              s0_ref, gn_ref, y_ref, sout_ref, cur_set, next_set, *, s_blk, nc, n_heads):
    q_scr, k_scr, v_scr, g_scr = cur_set
    hb = pl.program_id(0)
    q = cos_ref.shape[0]
    steps = q // s_blk
    half = RET_QK_DIM // 2
    s_src = s0_ref if nc == 1 else sout_ref

    _ret_project(hn_ref, gmix_ref, [wq_ref, wk_ref, wv_ref, wg_ref], next_set)

    qq = q_scr[...]
    kk = k_scr[...]
    vv = v_scr[...]
    rg = g_scr[...]
    cos = cos_ref[...]
    sin = sin_ref[...]
    _, seq_qk = _row_time_seq((q, RET_QK_DIM), s_blk)

    def rot(t, h):
        t1 = t[:, h * RET_QK_DIM:h * RET_QK_DIM + half]
        t2 = t[:, h * RET_QK_DIM + half:(h + 1) * RET_QK_DIM]
        return jnp.concatenate([t1 * cos - t2 * sin, t1 * sin + t2 * cos], axis=1)

    y_parts = []
    for h in range(n_heads):
        if n_heads == RET_HEADS:
            chunk_dec = math.exp(steps * _ret_log_decay(h))
        else:
            head = hb * n_heads + h
            chunk_dec = jnp.float32(math.exp(steps * _ret_log_decay(0)))
            for hh in range(1, RET_HEADS):
                chunk_dec = jnp.where(head == hh, jnp.float32(math.exp(steps * _ret_log_decay(hh))), chunk_dec)
        dmat = dmat_ref[h]
        q_dec = jnp.concatenate([qdec_ref[h]] * 2, axis=1)
        k_dec = jnp.concatenate([kdec_ref[h]] * 2, axis=1)
        qr = rot(qq, h)
        kr = rot(kk, h) * (RET_QK_DIM ** -0.5)
        vh = vv[:, h * RET_V_DIM:(h + 1) * RET_V_DIM]
        scores = _dot_nt(qr, kr) * dmat
        y = _dot(scores, vh)
        qd = qr * q_dec
        kd = kr * k_dec
        r0 = h * RET_QK_DIM
        for b in range(s_blk):
            qd_b = (qd if s_blk == 1 else jnp.where(seq_qk == b, qd, 0.0)).astype(BF16)
            kd_b = (kd if s_blk == 1 else jnp.where(seq_qk == b, kd, 0.0)).astype(BF16)
            sh = s_src[b, r0:r0 + RET_QK_DIM, :]
            y = y + _dot(qd_b, sh)
            sout_ref[b, r0:r0 + RET_QK_DIM, :] = sh * chunk_dec + _dot_tn(kd_b, vh)
        g = gn_ref[:, h * RET_V_DIM:(h + 1) * RET_V_DIM]
        y = _rms(y, g, NORM_EPS) * _silu(rg[:, h * RET_V_DIM:(h + 1) * RET_V_DIM])
        y_parts.append(y)
    y = y_parts[0] if n_heads == 1 else jnp.concatenate(y_parts, axis=1)
    y_ref[...] = y.reshape(y_ref.shape).astype(y_ref.dtype)


def _ret_decay_tables(q, s_blk):
    t = np.arange(q) // s_blk
    seq = np.arange(q) % s_blk
    steps = q // s_blk
    rel = t[:, None] - t[None, :]
    visible = (rel >= 0) & (seq[:, None] == seq[None, :])
    dmat, qdec, kdec = [], [], []
    for h in range(RET_HEADS):
        lg = _ret_log_decay(h)
        dmat.append(np.where(visible, np.exp(np.maximum(rel, 0) * lg), 0.0))
        qdec.append(np.broadcast_to(np.exp((t + 1.0) * lg)[:, None], (q, V7X_LANES)))
        kdec.append(np.broadcast_to(np.exp((steps - 1.0 - t) * lg)[:, None], (q, V7X_LANES)))
    return tuple(jnp.asarray(np.stack(a), F32) for a in (dmat, qdec, kdec))


def _ret(h, gmix, wq, wk, wv, wg, cos, sin, s0, gn, *, nblk, nc, q, s_blk, n_heads, time_major):
    dmat, qdec, kdec = _ret_decay_tables(q, s_blk)
    nhb = RET_HEADS // n_heads
    qk_w = n_heads * RET_QK_DIM
    v_w = n_heads * RET_V_DIM
    per_block0 = s0.shape[0] == nblk * s_blk
    assert per_block0 or s0.shape[0] == s_blk
    cps = RET_CHUNKS_PER_STEP if (nc % RET_CHUNKS_PER_STEP == 0 and not time_major) else 1
    ncs = nc // cps
    rows = functools.partial(_rows_spec, q, s_blk, nblk, nc, D_MODEL, time_major, lead_axes=1, chunks_per_step=cps)

    def head_cols(nrows, width):
        if nhb == 1:
            return _const_spec((nrows, width))
        return pl.BlockSpec((nrows, width), lambda hb, s, c: (0, hb))

    if time_major:
        y_spec = pl.BlockSpec((q // s_blk, s_blk, v_w), lambda hb, s, c: (0, s, hb))
        y_shape = (q // s_blk, nblk * s_blk, RET_V)
    else:
        y_spec = pl.BlockSpec((cps * q, v_w), lambda hb, s, c: (s * ncs + c, hb))
        y_shape = (nblk * nc * q, RET_V)
    return pl.pallas_call(
        functools.partial(_ret_kernel, s_blk=s_blk, nc=nc, n_heads=n_heads, chunks_per_step=cps),
        grid=(nhb, nblk, ncs),
        in_specs=[rows(ahead=k) for k in range(cps + 1)] + [
            _const_spec((1, D_MODEL)),
            head_cols(D_MODEL, qk_w),
            head_cols(D_MODEL, qk_w),
            head_cols(D_MODEL, v_w),
            head_cols(D_MODEL, v_w),
            pl.BlockSpec((cps * q, V7X_LANES), lambda hb, s, c: (c, 0)),
            pl.BlockSpec((cps * q, V7X_LANES), lambda hb, s, c: (c, 0)),
            pl.BlockSpec((n_heads, q, q), lambda hb, s, c: (hb, 0, 0)),
            pl.BlockSpec((n_heads, q, V7X_LANES), lambda hb, s, c: (hb, 0, 0)),
            pl.BlockSpec((n_heads, q, V7X_LANES), lambda hb, s, c: (hb, 0, 0)),
            pl.BlockSpec((s_blk, qk_w, RET_V_DIM), lambda hb, s, c: (s if per_block0 else 0, hb, 0)),
            head_cols(1, v_w),
        ],
        out_specs=[
            y_spec,
            pl.BlockSpec((s_blk, qk_w, RET_V_DIM), lambda hb, s, c: (s, hb, 0)),
        ],
        out_shape=[
            jax.ShapeDtypeStruct(y_shape, BF16),
            jax.ShapeDtypeStruct((nblk * s_blk, RET_HEADS * RET_QK_DIM, RET_V_DIM), F32),
        ],
        scratch_shapes=[
            pltpu.VMEM((q, qk_w), F32),
            pltpu.VMEM((q, qk_w), F32),
            pltpu.VMEM((q, v_w), BF16),
            pltpu.VMEM((q, v_w), F32),
        ] * 2,
        compiler_params=_params(3),
        name="ret",
    )(*([h] * (cps + 1)), gmix, wq, wk, wv, wg, cos, sin, dmat, qdec, kdec, s0, gn)


def _merge_kernel(h_ref, ys_ref, yr_ref, g_ref, wg_ref, wbs_ref, wbr_ref, wo_ref, o_ref):
    h = h_ref[...]
    u = _rms(h, g_ref[...], NORM_EPS).astype(BF16)
    gates = jnp.dot(u, wg_ref[...], preferred_element_type=F32)
    branch_ssm = jnp.dot(ys_ref[...], wbs_ref[...], preferred_element_type=F32)
    branch_ret = jnp.dot(yr_ref[...], wbr_ref[...], preferred_element_type=F32)
    merged = _sigmoid(gates[:, 0:D_MODEL]) * branch_ssm + _sigmoid(gates[:, D_MODEL:2 * D_MODEL]) * branch_ret
    o_ref[...] = h + jnp.dot(merged.astype(BF16), wo_ref[...], preferred_element_type=F32)


def _merge(h, ys, yr, g, wg, wbs, wbr, wo, *, tm):
    t = h.shape[0]
    assert t % tm == 0
    return pl.pallas_call(
        _merge_kernel,
        grid=(t // tm,),
        in_specs=[
            pl.BlockSpec((tm, D_MODEL), lambda i: (i, 0)),
            pl.BlockSpec((tm, SSM_D_INNER), lambda i: (i, 0)),
            pl.BlockSpec((tm, RET_V), lambda i: (i, 0)),
            _const_spec((1, D_MODEL)),
            _const_spec((D_MODEL, 2 * D_MODEL)),
            _const_spec((SSM_D_INNER, D_MODEL)),
            _const_spec((RET_V, D_MODEL)),
            _const_spec((D_MODEL, D_MODEL)),
        ],
        out_specs=pl.BlockSpec((tm, D_MODEL), lambda i: (i, 0)),
        out_shape=jax.ShapeDtypeStruct((t, D_MODEL), F32),
        compiler_params=_params(1),
        name="merge",
    )(h, ys, yr, g, wg, wbs, wbr, wo)


FFN_TILE = 512
SAMPLE_SEQS_PER_STEP = V7X_SUBLANES
SSD_CHUNKS_PER_STEP = 4
RET_CHUNKS_PER_STEP = 4
SAMPLE_RET_HEADS_PER_STEP = 2


def _layer(x2, pos0, ssm0, conv0, ret0, w, final_gain, *, nblk, nc, q, s_blk, ret_heads, time_major):
    t = x2.shape[0]
    tm_ffn = min(FFN_TILE, t)
    steps = q // s_blk
    nseq = nblk * s_blk

    h = _ffn(x2, w["norm_ffn1"], w["ffn1_w1"], w["ffn1_w3"], w["ffn1_w2"], final_gain, final_norm=False, tm=tm_ffn)
    h_rows = h.reshape(steps, nseq, D_MODEL) if time_major else h

    y_ssm, ssm_new, conv_new = _ssd(
        h_rows, w["norm_mix"], w["w_ssd"], ssm0, conv0, w["conv_w"], w["conv_b"], w["dt_bias"], w["a_log"],
        w["d_exp"], w["ssm_norm"], w["head_expand"], nblk=nblk, nc=nc, q=q, s_blk=s_blk, time_major=time_major)
    cos, sin = _rope_tables(w["rope_inv"], nc * q, pos0, s_blk)
    y_ret, ret_new = _ret(h_rows, w["norm_mix"], w["w_q"], w["w_k"], w["w_v"], w["w_rg"], cos, sin, ret0,
                          w["ret_norm"], nblk=nblk, nc=nc, q=q, s_blk=s_blk, n_heads=ret_heads,
                          time_major=time_major)

    h = _merge(h, y_ssm.reshape(t, SSM_D_INNER), y_ret.reshape(t, RET_V), w["norm_mix"], w["w_gates"],
               w["w_branch_ssm"], w["w_branch_ret"], w["w_out"], tm=tm_ffn)
    y = _ffn(h, w["norm_ffn2"], w["ffn2_w1"], w["ffn2_w3"], w["ffn2_w2"], final_gain, final_norm=True, tm=tm_ffn)
    return y, ssm_new, conv_new, ret_new


def _prompt_layer(x, w, final_gain):
    nseq, seq_len, _ = x.shape
    q = math.gcd(seq_len, CHUNK)
    y, ssm_new, conv_new, ret_new = _layer(
        x.reshape(nseq * seq_len, D_MODEL), 0.0,
        jnp.zeros((1, SSM_HEADS * SSM_HEAD_DIM, SSM_STATE), F32),
        jnp.zeros((1, SSM_CONV - 1, SSM_CONV_DIM), F32),
        jnp.zeros((1, RET_HEADS * RET_QK_DIM, RET_V_DIM), F32),
        w, final_gain, nblk=nseq, nc=seq_len // q, q=q, s_blk=1, ret_heads=RET_HEADS, time_major=False)
    return (y.reshape(nseq, seq_len, D_MODEL), ssm_new.reshape(nseq, SSM_HEADS, SSM_HEAD_DIM, SSM_STATE), conv_new,
            ret_new.reshape(nseq, RET_HEADS, RET_QK_DIM, RET_V_DIM))


def _sample_layer(x, ssm0, conv0, ret0, w, final_gain):
    nseq, seq_len, _ = x.shape
    s_blk = SAMPLE_SEQS_PER_STEP
    assert nseq % s_blk == 0 and seq_len <= CHUNK
    nblk = nseq // s_blk
    kc = SSM_CONV - 1
    x_tm = jnp.transpose(x, (1, 0, 2)).reshape(seq_len * nseq, D_MODEL)
    conv0_tm = jnp.transpose(conv0.reshape(nblk, s_blk, kc, SSM_CONV_DIM), (0, 2, 1, 3)).reshape(
        nblk, kc * s_blk, SSM_CONV_DIM)
    y, ssm_new, conv_new, ret_new = _layer(
        x_tm, float(PAST_LEN), ssm0.reshape(nseq, SSM_HEADS * SSM_HEAD_DIM, SSM_STATE), conv0_tm,
        ret0.reshape(nseq, RET_HEADS * RET_QK_DIM, RET_V_DIM), w, final_gain,
        nblk=nblk, nc=1, q=s_blk * seq_len, s_blk=s_blk, ret_heads=SAMPLE_RET_HEADS_PER_STEP, time_major=True)
    conv_new = jnp.transpose(conv_new.reshape(nblk, kc, s_blk, SSM_CONV_DIM), (0, 2, 1, 3)).reshape(
        nseq, kc, SSM_CONV_DIM)
    return (jnp.transpose(y.reshape(seq_len, nseq, D_MODEL), (1, 0, 2)),
            ssm_new.reshape(nseq, SSM_HEADS, SSM_HEAD_DIM, SSM_STATE), conv_new,
            ret_new.reshape(nseq, RET_HEADS, RET_QK_DIM, RET_V_DIM))


def _prep_weights(norm_ffn1, ffn1_w1, ffn1_w3, ffn1_w2, norm_mix, w_in, conv_w, conv_b, dt_bias, a_log, ssm_d,
                  ssm_norm, ret_norm, w_branch_ssm, w_branch_ret, w_out, norm_ffn2, ffn2_w1, ffn2_w3, ffn2_w2):
    o_z = 0
    o_xbc = o_z + SSM_D_INNER
    o_dt = o_xbc + SSM_CONV_DIM
    o_q = o_dt + SSM_HEADS
    o_k = o_q + RET_QK
    o_v = o_k + RET_QK
    o_rg = o_v + RET_V
    o_ga = o_rg + RET_V
    o_end = o_ga + 2 * D_MODEL
    assert o_end == w_in.shape[1]

    w_in_bf = w_in.astype(BF16)

    def cols(a, b):
        return w_in_bf[:, a:b]

    assert (o_z, o_xbc, o_dt) == (SSDP_Z, SSDP_XBC, SSDP_DT) and SSDP_DIM <= o_end
    half = RET_QK_DIM // 2
    head_of_channel = jnp.arange(SSM_D_INNER, dtype=jnp.int32) // SSM_HEAD_DIM
    return {
        "norm_ffn1": norm_ffn1.reshape(1, -1), "ffn1_w1": ffn1_w1.astype(BF16), "ffn1_w3": ffn1_w3.astype(BF16),
        "ffn1_w2": ffn1_w2.astype(BF16),
        "norm_mix": norm_mix.reshape(1, -1), "w_ssd": w_in_bf,
        "w_q": cols(o_q, o_k), "w_k": cols(o_k, o_v), "w_v": cols(o_v, o_rg), "w_rg": cols(o_rg, o_ga),
        "w_gates": cols(o_ga, o_end),
        "conv_w": conv_w, "conv_b": conv_b.reshape(1, -1),
        "dt_bias": jnp.pad(dt_bias, (0, DT_PAD - SSM_HEADS)).reshape(1, -1),
        "a_log": jnp.pad(a_log, (0, DT_PAD - SSM_HEADS)).reshape(1, -1),
        "d_exp": jnp.repeat(ssm_d, SSM_HEAD_DIM).reshape(1, -1),
        "ssm_norm": ssm_norm.reshape(1, -1), "ret_norm": ret_norm.reshape(1, -1),
        "head_expand": (head_of_channel[None, :] == jnp.arange(DT_PAD, dtype=jnp.int32)[:, None]).astype(BF16),
        "rope_inv": (ROPE_BASE ** (-jnp.arange(half, dtype=F32) / half)).reshape(1, half),
        "w_branch_ssm": w_branch_ssm.astype(BF16), "w_branch_ret": w_branch_ret.astype(BF16),
        "w_out": w_out.astype(BF16),
        "norm_ffn2": norm_ffn2.reshape(1, -1), "ffn2_w1": ffn2_w1.astype(BF16), "ffn2_w3": ffn2_w3.astype(BF16),
        "ffn2_w2": ffn2_w2.astype(BF16),
    }


def kernel(x_prompt, x_sample, state_ssm, state_conv, state_ret, norm_ffn1, ffn1_w1, ffn1_w3, ffn1_w2, norm_mix, w_in,
           conv_w, conv_b, dt_bias, a_log, ssm_d, ssm_norm, ret_norm, w_branch_ssm, w_branch_ret, w_out, norm_ffn2,
           ffn2_w1, ffn2_w3, ffn2_w2, norm_final):
    depth = norm_ffn1.shape[0]
    assert depth == 1, "the final RMSNorm is fused into the last layer's second FFN"
    w = _prep_weights(norm_ffn1[0], ffn1_w1[0], ffn1_w3[0], ffn1_w2[0], norm_mix[0], w_in[0], conv_w[0], conv_b[0],
                      dt_bias[0], a_log[0], ssm_d[0], ssm_norm[0], ret_norm[0], w_branch_ssm[0], w_branch_ret[0],
                      w_out[0], norm_ffn2[0], ffn2_w1[0], ffn2_w3[0], ffn2_w2[0])
    final_gain = norm_final.reshape(1, -1)
    yp, ssm_p, conv_p, ret_p = _prompt_layer(x_prompt, w, final_gain)
    ys, ssm_s, conv_s, ret_s = _sample_layer(x_sample, state_ssm[0], state_conv[0], state_ret[0], w, final_gain)
    return (yp, ys, ssm_p[None], conv_p[None], ret_p[None], ssm_s[None], conv_s[None], ret_s[None])
```

```python
import functools
import math

import jax
import jax.numpy as jnp
import numpy as np
from jax import lax
from jax.experimental import pallas as pl
from jax.experimental.pallas import tpu as pltpu

F32 = jnp.float32
BF16 = jnp.bfloat16

D_MODEL = 1024
D_FF = 2816
SSM_D_INNER = 2048
SSM_HEAD_DIM = 64
SSM_HEADS = 32
SSM_GROUPS = 4
SSM_STATE = 128
SSM_CONV = 4
SSM_BC = 2 * SSM_GROUPS * SSM_STATE
SSM_CONV_DIM = SSM_D_INNER + SSM_BC
RET_HEADS = 4
RET_QK_DIM = 256
RET_V_DIM = 512
RET_QK = RET_HEADS * RET_QK_DIM
RET_V = RET_HEADS * RET_V_DIM
ROPE_BASE = 10000.0
PAST_LEN = 16384
CHUNK = 128
NORM_EPS = 1e-6
GATED_NORM_EPS = 1e-5

V7X_SUBLANES = 8
V7X_LANES = 128
V7X_VMEM_BYTES = 64 * 1024 * 1024
VMEM_LIMIT = V7X_VMEM_BYTES - 8 * 1024 * 1024

DT_PAD = V7X_LANES
LOG_DT_FLOOR = -1e30
LOG2_E = math.log2(math.e)
EXPAND_PIECES = 2

SSDP_Z = 0
SSDP_XBC = SSM_D_INNER
SSDP_DT = SSM_D_INNER + SSM_CONV_DIM
SSDP_DIM = SSDP_DT + DT_PAD


def _rms(x, g, eps):
    return x * lax.rsqrt(jnp.mean(x * x, axis=-1, keepdims=True) + eps) * g


def _sigmoid(x):
    return 0.5 + 0.5 * jnp.tanh(0.5 * x)


def _silu(x):
    half = 0.5 * x
    return half + half * jnp.tanh(half)


def _softplus(x):
    return jnp.maximum(x, 0.0) + jnp.log1p(jnp.exp(-jnp.abs(x)))


def _dot(a, b):
    return jnp.dot(a.astype(BF16), b.astype(BF16), preferred_element_type=F32)


def _dot_nt(a, b):
    return lax.dot_general(a.astype(BF16), b.astype(BF16), (((1,), (1,)), ((), ())), preferred_element_type=F32)


def _dot_tn(a, b):
    return lax.dot_general(a.astype(BF16), b.astype(BF16), (((0,), (0,)), ((), ())), preferred_element_type=F32)


def _dot_split(a, b_f32, passes):
    acc = None
    rem = b_f32
    for _ in range(passes):
        piece = rem.astype(BF16)
        term = jnp.dot(a, piece, preferred_element_type=F32)
        acc = term if acc is None else acc + term
        rem = rem - piece.astype(F32)
    return acc


def _dot_split_lhs(a_f32, b_stacked, passes):
    pieces = []
    rem = a_f32
    for _ in range(passes):
        piece = rem.astype(BF16)
        pieces.append(piece)
        rem = rem - piece.astype(F32)
    return jnp.dot(jnp.concatenate(pieces, axis=1), b_stacked, preferred_element_type=F32)


def _transpose_split(eye, x_f32, passes):
    acc = None
    rem = x_f32
    for _ in range(passes):
        piece = rem.astype(BF16)
        term = lax.dot_general(eye, piece, (((1,), (1,)), ((), ())), preferred_element_type=F32)
        acc = term if acc is None else acc + term
        rem = rem - piece.astype(F32)
    return acc


def _const_spec(shape):
    nd = len(shape)
    return pl.BlockSpec(shape, lambda *_: (0,) * nd, pipeline_mode=pl.Buffered(1))


def _params(n_grid_dims):
    return pltpu.CompilerParams(dimension_semantics=("arbitrary",) * n_grid_dims, vmem_limit_bytes=VMEM_LIMIT)


def _ffn_kernel(x_ref, g_ref, w1_ref, w3_ref, w2_ref, gf_ref, o_ref, *, final_norm):
    x = x_ref[...]
    xn = _rms(x, g_ref[...], NORM_EPS).astype(BF16)
    a = jnp.dot(xn, w1_ref[...], preferred_element_type=F32)
    b = jnp.dot(xn, w3_ref[...], preferred_element_type=F32)
    gated = (_silu(a) * b).astype(BF16)
    h = x + 0.5 * jnp.dot(gated, w2_ref[...], preferred_element_type=F32)
    if final_norm:
        h = _rms(h, gf_ref[...], NORM_EPS)
    o_ref[...] = h


def _ffn(x, g, w1, w3, w2, gf, *, final_norm, tm):
    t = x.shape[0]
    assert t % tm == 0
    return pl.pallas_call(
        functools.partial(_ffn_kernel, final_norm=final_norm),
        grid=(t // tm,),
        in_specs=[
            pl.BlockSpec((tm, D_MODEL), lambda i: (i, 0)),
            _const_spec((1, D_MODEL)),
            _const_spec((D_MODEL, D_FF)),
            _const_spec((D_MODEL, D_FF)),
            _const_spec((D_FF, D_MODEL)),
            _const_spec((1, D_MODEL)),
        ],
        out_specs=pl.BlockSpec((tm, D_MODEL), lambda i: (i, 0)),
        out_shape=jax.ShapeDtypeStruct((t, D_MODEL), F32),
        compiler_params=_params(1),
        name="ffn_final" if final_norm else "ffn",
    )(x, g, w1, w3, w2, gf)


def _rope_kernel(inv_ref, cos_ref, sin_ref, *, pos0, rows_per_pos):
    rows = cos_ref.shape[0]
    step = lax.broadcasted_iota(jnp.int32, (rows, V7X_LANES), 0) // rows_per_pos
    ang = (pos0 + step.astype(F32)) * inv_ref[...]
    cos_ref[...] = jnp.cos(ang)
    sin_ref[...] = jnp.sin(ang)


def _rope_tables(inv, rows, pos0, rows_per_pos):
    return pl.pallas_call(
        functools.partial(_rope_kernel, pos0=pos0, rows_per_pos=rows_per_pos),
        out_shape=[jax.ShapeDtypeStruct((rows, V7X_LANES), F32)] * 2,
        name="rope",
    )(inv)


def _chunk_rows(ref):
    return ref[...].reshape(-1, ref.shape[-1])


def _row_time_seq(shape, s_blk):
    r = lax.broadcasted_iota(jnp.int32, shape, 0)
    return r // s_blk, r % s_blk


def _pair_mask(q, s_blk):
    r = lax.broadcasted_iota(jnp.int32, (q, q), 0)
    c = lax.broadcasted_iota(jnp.int32, (q, q), 1)
    if s_blk == 1:
        return r >= c
    return jnp.logical_and(r // s_blk >= c // s_blk, r % s_blk == c % s_blk)


def _rows_spec(q, s_blk, nblk, nc, width, time_major, *, ahead=0, lead_axes=0, chunks_per_step=1):
    last = nblk * nc - 1

    def chunk_index(args):
        s, c = args[lead_axes], args[lead_axes + 1]
        return jnp.minimum((s * (nc // chunks_per_step) + c) * chunks_per_step + ahead, last)

    if time_major:
        assert nc == 1
        return pl.BlockSpec((q // s_blk, s_blk, width), lambda *a: (0, chunk_index(a), 0))
    return pl.BlockSpec((q, width), lambda *a: (chunk_index(a), 0))


def _project(h_ref, g_ref, w_refs):
    u = _rms(_chunk_rows(h_ref), g_ref[...], NORM_EPS).astype(BF16)
    return [jnp.dot(u, w_ref[...], preferred_element_type=F32) for w_ref in w_refs]


def _conv_pad_rows(s_blk):
    need = (SSM_CONV - 1) * s_blk
    return -(-need // V7X_SUBLANES) * V7X_SUBLANES


def _ssd_kernel(hc_ref, *refs, s_blk, nc, chunks_per_step):
    hn_refs, refs = refs[:chunks_per_step], refs[chunks_per_step:]
    n_in = 11
    y_ref = refs[n_in]
    for k, hn_ref in enumerate(hn_refs):
        if chunks_per_step == 1:
            y_view = y_ref
        else:
            q = y_ref.shape[0] // chunks_per_step
            y_view = y_ref.at[pl.ds(k * q, q)]
        _ssd_chunk(hc_ref, hn_ref, *refs[:n_in], y_view, *refs[n_in + 1:], s_blk=s_blk, nc=nc,
                   chunks_per_step=chunks_per_step, first_of_step=(k == 0))


def _ssd_chunk(hc_ref, hn_ref, gmix_ref, w_ref, h0_ref, conv0_ref, cw_ref, cb_ref, dtb_ref, alog_ref, dexp_ref,
               gn_ref, e_ref, y_ref, hout_ref, convout_ref, xp_scr, zdt_scr, next_scr, *, s_blk, nc, chunks_per_step,
               first_of_step):
    s = pl.program_id(0)
    c = pl.program_id(1)
    pad = _conv_pad_rows(s_blk)
    q = xp_scr.shape[0] - pad
    carry = (SSM_CONV - 1) * s_blk
    steps = q // s_blk
    hg = SSM_HEADS // SSM_GROUPS
    gw = hg * SSM_HEAD_DIM
    h_src = h0_ref if nc == 1 else hout_ref

    if first_of_step:
        @pl.when(jnp.logical_and(s == 0, c == 0))
        def _first():
            next_scr[...] = _project(hc_ref, gmix_ref, [w_ref])[0]

        @pl.when(c == 0)
        def _init():
            xp_scr[0:pad, :] = jnp.zeros((pad, SSM_CONV_DIM), F32)
            xp_scr[pad - carry:pad, :] = conv0_ref[0]
            if nc > 1:
                hout_ref[...] = h0_ref[...]

    xp_scr[pad:pad + q, :] = next_scr[:, SSDP_XBC:SSDP_DT]
    zdt_scr[:, 0:SSM_D_INNER] = next_scr[:, SSDP_Z:SSDP_XBC]
    zdt_scr[:, SSM_D_INNER:SSM_D_INNER + DT_PAD] = next_scr[:, SSDP_DT:SSDP_DIM]
    next_scr[...] = _project(hn_ref, gmix_ref, [w_ref])[0]

    z_gate = _silu(zdt_scr[:, 0:SSM_D_INNER])

    xall = xp_scr[...]
    conv = cb_ref[...] + xall[pad:pad + q, :] * cw_ref[SSM_CONV - 1:SSM_CONV, :]
    for j in range(1, SSM_CONV):
        d = j * s_blk
        if d % V7X_SUBLANES == 0:
            back = xall[pad - d:pad - d + q, :]
        else:
            back = pltpu.roll(xall, d, 0)[pad:pad + q, :]
        conv = conv + back * cw_ref[SSM_CONV - 1 - j:SSM_CONV - j, :]
    @pl.when(c == nc // chunks_per_step - 1)
    def _():
        convout_ref[0] = xp_scr[pad + q - carry:pad + q, :]

    if nc > 1:
        xp_scr[0:pad, :] = xp_scr[q:q + pad, :]
    dt_rows = zdt_scr[:, SSM_D_INNER:SSM_D_INNER + DT_PAD]
    xc = _silu(conv)
    xs = xc[:, 0:SSM_D_INNER]
    bm = xc[:, SSM_D_INNER:SSM_D_INNER + SSM_GROUPS * SSM_STATE]
    cm = xc[:, SSM_D_INNER + SSM_GROUPS * SSM_STATE:SSM_CONV_DIM]

    dt = _softplus(dt_rows + dtb_ref[...])
    a = -jnp.exp(alog_ref[...])
    visible = _pair_mask(q, s_blk)
    la = _dot_split(jnp.where(visible, 1.0, 0.0).astype(BF16), dt * (a * LOG2_E), 3)
    lsd = la - jnp.maximum(jnp.log2(dt), LOG_DT_FLOOR)
    eye_r = lax.broadcasted_iota(jnp.int32, (DT_PAD, DT_PAD), 0)
    eye_c = lax.broadcasted_iota(jnp.int32, (DT_PAD, DT_PAD), 1)
    eye = jnp.where(eye_r == eye_c, 1.0, 0.0).astype(BF16)
    la_t = _transpose_split(eye, la, 3)
    lsd_t = _transpose_split(eye, lsd, 3)
    la_last = la[q - s_blk:q, :]
    la_last_rows = la_last if s_blk == 1 else jnp.concatenate([la_last] * steps, axis=0)
    dec_t = jnp.exp2(la_t[:, q - s_blk:q])

    lane = lax.broadcasted_iota(jnp.int32, (q, 2 * SSM_HEAD_DIM), 1)
    first_head = lane < SSM_HEAD_DIM
    y_parts = []
    for g in range(SSM_GROUPS):
        cg = cm[:, g * SSM_STATE:(g + 1) * SSM_STATE]
        bg = bm[:, g * SSM_STATE:(g + 1) * SSM_STATE]
        cbg = _dot_nt(cg, bg)
        for pair in range(hg // 2):
            h0 = g * hg + 2 * pair
            ws = []
            for h in (h0, h0 + 1):
                seg = la[:, h:h + 1] - lsd_t[h:h + 1, :]
                ws.append(cbg * jnp.exp2(jnp.where(visible, seg, -jnp.inf)))
            xpair = xs[:, h0 * SSM_HEAD_DIM:(h0 + 2) * SSM_HEAD_DIM]
            rhs = jnp.concatenate([jnp.where(first_head, xpair, 0.0), jnp.where(first_head, 0.0, xpair)], axis=0)
            y_parts.append(_dot(jnp.concatenate(ws, axis=1), rhs))
    y = jnp.concatenate(y_parts, axis=1)

    _, seq_b = _row_time_seq((q, SSM_STATE), s_blk)
    y_state_parts = []
    for g in range(SSM_GROUPS):
        cg = cm[:, g * SSM_STATE:(g + 1) * SSM_STATE]
        if s_blk == 1:
            c_seqs, h_seqs = cg, h_src[0, g * gw:(g + 1) * gw, :]
        else:
            c_seqs = jnp.concatenate([jnp.where(seq_b == b, cg, 0.0) for b in range(s_blk)], axis=1)
            h_seqs = jnp.concatenate([h_src[b, g * gw:(g + 1) * gw, :].astype(BF16) for b in range(s_blk)], axis=1)
        y_state_parts.append(_dot_nt(c_seqs, h_seqs))
    y_state = jnp.concatenate(y_state_parts, axis=1)

    e = e_ref[...]
    y = y + y_state * _dot_split_lhs(jnp.exp2(la), e, EXPAND_PIECES) + xs * dexp_ref[...]
    y = y * z_gate
    y = _rms(y, gn_ref[...], GATED_NORM_EPS)
    y_ref[...] = y.reshape(y_ref.shape).astype(y_ref.dtype)

    tail = jnp.exp2(la_last_rows - lsd)
    xt = (xs * _dot_split_lhs(tail, e, EXPAND_PIECES)).astype(BF16)
    dec =[jnp.broadcast_to(dec_t[:, b:b + 1], (DT_PAD, SSM_STATE)) for b in range(s_blk)]
    for g in range(SSM_GROUPS):
        bg = bm[:, g * SSM_STATE:(g + 1) * SSM_STATE]
        bg_seqs = bg if s_blk == 1 else jnp.concatenate([jnp.where(seq_b == b, bg, 0.0) for b in range(s_blk)], axis=1)
        upd = _dot_tn(xt[:, g * gw:(g + 1) * gw], bg_seqs)
        for b in range(s_blk):
            for hh in range(hg):
                h = g * hg + hh
                r0 = h * SSM_HEAD_DIM
                hout_ref[b, r0:r0 + SSM_HEAD_DIM, :] = (
                    h_src[b, r0:r0 + SSM_HEAD_DIM, :] * dec[b][h:h + 1, :]
                    + upd[hh * SSM_HEAD_DIM:(hh + 1) * SSM_HEAD_DIM, b * SSM_STATE:(b + 1) * SSM_STATE])


def _ssd(h, gmix, w, h0, conv0, cw, cb, dtb, alog, dexp, gn, e, *, nblk, nc, q, s_blk, time_major):
    rows = SSM_HEADS * SSM_HEAD_DIM
    carry = (SSM_CONV - 1) * s_blk
    per_block0 = h0.shape[0] == nblk * s_blk
    assert per_block0 or (h0.shape[0] == s_blk and conv0.shape[0] == 1)
    cps = SSD_CHUNKS_PER_STEP if (nc % SSD_CHUNKS_PER_STEP == 0 and not time_major) else 1
    ncs = nc // cps
    spec = functools.partial(_rows_spec, q, s_blk, nblk, nc, time_major=time_major, chunks_per_step=cps)
    if time_major:
        y_spec = spec(SSM_D_INNER)
        y_shape = (q // s_blk, nblk * s_blk, SSM_D_INNER)
    else:
        y_spec = pl.BlockSpec((cps * q, SSM_D_INNER), lambda s, c: (s * ncs + c, 0))
        y_shape = (nblk * nc * q, SSM_D_INNER)
    return pl.pallas_call(
        functools.partial(_ssd_kernel, s_blk=s_blk, nc=nc, chunks_per_step=cps),
        grid=(nblk, ncs),
        in_specs=[spec(D_MODEL, ahead=k) for k in range(cps + 1)] + [
            _const_spec((1, D_MODEL)),
            _const_spec((D_MODEL, SSDP_DIM)),
            pl.BlockSpec((s_blk, rows, SSM_STATE), lambda s, c: (s if per_block0 else 0, 0, 0)),
            pl.BlockSpec((1, carry, SSM_CONV_DIM), lambda s, c: (s if per_block0 else 0, 0, 0)),
            _const_spec((SSM_CONV, SSM_CONV_DIM)),
            _const_spec((1, SSM_CONV_DIM)),
            _const_spec((1, DT_PAD)),
            _const_spec((1, DT_PAD)),
            _const_spec((1, SSM_D_INNER)),
            _const_spec((1, SSM_D_INNER)),
            _const_spec((EXPAND_PIECES * DT_PAD, SSM_D_INNER)),
        ],
        out_specs=[
            y_spec,
            pl.BlockSpec((s_blk, rows, SSM_STATE), lambda s, c: (s, 0, 0)),
            pl.BlockSpec((1, carry, SSM_CONV_DIM), lambda s, c: (s, 0, 0)),
        ],
        out_shape=[
            jax.ShapeDtypeStruct(y_shape, BF16),
            jax.ShapeDtypeStruct((nblk * s_blk, rows, SSM_STATE), F32),
            jax.ShapeDtypeStruct((nblk, carry, SSM_CONV_DIM), F32),
        ],
        scratch_shapes=[
            pltpu.VMEM((_conv_pad_rows(s_blk) + q, SSM_CONV_DIM), F32),
            pltpu.VMEM((q, SSM_D_INNER + DT_PAD), F32),
            pltpu.VMEM((q, SSDP_DIM), F32),
        ],
        compiler_params=_params(2),
        name="ssd",
    )(*([h] * (cps + 1)), gmix, w, h0, conv0, cw, cb, dtb, alog, dexp, gn, e)


def _ret_log_decay(h):
    return float(np.log1p(-np.exp2(-5.0 - h)))


def _ret_project(h_ref, gmix_ref, w_refs, dst):
    pq, pk, pv, pg = _project(h_ref, gmix_ref, w_refs)
    dst[0][...] = pq
    dst[1][...] = pk
    dst[2][...] = pv.astype(BF16)
    dst[3][...] = pg


def _ret_kernel(*refs, s_blk, nc, n_heads, chunks_per_step):
    io_refs, sets = refs[:-8], (refs[-8:-4], refs[-4:])
    hc_ref, hn_refs, io_refs = io_refs[0], io_refs[1:1 + chunks_per_step], io_refs[1 + chunks_per_step:]
    gmix_ref = io_refs[0]
    w_refs = io_refs[1:5]
    cos_ref, sin_ref = io_refs[5:7]
    tables = io_refs[7:10]
    s0_ref, gn_ref, y_ref, sout_ref = io_refs[10:14]
    s = pl.program_id(1)
    c = pl.program_id(2)
    q = cos_ref.shape[0] // chunks_per_step

    @pl.when(jnp.logical_and(s == 0, c == 0))
    def _first():
        _ret_project(hc_ref, gmix_ref, w_refs, sets[0])

    if nc > 1:
        @pl.when(c == 0)
        def _init():
            sout_ref[...] = s0_ref[...]

    def chunk(k, hn_ref, cur_set, next_set):
        rows = pl.ds(k * q, q)
        y_view = y_ref if chunks_per_step == 1 else y_ref.at[rows]
        _ret_step(hn_ref, gmix_ref, *w_refs, cos_ref.at[rows], sin_ref.at[rows], *tables, s0_ref, gn_ref, y_view,
                  sout_ref, cur_set, next_set, s_blk=s_blk, nc=nc, n_heads=n_heads)

    if chunks_per_step % 2 == 0:
        for k, hn_ref in enumerate(hn_refs):
            chunk(k, hn_ref, sets[k % 2], sets[(k + 1) % 2])
    else:
        assert chunks_per_step == 1
        parity = (s * nc + c) % 2

        @pl.when(parity == 0)
        def _even():
            chunk(0, hn_refs[0], sets[0], sets[1])

        @pl.when(parity == 1)
        def _odd():
            chunk(0, hn_refs[0], sets[1], sets[0])


def _ret_step(hn_ref, gmix_ref, wq_ref, wk_ref, wv_ref, wg_ref, cos_ref, sin_ref, dmat_ref, qdec_ref, kdec_ref,
              s0_ref, gn_ref, y_ref, sout_ref, cur_set, next_set, *, s_blk, nc, n_heads):
    q_scr, k_scr, v_scr, g_scr = cur_set
    hb = pl.program_id(0)
    q = cos_ref.shape[0]
    steps = q // s_blk
    half = RET_QK_DIM // 2
    s_src = s0_ref if nc == 1 else sout_ref

    _ret_project(hn_ref, gmix_ref, [wq_ref, wk_ref, wv_ref, wg_ref], next_set)

    qq = q_scr[...]
    kk = k_scr[...]
    vv = v_scr[...]
    rg = g_scr[...]
    cos = cos_ref[...]
    sin = sin_ref[...]
    _, seq_qk = _row_time_seq((q, RET_QK_DIM), s_blk)

    def rot(t, h):
        t1 = t[:, h * RET_QK_DIM:h * RET_QK_DIM + half]
        t2 = t[:, h * RET_QK_DIM + half:(h + 1) * RET_QK_DIM]
        return jnp.concatenate([t1 * cos - t2 * sin, t1 * sin + t2 * cos], axis=1)

    y_parts = []
    for h in range(n_heads):
        if n_heads == RET_HEADS:
            chunk_dec = math.exp(steps * _ret_log_decay(h))
        else:
            head = hb * n_heads + h
            chunk_dec = jnp.float32(math.exp(steps * _ret_log_decay(0)))
            for hh in range(1, RET_HEADS):
                chunk_dec = jnp.where(head == hh, jnp.float32(math.exp(steps * _ret_log_decay(hh))), chunk_dec)
        dmat = dmat_ref[h]
        q_dec = jnp.concatenate([qdec_ref[h]] * 2, axis=1)
        k_dec = jnp.concatenate([kdec_ref[h]] * 2, axis=1)
        qr = rot(qq, h)
        kr = rot(kk, h) * (RET_QK_DIM ** -0.5)
        vh = vv[:, h * RET_V_DIM:(h + 1) * RET_V_DIM]
        scores = _dot_nt(qr, kr) * dmat
        y = _dot(scores, vh)
        qd = qr * q_dec
        kd = kr * k_dec
        r0 = h * RET_QK_DIM
        for b in range(s_blk):
            qd_b = (qd if s_blk == 1 else jnp.where(seq_qk == b, qd, 0.0)).astype(BF16)
            kd_b = (kd if s_blk == 1 else jnp.where(seq_qk == b, kd, 0.0)).astype(BF16)
            sh = s_src[b, r0:r0 + RET_QK_DIM, :]
            y = y + _dot(qd_b, sh)
            sout_ref[b, r0:r0 + RET_QK_DIM, :] = sh * chunk_dec + _dot_tn(kd_b, vh)
        g = gn_ref[:, h * RET_V_DIM:(h + 1) * RET_V_DIM]
        y = _rms(y, g, NORM_EPS) * _silu(rg[:, h * RET_V_DIM:(h + 1) * RET_V_DIM])
        y_parts.append(y)
    y = y_parts[0] if n_heads == 1 else jnp.concatenate(y_parts, axis=1)
    y_ref[...] = y.reshape(y_ref.shape).astype(y_ref.dtype)


def _ret_decay_tables(q, s_blk):
    t = np.arange(q) // s_blk
    seq = np.arange(q) % s_blk
    steps = q // s_blk
    rel = t[:, None] - t[None, :]
    visible = (rel >= 0) & (seq[:, None] == seq[None, :])
    dmat, qdec, kdec = [], [], []
    for h in range(RET_HEADS):
        lg = _ret_log_decay(h)
        dmat.append(np.where(visible, np.exp(np.maximum(rel, 0) * lg), 0.0))
        qdec.append(np.broadcast_to(np.exp((t + 1.0) * lg)[:, None], (q, V7X_LANES)))
        kdec.append(np.broadcast_to(np.exp((steps - 1.0 - t) * lg)[:, None], (q, V7X_LANES)))
    return tuple(jnp.asarray(np.stack(a), F32) for a in (dmat, qdec, kdec))


def _ret(h, gmix, wq, wk, wv, wg, cos, sin, s0, gn, *, nblk, nc, q, s_blk, n_heads, time_major):
    dmat, qdec, kdec = _ret_decay_tables(q, s_blk)
    nhb = RET_HEADS // n_heads
    qk_w = n_heads * RET_QK_DIM
    v_w = n_heads * RET_V_DIM
    per_block0 = s0.shape[0] == nblk * s_blk
    assert per_block0 or s0.shape[0] == s_blk
    cps = RET_CHUNKS_PER_STEP if (nc % RET_CHUNKS_PER_STEP == 0 and not time_major) else 1
    ncs = nc // cps
    rows = functools.partial(_rows_spec, q, s_blk, nblk, nc, D_MODEL, time_major, lead_axes=1, chunks_per_step=cps)

    def head_cols(nrows, width):
        if nhb == 1:
            return _const_spec((nrows, width))
        return pl.BlockSpec((nrows, width), lambda hb, s, c: (0, hb))

    if time_major:
        y_spec = pl.BlockSpec((q // s_blk, s_blk, v_w), lambda hb, s, c: (0, s, hb))
        y_shape = (q // s_blk, nblk * s_blk, RET_V)
    else:
        y_spec = pl.BlockSpec((cps * q, v_w), lambda hb, s, c: (s * ncs + c, hb))
        y_shape = (nblk * nc * q, RET_V)
    return pl.pallas_call(
        functools.partial(_ret_kernel, s_blk=s_blk, nc=nc, n_heads=n_heads, chunks_per_step=cps),
        grid=(nhb, nblk, ncs),
        in_specs=[rows(ahead=k) for k in range(cps + 1)] + [
            _const_spec((1, D_MODEL)),
            head_cols(D_MODEL, qk_w),
            head_cols(D_MODEL, qk_w),
            head_cols(D_MODEL, v_w),
            head_cols(D_MODEL, v_w),
            pl.BlockSpec((cps * q, V7X_LANES), lambda hb, s, c: (c, 0)),
            pl.BlockSpec((cps * q, V7X_LANES), lambda hb, s, c: (c, 0)),
            pl.BlockSpec((n_heads, q, q), lambda hb, s, c: (hb, 0, 0)),
            pl.BlockSpec((n_heads, q, V7X_LANES), lambda hb, s, c: (hb, 0, 0)),
            pl.BlockSpec((n_heads, q, V7X_LANES), lambda hb, s, c: (hb, 0, 0)),
            pl.BlockSpec((s_blk, qk_w, RET_V_DIM), lambda hb, s, c: (s if per_block0 else 0, hb, 0)),
            head_cols(1, v_w),
        ],
        out_specs=[
            y_spec,
            pl.BlockSpec((s_blk, qk_w, RET_V_DIM), lambda hb, s, c: (s, hb, 0)),
        ],
        out_shape=[
            jax.ShapeDtypeStruct(y_shape, BF16),
            jax.ShapeDtypeStruct((nblk * s_blk, RET_HEADS * RET_QK_DIM, RET_V_DIM), F32),
        ],
        scratch_shapes=[
            pltpu.VMEM((q, qk_w), F32),
            pltpu.VMEM((q, qk_w), F32),
            pltpu.VMEM((q, v_w), BF16),
            pltpu.VMEM((q, v_w), F32),
        ] * 2,
        compiler_params=_params(3),
        name="ret",
    )(*([h] * (cps + 1)), gmix, wq, wk, wv, wg, cos, sin, dmat, qdec, kdec, s0, gn)


def _merge_kernel(h_ref, ys_ref, yr_ref, g_ref, wg_ref, wbs_ref, wbr_ref, wo_ref, o_ref):
    h = h_ref[...]
    u = _rms(h, g_ref[...], NORM_EPS).astype(BF16)
    gates = jnp.dot(u, wg_ref[...], preferred_element_type=F32)
    branch_ssm = jnp.dot(ys_ref[...], wbs_ref[...], preferred_element_type=F32)
    branch_ret = jnp.dot(yr_ref[...], wbr_ref[...], preferred_element_type=F32)
    merged = _sigmoid(gates[:, 0:D_MODEL]) * branch_ssm + _sigmoid(gates[:, D_MODEL:2 * D_MODEL]) * branch_ret
    o_ref[...] = h + jnp.dot(merged.astype(BF16), wo_ref[...], preferred_element_type=F32)


def _merge(h, ys, yr, g, wg, wbs, wbr, wo, *, tm):
    t = h.shape[0]
    assert t % tm == 0
    return pl.pallas_call(
        _merge_kernel,
        grid=(t // tm,),
        in_specs=[
            pl.BlockSpec((tm, D_MODEL), lambda i: (i, 0)),
            pl.BlockSpec((tm, SSM_D_INNER), lambda i: (i, 0)),
            pl.BlockSpec((tm, RET_V), lambda i: (i, 0)),
            _const_spec((1, D_MODEL)),
            _const_spec((D_MODEL, 2 * D_MODEL)),
            _const_spec((SSM_D_INNER, D_MODEL)),
            _const_spec((RET_V, D_MODEL)),
            _const_spec((D_MODEL, D_MODEL)),
        ],
        out_specs=pl.BlockSpec((tm, D_MODEL), lambda i: (i, 0)),
        out_shape=jax.ShapeDtypeStruct((t, D_MODEL), F32),
        compiler_params=_params(1),
        name="merge",
    )(h, ys, yr, g, wg, wbs, wbr, wo)


FFN_TILE = 512
SAMPLE_SEQS_PER_STEP = V7X_SUBLANES
SSD_CHUNKS_PER_STEP = 4
RET_CHUNKS_PER_STEP = 4
SAMPLE_RET_HEADS_PER_STEP = 2


def _layer(x2, pos0, ssm0, conv0, ret0, w, final_gain, *, nblk, nc, q, s_blk, ret_heads, time_major):
    t = x2.shape[0]
    tm_ffn = min(FFN_TILE, t)
    steps = q // s_blk
    nseq = nblk * s_blk

    h = _ffn(x2, w["norm_ffn1"], w["ffn1_w1"], w["ffn1_w3"], w["ffn1_w2"], final_gain, final_norm=False, tm=tm_ffn)
    h_rows = h.reshape(steps, nseq, D_MODEL) if time_major else h

    y_ssm, ssm_new, conv_new = _ssd(
        h_rows, w["norm_mix"], w["w_ssd"], ssm0, conv0, w["conv_w"], w["conv_b"], w["dt_bias"], w["a_log"],
        w["d_exp"], w["ssm_norm"], w["head_expand"], nblk=nblk, nc=nc, q=q, s_blk=s_blk, time_major=time_major)
    cos, sin = _rope_tables(w["rope_inv"], nc * q, pos0, s_blk)
    y_ret, ret_new = _ret(h_rows, w["norm_mix"], w["w_q"], w["w_k"], w["w_v"], w["w_rg"], cos, sin, ret0,
                          w["ret_norm"], nblk=nblk, nc=nc, q=q, s_blk=s_blk, n_heads=ret_heads,
                          time_major=time_major)

    h = _merge(h, y_ssm.reshape(t, SSM_D_INNER), y_ret.reshape(t, RET_V), w["norm_mix"], w["w_gates"],
               w["w_branch_ssm"], w["w_branch_ret"], w["w_out"], tm=tm_ffn)
    y = _ffn(h, w["norm_ffn2"], w["ffn2_w1"], w["ffn2_w3"], w["ffn2_w2"], final_gain, final_norm=True, tm=tm_ffn)
    return y, ssm_new, conv_new, ret_new


def _prompt_layer(x, w, final_gain):
    nseq, seq_len, _ = x.shape
    q = math.gcd(seq_len, CHUNK)
    y, ssm_new, conv_new, ret_new = _layer(
        x.reshape(nseq * seq_len, D_MODEL), 0.0,
        jnp.zeros((1, SSM_HEADS * SSM_HEAD_DIM, SSM_STATE), F32),
        jnp.zeros((1, SSM_CONV - 1, SSM_CONV_DIM), F32),
        jnp.zeros((1, RET_HEADS * RET_QK_DIM, RET_V_DIM), F32),
        w, final_gain, nblk=nseq, nc=seq_len // q, q=q, s_blk=1, ret_heads=RET_HEADS, time_major=False)
    return (y.reshape(nseq, seq_len, D_MODEL), ssm_new.reshape(nseq, SSM_HEADS, SSM_HEAD_DIM, SSM_STATE), conv_new,
            ret_new.reshape(nseq, RET_HEADS, RET_QK_DIM, RET_V_DIM))


def _sample_layer(x, ssm0, conv0, ret0, w, final_gain):
    nseq, seq_len, _ = x.shape
    s_blk = SAMPLE_SEQS_PER_STEP
    assert nseq % s_blk == 0 and seq_len <= CHUNK
    nblk = nseq // s_blk
    kc = SSM_CONV - 1
    x_tm = jnp.transpose(x, (1, 0, 2)).reshape(seq_len * nseq, D_MODEL)
    conv0_tm = jnp.transpose(conv0.reshape(nblk, s_blk, kc, SSM_CONV_DIM), (0, 2, 1, 3)).reshape(
        nblk, kc * s_blk, SSM_CONV_DIM)
    y, ssm_new, conv_new, ret_new = _layer(
        x_tm, float(PAST_LEN), ssm0.reshape(nseq, SSM_HEADS * SSM_HEAD_DIM, SSM_STATE), conv0_tm,
        ret0.reshape(nseq, RET_HEADS * RET_QK_DIM, RET_V_DIM), w, final_gain,
        nblk=nblk, nc=1, q=s_blk * seq_len, s_blk=s_blk, ret_heads=SAMPLE_RET_HEADS_PER_STEP, time_major=True)
    conv_new = jnp.transpose(conv_new.reshape(nblk, kc, s_blk, SSM_CONV_DIM), (0, 2, 1, 3)).reshape(
        nseq, kc, SSM_CONV_DIM)
    return (jnp.transpose(y.reshape(seq_len, nseq, D_MODEL), (1, 0, 2)),
            ssm_new.reshape(nseq, SSM_HEADS, SSM_HEAD_DIM, SSM_STATE), conv_new,
            ret_new.reshape(nseq, RET_HEADS, RET_QK_DIM, RET_V_DIM))


def _prep_weights(norm_ffn1, ffn1_w1, ffn1_w3, ffn1_w2, norm_mix, w_in, conv_w, conv_b, dt_bias, a_log, ssm_d,
                  ssm_norm, ret_norm, w_branch_ssm, w_branch_ret, w_out, norm_ffn2, ffn2_w1, ffn2_w3, ffn2_w2):
    o_z = 0
    o_xbc = o_z + SSM_D_INNER
    o_dt = o_xbc + SSM_CONV_DIM
    o_q = o_dt + SSM_HEADS
    o_k = o_q + RET_QK
    o_v = o_k + RET_QK
    o_rg = o_v + RET_V
    o_ga = o_rg + RET_V
    o_end = o_ga + 2 * D_MODEL
    assert o_end == w_in.shape[1]

    w_in_bf = w_in.astype(BF16)

    def cols(a, b):
        return w_in_bf[:, a:b]

    assert (o_z, o_xbc, o_dt) == (SSDP_Z, SSDP_XBC, SSDP_DT) and SSDP_DIM <= o_end
    half = RET_QK_DIM // 2
    head_of_channel = jnp.arange(SSM_D_INNER, dtype=jnp.int32) // SSM_HEAD_DIM
    return {
        "norm_ffn1": norm_ffn1.reshape(1, -1), "ffn1_w1": ffn1_w1.astype(BF16), "ffn1_w3": ffn1_w3.astype(BF16),
        "ffn1_w2": ffn1_w2.astype(BF16),
        "norm_mix": norm_mix.reshape(1, -1), "w_ssd": w_in_bf,
        "w_q": cols(o_q, o_k), "w_k": cols(o_k, o_v), "w_v": cols(o_v, o_rg), "w_rg": cols(o_rg, o_ga),
        "w_gates": cols(o_ga, o_end),
        "conv_w": conv_w, "conv_b": conv_b.reshape(1, -1),
        "dt_bias": jnp.pad(dt_bias, (0, DT_PAD - SSM_HEADS)).reshape(1, -1),
        "a_log": jnp.pad(a_log, (0, DT_PAD - SSM_HEADS)).reshape(1, -1),
        "d_exp": jnp.repeat(ssm_d, SSM_HEAD_DIM).reshape(1, -1),
        "ssm_norm": ssm_norm.reshape(1, -1), "ret_norm": ret_norm.reshape(1, -1),
        "head_expand": jnp.tile(
            (head_of_channel[None, :] == jnp.arange(DT_PAD, dtype=jnp.int32)[:, None]).astype(BF16),
            (EXPAND_PIECES, 1)),
        "rope_inv": (ROPE_BASE ** (-jnp.arange(half, dtype=F32) / half)).reshape(1, half),
        "w_branch_ssm": w_branch_ssm.astype(BF16), "w_branch_ret": w_branch_ret.astype(BF16),
        "w_out": w_out.astype(BF16),
        "norm_ffn2": norm_ffn2.reshape(1, -1), "ffn2_w1": ffn2_w1.astype(BF16), "ffn2_w3": ffn2_w3.astype(BF16),
        "ffn2_w2": ffn2_w2.astype(BF16),
    }


def kernel(x_prompt, x_sample, state_ssm, state_conv, state_ret, norm_ffn1, ffn1_w1, ffn1_w3, ffn1_w2, norm_mix, w_in,
           conv_w, conv_b, dt_bias, a_log, ssm_d, ssm_norm, ret_norm, w_branch_ssm, w_branch_ret, w_out, norm_ffn2,
           ffn2_w1, ffn2_w3, ffn2_w2, norm_final):
    depth = norm_ffn1.shape[0]
    assert depth == 1, "the final RMSNorm is fused into the last layer's second FFN"
    w = _prep_weights(norm_ffn1[0], ffn1_w1[0], ffn1_w3[0], ffn1_w2[0], norm_mix[0], w_in[0], conv_w[0], conv_b[0],
                      dt_bias[0], a_log[0], ssm_d[0], ssm_norm[0], ret_norm[0], w_branch_ssm[0], w_branch_ret[0],
                      w_out[0], norm_ffn2[0], ffn2_w1[0], ffn2_w3[0], ffn2_w2[0])
    final_gain = norm_final.reshape(1, -1)
    yp, ssm_p, conv_p, ret_p = _prompt_layer(x_prompt, w, final_gain)
    ys, ssm_s, conv_s, ret_s = _sample_layer(x_sample, state_ssm[0], state_conv[0], state_ret[0], w, final_gain)
    return (yp, ys, ssm_p[None], conv_p[None], ret_p[None], ssm_s[None], conv_s[None], ret_s[None])
```

```python
import functools
import math

import jax
import jax.numpy as jnp
import numpy as np
from jax import lax
from jax.experimental import pallas as pl
from jax.experimental.pallas import tpu as pltpu

F32 = jnp.float32
BF16 = jnp.bfloat16

D_MODEL = 1024
D_FF = 2816
SSM_D_INNER = 2048
SSM_HEAD_DIM = 64
SSM_HEADS = 32
SSM_GROUPS = 4
SSM_STATE = 128
SSM_CONV = 4
SSM_BC = 2 * SSM_GROUPS * SSM_STATE
SSM_CONV_DIM = SSM_D_INNER + SSM_BC
RET_HEADS = 4
RET_QK_DIM = 256
RET_V_DIM = 512
RET_QK = RET_HEADS * RET_QK_DIM
RET_V = RET_HEADS * RET_V_DIM
ROPE_BASE = 10000.0
PAST_LEN = 16384
CHUNK = 128
NORM_EPS = 1e-6
GATED_NORM_EPS = 1e-5

V7X_SUBLANES = 8
V7X_LANES = 128
V7X_VMEM_BYTES = 64 * 1024 * 1024
VMEM_LIMIT = V7X_VMEM_BYTES - 8 * 1024 * 1024

DT_PAD = V7X_LANES
LOG_DT_FLOOR = -1e30
LOG2_E = math.log2(math.e)
EXPAND_PIECES = 2

SSDP_Z = 0
SSDP_XBC = SSM_D_INNER
SSDP_DT = SSM_D_INNER + SSM_CONV_DIM
SSDP_DIM = SSDP_DT + DT_PAD


def _rms(x, g, eps):
    return x * lax.rsqrt(jnp.mean(x * x, axis=-1, keepdims=True) + eps) * g


def _sigmoid(x):
    return 0.5 + 0.5 * jnp.tanh(0.5 * x)


def _silu(x):
    half = 0.5 * x
    return half + half * jnp.tanh(half)


def _softplus(x):
    return jnp.maximum(x, 0.0) + jnp.log1p(jnp.exp(-jnp.abs(x)))


def _dot(a, b):
    return jnp.dot(a.astype(BF16), b.astype(BF16), preferred_element_type=F32)


def _dot_nt(a, b):
    return lax.dot_general(a.astype(BF16), b.astype(BF16), (((1,), (1,)), ((), ())), preferred_element_type=F32)


def _dot_tn(a, b):
    return lax.dot_general(a.astype(BF16), b.astype(BF16), (((0,), (0,)), ((), ())), preferred_element_type=F32)


def _dot_split(a, b_f32, passes):
    pieces = []
    rem = b_f32
    for _ in range(passes):
        piece = rem.astype(BF16)
        pieces.append(piece)
        rem = rem - piece.astype(F32)
    return jnp.dot(jnp.concatenate([a] * passes, axis=1), jnp.concatenate(pieces, axis=0), preferred_element_type=F32)


def _dot_split_lhs(a_f32, b_stacked, passes):
    pieces = []
    rem = a_f32
    for _ in range(passes):
        piece = rem.astype(BF16)
        pieces.append(piece)
        rem = rem - piece.astype(F32)
    return jnp.dot(jnp.concatenate(pieces, axis=1), b_stacked, preferred_element_type=F32)


def _transpose_split(eye, x_f32, passes):
    pieces = []
    rem = x_f32
    for _ in range(passes):
        piece = rem.astype(BF16)
        pieces.append(piece)
        rem = rem - piece.astype(F32)
    return lax.dot_general(jnp.concatenate([eye] * passes, axis=1), jnp.concatenate(pieces, axis=1),
                           (((1,), (1,)), ((), ())), preferred_element_type=F32)


def _const_spec(shape):
    nd = len(shape)
    return pl.BlockSpec(shape, lambda *_: (0,) * nd, pipeline_mode=pl.Buffered(1))


def _params(n_grid_dims):
    return pltpu.CompilerParams(dimension_semantics=("arbitrary",) * n_grid_dims, vmem_limit_bytes=VMEM_LIMIT)


def _ffn_kernel(x_ref, g_ref, w1_ref, w3_ref, w2_ref, gf_ref, o_ref, *, final_norm):
    x = x_ref[...]
    xn = _rms(x, g_ref[...], NORM_EPS).astype(BF16)
    a = jnp.dot(xn, w1_ref[...], preferred_element_type=F32)
    b = jnp.dot(xn, w3_ref[...], preferred_element_type=F32)
    gated = (_silu(a) * b).astype(BF16)
    h = x + 0.5 * jnp.dot(gated, w2_ref[...], preferred_element_type=F32)
    if final_norm:
        h = _rms(h, gf_ref[...], NORM_EPS)
    o_ref[...] = h


def _ffn(x, g, w1, w3, w2, gf, *, final_norm, tm):
    t = x.shape[0]
    assert t % tm == 0
    return pl.pallas_call(
        functools.partial(_ffn_kernel, final_norm=final_norm),
        grid=(t // tm,),
        in_specs=[
            pl.BlockSpec((tm, D_MODEL), lambda i: (i, 0)),
            _const_spec((1, D_MODEL)),
            _const_spec((D_MODEL, D_FF)),
            _const_spec((D_MODEL, D_FF)),
            _const_spec((D_FF, D_MODEL)),
            _const_spec((1, D_MODEL)),
        ],
        out_specs=pl.BlockSpec((tm, D_MODEL), lambda i: (i, 0)),
        out_shape=jax.ShapeDtypeStruct((t, D_MODEL), F32),
        compiler_params=_params(1),
        name="ffn_final" if final_norm else "ffn",
    )(x, g, w1, w3, w2, gf)


def _rope_kernel(inv_ref, cos_ref, sin_ref, *, pos0, rows_per_pos):
    rows = cos_ref.shape[0]
    step = lax.broadcasted_iota(jnp.int32, (rows, V7X_LANES), 0) // rows_per_pos
    ang = (pos0 + step.astype(F32)) * inv_ref[...]
    cos_ref[...] = jnp.cos(ang)
    sin_ref[...] = jnp.sin(ang)


def _rope_tables(inv, rows, pos0, rows_per_pos):
    return pl.pallas_call(
        functools.partial(_rope_kernel, pos0=pos0, rows_per_pos=rows_per_pos),
        out_shape=[jax.ShapeDtypeStruct((rows, V7X_LANES), F32)] * 2,
        name="rope",
    )(inv)


def _chunk_rows(ref):
    return ref[...].reshape(-1, ref.shape[-1])


def _row_time_seq(shape, s_blk):
    r = lax.broadcasted_iota(jnp.int32, shape, 0)
    return r // s_blk, r % s_blk


def _pair_mask(q, s_blk):
    r = lax.broadcasted_iota(jnp.int32, (q, q), 0)
    c = lax.broadcasted_iota(jnp.int32, (q, q), 1)
    if s_blk == 1:
        return r >= c
    return jnp.logical_and(r // s_blk >= c // s_blk, r % s_blk == c % s_blk)


def _rows_spec(q, s_blk, nblk, nc, width, time_major, *, ahead=0, lead_axes=0, chunks_per_step=1):
    last = nblk * nc - 1

    def chunk_index(args):
        s, c = args[lead_axes], args[lead_axes + 1]
        return jnp.minimum((s * (nc // chunks_per_step) + c) * chunks_per_step + ahead, last)

    if time_major:
        assert nc == 1
        return pl.BlockSpec((q // s_blk, s_blk, width), lambda *a: (0, chunk_index(a), 0))
    return pl.BlockSpec((q, width), lambda *a: (chunk_index(a), 0))


def _project(h_ref, g_ref, w_refs):
    u = _rms(_chunk_rows(h_ref), g_ref[...], NORM_EPS).astype(BF16)
    return [jnp.dot(u, w_ref[...], preferred_element_type=F32) for w_ref in w_refs]


def _conv_pad_rows(s_blk):
    need = (SSM_CONV - 1) * s_blk
    return -(-need // V7X_SUBLANES) * V7X_SUBLANES


def _ssd_kernel(hc_ref, *refs, s_blk, nc, chunks_per_step):
    hn_refs, refs = refs[:chunks_per_step], refs[chunks_per_step:]
    n_in = 11
    y_ref = refs[n_in]
    for k, hn_ref in enumerate(hn_refs):
        if chunks_per_step == 1:
            y_view = y_ref
        else:
            q = y_ref.shape[0] // chunks_per_step
            y_view = y_ref.at[pl.ds(k * q, q)]
        _ssd_chunk(hc_ref, hn_ref, *refs[:n_in], y_view, *refs[n_in + 1:], s_blk=s_blk, nc=nc,
                   chunks_per_step=chunks_per_step, first_of_step=(k == 0))


def _ssd_chunk(hc_ref, hn_ref, gmix_ref, w_ref, h0_ref, conv0_ref, cw_ref, cb_ref, dtb_ref, alog_ref, dexp_ref,
               gn_ref, e_ref, y_ref, hout_ref, convout_ref, xp_scr, zdt_scr, next_scr, *, s_blk, nc, chunks_per_step,
               first_of_step):
    s = pl.program_id(0)
    c = pl.program_id(1)
    pad = _conv_pad_rows(s_blk)
    q = xp_scr.shape[0] - pad
    carry = (SSM_CONV - 1) * s_blk
    steps = q // s_blk
    hg = SSM_HEADS // SSM_GROUPS
    gw = hg * SSM_HEAD_DIM
    h_src = h0_ref if nc == 1 else hout_ref

    if first_of_step:
        @pl.when(jnp.logical_and(s == 0, c == 0))
        def _first():
            next_scr[...] = _project(hc_ref, gmix_ref, [w_ref])[0]

        @pl.when(c == 0)
        def _init():
            xp_scr[0:pad, :] = jnp.zeros((pad, SSM_CONV_DIM), F32)
            xp_scr[pad - carry:pad, :] = conv0_ref[0]
            if nc > 1:
                hout_ref[...] = h0_ref[...]

    xp_scr[pad:pad + q, :] = next_scr[:, SSDP_XBC:SSDP_DT]
    zdt_scr[:, 0:SSM_D_INNER] = next_scr[:, SSDP_Z:SSDP_XBC]
    zdt_scr[:, SSM_D_INNER:SSM_D_INNER + DT_PAD] = next_scr[:, SSDP_DT:SSDP_DIM]
    next_scr[...] = _project(hn_ref, gmix_ref, [w_ref])[0]

    z_gate = _silu(zdt_scr[:, 0:SSM_D_INNER])

    xall = xp_scr[...]
    conv = cb_ref[...] + xall[pad:pad + q, :] * cw_ref[SSM_CONV - 1:SSM_CONV, :]
    for j in range(1, SSM_CONV):
        d = j * s_blk
        if d % V7X_SUBLANES == 0:
            back = xall[pad - d:pad - d + q, :]
        else:
            back = pltpu.roll(xall, d, 0)[pad:pad + q, :]
        conv = conv + back * cw_ref[SSM_CONV - 1 - j:SSM_CONV - j, :]
    @pl.when(c == nc // chunks_per_step - 1)
    def _():
        convout_ref[0] = xp_scr[pad + q - carry:pad + q, :]

    if nc > 1:
        xp_scr[0:pad, :] = xp_scr[q:q + pad, :]
    dt_rows = zdt_scr[:, SSM_D_INNER:SSM_D_INNER + DT_PAD]
    xc = _silu(conv)
    xs = xc[:, 0:SSM_D_INNER]
    bm = xc[:, SSM_D_INNER:SSM_D_INNER + SSM_GROUPS * SSM_STATE]
    cm = xc[:, SSM_D_INNER + SSM_GROUPS * SSM_STATE:SSM_CONV_DIM]

    dt = _softplus(dt_rows + dtb_ref[...])
    a = -jnp.exp(alog_ref[...])
    visible = _pair_mask(q, s_blk)
    la = _dot_split(jnp.where(visible, 1.0, 0.0).astype(BF16), dt * (a * LOG2_E), 3)
    lsd = la - jnp.maximum(jnp.log2(dt), LOG_DT_FLOOR)
    eye_r = lax.broadcasted_iota(jnp.int32, (DT_PAD, DT_PAD), 0)
    eye_c = lax.broadcasted_iota(jnp.int32, (DT_PAD, DT_PAD), 1)
    eye = jnp.where(eye_r == eye_c, 1.0, 0.0).astype(BF16)
    la_t = _transpose_split(eye, la, 3)
    lsd_t = _transpose_split(eye, lsd, 3)
    la_last = la[q - s_blk:q, :]
    la_last_rows = la_last if s_blk == 1 else jnp.concatenate([la_last] * steps, axis=0)
    dec_t = jnp.exp2(la_t[:, q - s_blk:q])

    lane = lax.broadcasted_iota(jnp.int32, (q, 2 * SSM_HEAD_DIM), 1)
    first_head = lane < SSM_HEAD_DIM
    y_parts = []
    for g in range(SSM_GROUPS):
        cg = cm[:, g * SSM_STATE:(g + 1) * SSM_STATE]
        bg = bm[:, g * SSM_STATE:(g + 1) * SSM_STATE]
        cbg = _dot_nt(cg, bg)
        for pair in range(hg // 2):
            h0 = g * hg + 2 * pair
            ws = []
            for h in (h0, h0 + 1):
                seg = la[:, h:h + 1] - lsd_t[h:h + 1, :]
                ws.append(cbg * jnp.exp2(jnp.where(visible, seg, -jnp.inf)))
            xpair = xs[:, h0 * SSM_HEAD_DIM:(h0 + 2) * SSM_HEAD_DIM]
            rhs = jnp.concatenate([jnp.where(first_head, xpair, 0.0), jnp.where(first_head, 0.0, xpair)], axis=0)
            y_parts.append(_dot(jnp.concatenate(ws, axis=1), rhs))
    y = jnp.concatenate(y_parts, axis=1)

    _, seq_b = _row_time_seq((q, SSM_STATE), s_blk)
    y_state_parts = []
    for g in range(SSM_GROUPS):
        cg = cm[:, g * SSM_STATE:(g + 1) * SSM_STATE]
        if s_blk == 1:
            c_seqs, h_seqs = cg, h_src[0, g * gw:(g + 1) * gw, :]
        else:
            c_seqs = jnp.concatenate([jnp.where(seq_b == b, cg, 0.0) for b in range(s_blk)], axis=1)
            h_seqs = jnp.concatenate([h_src[b, g * gw:(g + 1) * gw, :].astype(BF16) for b in range(s_blk)], axis=1)
        y_state_parts.append(_dot_nt(c_seqs, h_seqs))
    y_state = jnp.concatenate(y_state_parts, axis=1)

    e = e_ref[...]
    y = y + y_state * _dot_split_lhs(jnp.exp2(la), e, EXPAND_PIECES) + xs * dexp_ref[...]
    y = y * z_gate
    y = _rms(y, gn_ref[...], GATED_NORM_EPS)
    y_ref[...] = y.reshape(y_ref.shape).astype(y_ref.dtype)

    tail = jnp.exp2(la_last_rows - lsd)
    xt = (xs * _dot_split_lhs(tail, e, EXPAND_PIECES)).astype(BF16)
    dec =[jnp.broadcast_to(dec_t[:, b:b + 1], (DT_PAD, SSM_STATE)) for b in range(s_blk)]
    for g in range(SSM_GROUPS):
        bg = bm[:, g * SSM_STATE:(g + 1) * SSM_STATE]
        bg_seqs = bg if s_blk == 1 else jnp.concatenate([jnp.where(seq_b == b, bg, 0.0) for b in range(s_blk)], axis=1)
        upd = _dot_tn(xt[:, g * gw:(g + 1) * gw], bg_seqs)
        for b in range(s_blk):
            for hh in range(hg):
                h = g * hg + hh
                r0 = h * SSM_HEAD_DIM
                hout_ref[b, r0:r0 + SSM_HEAD_DIM, :] = (
                    h_src[b, r0:r0 + SSM_HEAD_DIM, :] * dec[b][h:h + 1, :]
                    + upd[hh * SSM_HEAD_DIM:(hh + 1) * SSM_HEAD_DIM, b * SSM_STATE:(b + 1) * SSM_STATE])


def _ssd(h, gmix, w, h0, conv0, cw, cb, dtb, alog, dexp, gn, e, *, nblk, nc, q, s_blk, time_major):
    rows = SSM_HEADS * SSM_HEAD_DIM
    carry = (SSM_CONV - 1) * s_blk
    per_block0 = h0.shape[0] == nblk * s_blk
    assert per_block0 or (h0.shape[0] == s_blk and conv0.shape[0] == 1)
    cps = SSD_CHUNKS_PER_STEP if (nc % SSD_CHUNKS_PER_STEP == 0 and not time_major) else 1
    ncs = nc // cps
    spec = functools.partial(_rows_spec, q, s_blk, nblk, nc, time_major=time_major, chunks_per_step=cps)
    if time_major:
        y_spec = spec(SSM_D_INNER)
        y_shape = (q // s_blk, nblk * s_blk, SSM_D_INNER)
    else:
        y_spec = pl.BlockSpec((cps * q, SSM_D_INNER), lambda s, c: (s * ncs + c, 0))
        y_shape = (nblk * nc * q, SSM_D_INNER)
    return pl.pallas_call(
        functools.partial(_ssd_kernel, s_blk=s_blk, nc=nc, chunks_per_step=cps),
        grid=(nblk, ncs),
        in_specs=[spec(D_MODEL, ahead=k) for k in range(cps + 1)] + [
            _const_spec((1, D_MODEL)),
            _const_spec((D_MODEL, SSDP_DIM)),
            pl.BlockSpec((s_blk, rows, SSM_STATE), lambda s, c: (s if per_block0 else 0, 0, 0)),
            pl.BlockSpec((1, carry, SSM_CONV_DIM), lambda s, c: (s if per_block0 else 0, 0, 0)),
            _const_spec((SSM_CONV, SSM_CONV_DIM)),
            _const_spec((1, SSM_CONV_DIM)),
            _const_spec((1, DT_PAD)),
            _const_spec((1, DT_PAD)),
            _const_spec((1, SSM_D_INNER)),
            _const_spec((1, SSM_D_INNER)),
            _const_spec((EXPAND_PIECES * DT_PAD, SSM_D_INNER)),
        ],
        out_specs=[
            y_spec,
            pl.BlockSpec((s_blk, rows, SSM_STATE), lambda s, c: (s, 0, 0)),
            pl.BlockSpec((1, carry, SSM_CONV_DIM), lambda s, c: (s, 0, 0)),
        ],
        out_shape=[
            jax.ShapeDtypeStruct(y_shape, BF16),
            jax.ShapeDtypeStruct((nblk * s_blk, rows, SSM_STATE), F32),
            jax.ShapeDtypeStruct((nblk, carry, SSM_CONV_DIM), F32),
        ],
        scratch_shapes=[
            pltpu.VMEM((_conv_pad_rows(s_blk) + q, SSM_CONV_DIM), F32),
            pltpu.VMEM((q, SSM_D_INNER + DT_PAD), F32),
            pltpu.VMEM((q, SSDP_DIM), F32),
        ],
        compiler_params=_params(2),
        name="ssd",
    )(*([h] * (cps + 1)), gmix, w, h0, conv0, cw, cb, dtb, alog, dexp, gn, e)


def _ret_log_decay(h):
    return float(np.log1p(-np.exp2(-5.0 - h)))


def _ret_project(h_ref, gmix_ref, w_refs, dst):
    pq, pk, pv, pg = _project(h_ref, gmix_ref, w_refs)
    dst[0][...] = pq
    dst[1][...] = pk
    dst[2][...] = pv.astype(BF16)
    dst[3][...] = pg


def _ret_kernel(*refs, s_blk, nc, n_heads, chunks_per_step):
    io_refs, sets = refs[:-8], (refs[-8:-4], refs[-4:])
    hc_ref, hn_refs, io_refs = io_refs[0], io_refs[1:1 + chunks_per_step], io_refs[1 + chunks_per_step:]
    gmix_ref = io_refs[0]
    w_refs = io_refs[1:5]
    cos_ref, sin_ref = io_refs[5:7]
    tables = io_refs[7:10]
    s0_ref, gn_ref, y_ref, sout_ref = io_refs[10:14]
    s = pl.program_id(1)
    c = pl.program_id(2)
    q = cos_ref.shape[0] // chunks_per_step

    @pl.when(jnp.logical_and(s == 0, c == 0))
    def _first():
        _ret_project(hc_ref, gmix_ref, w_refs, sets[0])

    if nc > 1:
        @pl.when(c == 0)
        def _init():
            sout_ref[...] = s0_ref[...]

    def chunk(k, hn_ref, cur_set, next_set):
        rows = pl.ds(k * q, q)
        y_view = y_ref if chunks_per_step == 1 else y_ref.at[rows]
        _ret_step(hn_ref, gmix_ref, *w_refs, cos_ref.at[rows], sin_ref.at[rows], *tables, s0_ref, gn_ref, y_view,
                  sout_ref, cur_set, next_set, s_blk=s_blk, nc=nc, n_heads=n_heads)

    if chunks_per_step % 2 == 0:
        for k, hn_ref in enumerate(hn_refs):
            chunk(k, hn_ref, sets[k % 2], sets[(k + 1) % 2])
    else:
        assert chunks_per_step == 1
        parity = (s * nc + c) % 2

        @pl.when(parity == 0)
        def _even():
            chunk(0, hn_refs[0], sets[0], sets[1])

        @pl.when(parity == 1)
        def _odd():
            chunk(0, hn_refs[0], sets[1], sets[0])


def _ret_step(hn_ref, gmix_ref, wq_ref, wk_ref, wv_ref, wg_ref, cos_ref, sin_ref, dmat_ref, qdec_ref, kdec_ref,
              s0_ref, gn_ref, y_ref, sout_ref, cur_set, next_set, *, s_blk, nc, n_heads):
    q_scr, k_scr, v_scr, g_scr = cur_set
    hb = pl.program_id(0)
    q = cos_ref.shape[0]
    steps = q // s_blk
    half = RET_QK_DIM // 2
    s_src = s0_ref if nc == 1 else sout_ref

    _ret_project(hn_ref, gmix_ref, [wq_ref, wk_ref, wv_ref, wg_ref], next_set)

    qq = q_scr[...]
    kk = k_scr[...]
    vv = v_scr[...]
    rg = g_scr[...]
    cos = cos_ref[...]
    sin = sin_ref[...]
    _, seq_qk = _row_time_seq((q, RET_QK_DIM), s_blk)

    def rot(t, h):
        t1 = t[:, h * RET_QK_DIM:h * RET_QK_DIM + half]
        t2 = t[:, h * RET_QK_DIM + half:(h + 1) * RET_QK_DIM]
        return jnp.concatenate([t1 * cos - t2 * sin, t1 * sin + t2 * cos], axis=1)

    y_parts = []
    for h in range(n_heads):
        if n_heads == RET_HEADS:
            chunk_dec = math.exp(steps * _ret_log_decay(h))
        else:
            head = hb * n_heads + h
            chunk_dec = jnp.float32(math.exp(steps * _ret_log_decay(0)))
            for hh in range(1, RET_HEADS):
                chunk_dec = jnp.where(head == hh, jnp.float32(math.exp(steps * _ret_log_decay(hh))), chunk_dec)
        dmat = dmat_ref[h]
        q_dec = jnp.concatenate([qdec_ref[h]] * 2, axis=1)
        k_dec = jnp.concatenate([kdec_ref[h]] * 2, axis=1)
        qr = rot(qq, h)
        kr = rot(kk, h) * (RET_QK_DIM ** -0.5)
        vh = vv[:, h * RET_V_DIM:(h + 1) * RET_V_DIM]
        scores = _dot_nt(qr, kr) * dmat
        y = _dot(scores, vh)
        qd = qr * q_dec
        kd = kr * k_dec
        r0 = h * RET_QK_DIM
        for b in range(s_blk):
            qd_b = (qd if s_blk == 1 else jnp.where(seq_qk == b, qd, 0.0)).astype(BF16)
            kd_b = (kd if s_blk == 1 else jnp.where(seq_qk == b, kd, 0.0)).astype(BF16)
            sh = s_src[b, r0:r0 + RET_QK_DIM, :]
            y = y + _dot(qd_b, sh)
            sout_ref[b, r0:r0 + RET_QK_DIM, :] = sh * chunk_dec + _dot_tn(kd_b, vh)
        g = gn_ref[:, h * RET_V_DIM:(h + 1) * RET_V_DIM]
        y = _rms(y, g, NORM_EPS) * _silu(rg[:, h * RET_V_DIM:(h + 1) * RET_V_DIM])
        y_parts.append(y)
    y = y_parts[0] if n_heads == 1 else jnp.concatenate(y_parts, axis=1)
    y_ref[...] = y.reshape(y_ref.shape).astype(y_ref.dtype)


def _ret_decay_tables(q, s_blk):
    t = np.arange(q) // s_blk
    seq = np.arange(q) % s_blk
    steps = q // s_blk
    rel = t[:, None] - t[None, :]
    visible = (rel >= 0) & (seq[:, None] == seq[None, :])
    dmat, qdec, kdec = [], [], []
    for h in range(RET_HEADS):
        lg = _ret_log_decay(h)
        dmat.append(np.where(visible, np.exp(np.maximum(rel, 0) * lg), 0.0))
        qdec.append(np.broadcast_to(np.exp((t + 1.0) * lg)[:, None], (q, V7X_LANES)))
        kdec.append(np.broadcast_to(np.exp((steps - 1.0 - t) * lg)[:, None], (q, V7X_LANES)))
    return tuple(jnp.asarray(np.stack(a), F32) for a in (dmat, qdec, kdec))


def _ret(h, gmix, wq, wk, wv, wg, cos, sin, s0, gn, *, nblk, nc, q, s_blk, n_heads, time_major):
    dmat, qdec, kdec = _ret_decay_tables(q, s_blk)
    nhb = RET_HEADS // n_heads
    qk_w = n_heads * RET_QK_DIM
    v_w = n_heads * RET_V_DIM
    per_block0 = s0.shape[0] == nblk * s_blk
    assert per_block0 or s0.shape[0] == s_blk
    cps = RET_CHUNKS_PER_STEP if (nc % RET_CHUNKS_PER_STEP == 0 and not time_major) else 1
    ncs = nc // cps
    rows = functools.partial(_rows_spec, q, s_blk, nblk, nc, D_MODEL, time_major, lead_axes=1, chunks_per_step=cps)

    def head_cols(nrows, width):
        if nhb == 1:
            return _const_spec((nrows, width))
        return pl.BlockSpec((nrows, width), lambda hb, s, c: (0, hb))

    if time_major:
        y_spec = pl.BlockSpec((q // s_blk, s_blk, v_w), lambda hb, s, c: (0, s, hb))
        y_shape = (q // s_blk, nblk * s_blk, RET_V)
    else:
        y_spec = pl.BlockSpec((cps * q, v_w), lambda hb, s, c: (s * ncs + c, hb))
        y_shape = (nblk * nc * q, RET_V)
    return pl.pallas_call(
        functools.partial(_ret_kernel, s_blk=s_blk, nc=nc, n_heads=n_heads, chunks_per_step=cps),
        grid=(nhb, nblk, ncs),
        in_specs=[rows(ahead=k) for k in range(cps + 1)] + [
            _const_spec((1, D_MODEL)),
            head_cols(D_MODEL, qk_w),
            head_cols(D_MODEL, qk_w),
            head_cols(D_MODEL, v_w),
            head_cols(D_MODEL, v_w),
            pl.BlockSpec((cps * q, V7X_LANES), lambda hb, s, c: (c, 0)),
            pl.BlockSpec((cps * q, V7X_LANES), lambda hb, s, c: (c, 0)),
            pl.BlockSpec((n_heads, q, q), lambda hb, s, c: (hb, 0, 0)),
            pl.BlockSpec((n_heads, q, V7X_LANES), lambda hb, s, c: (hb, 0, 0)),
            pl.BlockSpec((n_heads, q, V7X_LANES), lambda hb, s, c: (hb, 0, 0)),
            pl.BlockSpec((s_blk, qk_w, RET_V_DIM), lambda hb, s, c: (s if per_block0 else 0, hb, 0)),
            head_cols(1, v_w),
        ],
        out_specs=[
            y_spec,
            pl.BlockSpec((s_blk, qk_w, RET_V_DIM), lambda hb, s, c: (s, hb, 0)),
        ],
        out_shape=[
            jax.ShapeDtypeStruct(y_shape, BF16),
            jax.ShapeDtypeStruct((nblk * s_blk, RET_HEADS * RET_QK_DIM, RET_V_DIM), F32),
        ],
        scratch_shapes=[
            pltpu.VMEM((q, qk_w), F32),
            pltpu.VMEM((q, qk_w), F32),
            pltpu.VMEM((q, v_w), BF16),
            pltpu.VMEM((q, v_w), F32),
        ] * 2,
        compiler_params=_params(3),
        name="ret",
    )(*([h] * (cps + 1)), gmix, wq, wk, wv, wg, cos, sin, dmat, qdec, kdec, s0, gn)


def _merge_kernel(h_ref, ys_ref, yr_ref, g_ref, wg_ref, wbs_ref, wbr_ref, wo_ref, o_ref):
    h = h_ref[...]
    u = _rms(h, g_ref[...], NORM_EPS).astype(BF16)
    gates = jnp.dot(u, wg_ref[...], preferred_element_type=F32)
    branch_ssm = jnp.dot(ys_ref[...], wbs_ref[...], preferred_element_type=F32)
    branch_ret = jnp.dot(yr_ref[...], wbr_ref[...], preferred_element_type=F32)
    merged = _sigmoid(gates[:, 0:D_MODEL]) * branch_ssm + _sigmoid(gates[:, D_MODEL:2 * D_MODEL]) * branch_ret
    o_ref[...] = h + jnp.dot(merged.astype(BF16), wo_ref[...], preferred_element_type=F32)


def _merge(h, ys, yr, g, wg, wbs, wbr, wo, *, tm):
    t = h.shape[0]
    assert t % tm == 0
    return pl.pallas_call(
        _merge_kernel,
        grid=(t // tm,),
        in_specs=[
            pl.BlockSpec((tm, D_MODEL), lambda i: (i, 0)),
            pl.BlockSpec((tm, SSM_D_INNER), lambda i: (i, 0)),
            pl.BlockSpec((tm, RET_V), lambda i: (i, 0)),
            _const_spec((1, D_MODEL)),
            _const_spec((D_MODEL, 2 * D_MODEL)),
            _const_spec((SSM_D_INNER, D_MODEL)),
            _const_spec((RET_V, D_MODEL)),
            _const_spec((D_MODEL, D_MODEL)),
        ],
        out_specs=pl.BlockSpec((tm, D_MODEL), lambda i: (i, 0)),
        out_shape=jax.ShapeDtypeStruct((t, D_MODEL), F32),
        compiler_params=_params(1),
        name="merge",
    )(h, ys, yr, g, wg, wbs, wbr, wo)


FFN_TILE = 512
SAMPLE_SEQS_PER_STEP = V7X_SUBLANES
SSD_CHUNKS_PER_STEP = 4
RET_CHUNKS_PER_STEP = 4
SAMPLE_RET_HEADS_PER_STEP = 2


def _layer(x2, pos0, ssm0, conv0, ret0, w, final_gain, *, nblk, nc, q, s_blk, ret_heads, time_major):
    t = x2.shape[0]
    tm_ffn = min(FFN_TILE, t)
    steps = q // s_blk
    nseq = nblk * s_blk

    h = _ffn(x2, w["norm_ffn1"], w["ffn1_w1"], w["ffn1_w3"], w["ffn1_w2"], final_gain, final_norm=False, tm=tm_ffn)
    h_rows = h.reshape(steps, nseq, D_MODEL) if time_major else h

    y_ssm, ssm_new, conv_new = _ssd(
        h_rows, w["norm_mix"], w["w_ssd"], ssm0, conv0, w["conv_w"], w["conv_b"], w["dt_bias"], w["a_log"],
        w["d_exp"], w["ssm_norm"], w["head_expand"], nblk=nblk, nc=nc, q=q, s_blk=s_blk, time_major=time_major)
    cos, sin = _rope_tables(w["rope_inv"], nc * q, pos0, s_blk)
    y_ret, ret_new = _ret(h_rows, w["norm_mix"], w["w_q"], w["w_k"], w["w_v"], w["w_rg"], cos, sin, ret0,
                          w["ret_norm"], nblk=nblk, nc=nc, q=q, s_blk=s_blk, n_heads=ret_heads,
                          time_major=time_major)

    h = _merge(h, y_ssm.reshape(t, SSM_D_INNER), y_ret.reshape(t, RET_V), w["norm_mix"], w["w_gates"],
               w["w_branch_ssm"], w["w_branch_ret"], w["w_out"], tm=tm_ffn)
    y = _ffn(h, w["norm_ffn2"], w["ffn2_w1"], w["ffn2_w3"], w["ffn2_w2"], final_gain, final_norm=True, tm=tm_ffn)
    return y, ssm_new, conv_new, ret_new


def _prompt_layer(x, w, final_gain):
    nseq, seq_len, _ = x.shape
    q = math.gcd(seq_len, CHUNK)
    y, ssm_new, conv_new, ret_new = _layer(
        x.reshape(nseq * seq_len, D_MODEL), 0.0,
        jnp.zeros((1, SSM_HEADS * SSM_HEAD_DIM, SSM_STATE), F32),
        jnp.zeros((1, SSM_CONV - 1, SSM_CONV_DIM), F32),
        jnp.zeros((1, RET_HEADS * RET_QK_DIM, RET_V_DIM), F32),
        w, final_gain, nblk=nseq, nc=seq_len // q, q=q, s_blk=1, ret_heads=RET_HEADS, time_major=False)
    return (y.reshape(nseq, seq_len, D_MODEL), ssm_new.reshape(nseq, SSM_HEADS, SSM_HEAD_DIM, SSM_STATE), conv_new,
            ret_new.reshape(nseq, RET_HEADS, RET_QK_DIM, RET_V_DIM))


def _sample_layer(x, ssm0, conv0, ret0, w, final_gain):
    nseq, seq_len, _ = x.shape
    s_blk = SAMPLE_SEQS_PER_STEP
    assert nseq % s_blk == 0 and seq_len <= CHUNK
    nblk = nseq // s_blk
    kc = SSM_CONV - 1
    x_tm = jnp.transpose(x, (1, 0, 2)).reshape(seq_len * nseq, D_MODEL)
    conv0_tm = jnp.transpose(conv0.reshape(nblk, s_blk, kc, SSM_CONV_DIM), (0, 2, 1, 3)).reshape(
        nblk, kc * s_blk, SSM_CONV_DIM)
    y, ssm_new, conv_new, ret_new = _layer(
        x_tm, float(PAST_LEN), ssm0.reshape(nseq, SSM_HEADS * SSM_HEAD_DIM, SSM_STATE), conv0_tm,
        ret0.reshape(nseq, RET_HEADS * RET_QK_DIM, RET_V_DIM), w, final_gain,
        nblk=nblk, nc=1, q=s_blk * seq_len, s_blk=s_blk, ret_heads=SAMPLE_RET_HEADS_PER_STEP, time_major=True)
    conv_new = jnp.transpose(conv_new.reshape(nblk, kc, s_blk, SSM_CONV_DIM), (0, 2, 1, 3)).reshape(
        nseq, kc, SSM_CONV_DIM)
    return (jnp.transpose(y.reshape(seq_len, nseq, D_MODEL), (1, 0, 2)),
            ssm_new.reshape(nseq, SSM_HEADS, SSM_HEAD_DIM, SSM_STATE), conv_new,
            ret_new.reshape(nseq, RET_HEADS, RET_QK_DIM, RET_V_DIM))


def _prep_weights(norm_ffn1, ffn1_w1, ffn1_w3, ffn1_w2, norm_mix, w_in, conv_w, conv_b, dt_bias, a_log, ssm_d,
                  ssm_norm, ret_norm, w_branch_ssm, w_branch_ret, w_out, norm_ffn2, ffn2_w1, ffn2_w3, ffn2_w2):
    o_z = 0
    o_xbc = o_z + SSM_D_INNER
    o_dt = o_xbc + SSM_CONV_DIM
    o_q = o_dt + SSM_HEADS
    o_k = o_q + RET_QK
    o_v = o_k + RET_QK
    o_rg = o_v + RET_V
    o_ga = o_rg + RET_V
    o_end = o_ga + 2 * D_MODEL
    assert o_end == w_in.shape[1]

    w_in_bf = w_in.astype(BF16)

    def cols(a, b):
        return w_in_bf[:, a:b]

    assert (o_z, o_xbc, o_dt) == (SSDP_Z, SSDP_XBC, SSDP_DT) and SSDP_DIM <= o_end
    half = RET_QK_DIM // 2
    head_of_channel = jnp.arange(SSM_D_INNER, dtype=jnp.int32) // SSM_HEAD_DIM
    return {
        "norm_ffn1": norm_ffn1.reshape(1, -1), "ffn1_w1": ffn1_w1.astype(BF16), "ffn1_w3": ffn1_w3.astype(BF16),
        "ffn1_w2": ffn1_w2.astype(BF16),
        "norm_mix": norm_mix.reshape(1, -1), "w_ssd": w_in_bf,
        "w_q": cols(o_q, o_k), "w_k": cols(o_k, o_v), "w_v": cols(o_v, o_rg), "w_rg": cols(o_rg, o_ga),
        "w_gates": cols(o_ga, o_end),
        "conv_w": conv_w, "conv_b": conv_b.reshape(1, -1),
        "dt_bias": jnp.pad(dt_bias, (0, DT_PAD - SSM_HEADS)).reshape(1, -1),
        "a_log": jnp.pad(a_log, (0, DT_PAD - SSM_HEADS)).reshape(1, -1),
        "d_exp": jnp.repeat(ssm_d, SSM_HEAD_DIM).reshape(1, -1),
        "ssm_norm": ssm_norm.reshape(1, -1), "ret_norm": ret_norm.reshape(1, -1),
        "head_expand": jnp.tile(
            (head_of_channel[None, :] == jnp.arange(DT_PAD, dtype=jnp.int32)[:, None]).astype(BF16),
            (EXPAND_PIECES, 1)),
        "rope_inv": (ROPE_BASE ** (-jnp.arange(half, dtype=F32) / half)).reshape(1, half),
        "w_branch_ssm": w_branch_ssm.astype(BF16), "w_branch_ret": w_branch_ret.astype(BF16),
        "w_out": w_out.astype(BF16),
        "norm_ffn2": norm_ffn2.reshape(1, -1), "ffn2_w1": ffn2_w1.astype(BF16), "ffn2_w3": ffn2_w3.astype(BF16),
        "ffn2_w2": ffn2_w2.astype(BF16),
    }


def kernel(x_prompt, x_sample, state_ssm, state_conv, state_ret, norm_ffn1, ffn1_w1, ffn1_w3, ffn1_w2, norm_mix, w_in,
           conv_w, conv_b, dt_bias, a_log, ssm_d, ssm_norm, ret_norm, w_branch_ssm, w_branch_ret, w_out, norm_ffn2,
           ffn2_w1, ffn2_w3, ffn2_w2, norm_final):
    depth = norm_ffn1.shape[0]
    assert depth == 1, "the final RMSNorm is fused into the last layer's second FFN"
    w = _prep_weights(norm_ffn1[0], ffn1_w1[0], ffn1_w3[0], ffn1_w2[0], norm_mix[0], w_in[0], conv_w[0], conv_b[0],
                      dt_bias[0], a_log[0], ssm_d[0], ssm_norm[0], ret_norm[0], w_branch_ssm[0], w_branch_ret[0],
                      w_out[0], norm_ffn2[0], ffn2_w1[0], ffn2_w3[0], ffn2_w2[0])
    final_gain = norm_final.reshape(1, -1)
    yp, ssm_p, conv_p, ret_p = _prompt_layer(x_prompt, w, final_gain)
    ys, ssm_s, conv_s, ret_s = _sample_layer(x_sample, state_ssm[0], state_conv[0], state_ret[0], w, final_gain)
    return (yp, ys, ssm_p[None], conv_p[None], ret_p[None], ssm_s[None], conv_s[None], ret_s[None])
```

```python
import functools
import math

import jax
import jax.numpy as jnp
import numpy as np
from jax import lax
from jax.experimental import pallas as pl
from jax.experimental.pallas import tpu as pltpu

F32 = jnp.float32
BF16 = jnp.bfloat16

D_MODEL = 1024
D_FF = 2816
SSM_D_INNER = 2048
SSM_HEAD_DIM = 64
SSM_HEADS = 32
SSM_GROUPS = 4
SSM_STATE = 128
SSM_CONV = 4
SSM_BC = 2 * SSM_GROUPS * SSM_STATE
SSM_CONV_DIM = SSM_D_INNER + SSM_BC
RET_HEADS = 4
RET_QK_DIM = 256
RET_V_DIM = 512
RET_QK = RET_HEADS * RET_QK_DIM
RET_V = RET_HEADS * RET_V_DIM
ROPE_BASE = 10000.0
PAST_LEN = 16384
CHUNK = 128
NORM_EPS = 1e-6
GATED_NORM_EPS = 1e-5

V7X_SUBLANES = 8
V7X_LANES = 128
V7X_VMEM_BYTES = 64 * 1024 * 1024
VMEM_LIMIT = V7X_VMEM_BYTES - 8 * 1024 * 1024

DT_PAD = V7X_LANES
LOG_DT_FLOOR = -1e30
LOG2_E = math.log2(math.e)
EXPAND_PIECES = 2

SSDP_Z = 0
SSDP_XBC = SSM_D_INNER
SSDP_DT = SSM_D_INNER + SSM_CONV_DIM
SSDP_DIM = SSDP_DT + DT_PAD


def _rms(x, g, eps):
    return x * lax.rsqrt(jnp.mean(x * x, axis=-1, keepdims=True) + eps) * g


def _sigmoid(x):
    return 0.5 + 0.5 * jnp.tanh(0.5 * x)


def _silu(x):
    half = 0.5 * x
    return half + half * jnp.tanh(half)


def _softplus(x):
    return jnp.maximum(x, 0.0) + jnp.log1p(jnp.exp(-jnp.abs(x)))


def _dot(a, b):
    return jnp.dot(a.astype(BF16), b.astype(BF16), preferred_element_type=F32)


def _dot_nt(a, b):
    return lax.dot_general(a.astype(BF16), b.astype(BF16), (((1,), (1,)), ((), ())), preferred_element_type=F32)


def _dot_tn(a, b):
    return lax.dot_general(a.astype(BF16), b.astype(BF16), (((0,), (0,)), ((), ())), preferred_element_type=F32)


def _dot_split(a, b_f32, passes):
    pieces = []
    rem = b_f32
    for _ in range(passes):
        piece = rem.astype(BF16)
        pieces.append(piece)
        rem = rem - piece.astype(F32)
    return jnp.dot(jnp.concatenate([a] * passes, axis=1), jnp.concatenate(pieces, axis=0), preferred_element_type=F32)


def _dot_split_tn(x_f32, m, passes):
    pieces = []
    rem = x_f32
    for _ in range(passes):
        piece = rem.astype(BF16)
        pieces.append(piece)
        rem = rem - piece.astype(F32)
    n = x_f32.shape[1]
    stacked = _dot_tn(jnp.concatenate(pieces, axis=1), m)
    out = stacked[0:n]
    for p in range(1, passes):
        out = out + stacked[p * n:(p + 1) * n]
    return out


def _dot_split_lhs(a_f32, b_stacked, passes):
    pieces = []
    rem = a_f32
    for _ in range(passes):
        piece = rem.astype(BF16)
        pieces.append(piece)
        rem = rem - piece.astype(F32)
    return jnp.dot(jnp.concatenate(pieces, axis=1), b_stacked, preferred_element_type=F32)


def _transpose_split(eye, x_f32, passes):
    pieces = []
    rem = x_f32
    for _ in range(passes):
        piece = rem.astype(BF16)
        pieces.append(piece)
        rem = rem - piece.astype(F32)
    return lax.dot_general(jnp.concatenate([eye] * passes, axis=1), jnp.concatenate(pieces, axis=1),
                           (((1,), (1,)), ((), ())), preferred_element_type=F32)


def _const_spec(shape):
    nd = len(shape)
    return pl.BlockSpec(shape, lambda *_: (0,) * nd, pipeline_mode=pl.Buffered(1))


def _params(n_grid_dims):
    return pltpu.CompilerParams(dimension_semantics=("arbitrary",) * n_grid_dims, vmem_limit_bytes=VMEM_LIMIT)


def _ffn_kernel(x_ref, g_ref, w1_ref, w3_ref, w2_ref, gf_ref, o_ref, *, final_norm):
    x = x_ref[...]
    xn = _rms(x, g_ref[...], NORM_EPS).astype(BF16)
    a = jnp.dot(xn, w1_ref[...], preferred_element_type=F32)
    b = jnp.dot(xn, w3_ref[...], preferred_element_type=F32)
    gated = (_silu(a) * b).astype(BF16)
    h = x + 0.5 * jnp.dot(gated, w2_ref[...], preferred_element_type=F32)
    if final_norm:
        h = _rms(h, gf_ref[...], NORM_EPS)
    o_ref[...] = h


def _ffn(x, g, w1, w3, w2, gf, *, final_norm, tm):
    t = x.shape[0]
    assert t % tm == 0
    return pl.pallas_call(
        functools.partial(_ffn_kernel, final_norm=final_norm),
        grid=(t // tm,),
        in_specs=[
            pl.BlockSpec((tm, D_MODEL), lambda i: (i, 0)),
            _const_spec((1, D_MODEL)),
            _const_spec((D_MODEL, D_FF)),
            _const_spec((D_MODEL, D_FF)),
            _const_spec((D_FF, D_MODEL)),
            _const_spec((1, D_MODEL)),
        ],
        out_specs=pl.BlockSpec((tm, D_MODEL), lambda i: (i, 0)),
        out_shape=jax.ShapeDtypeStruct((t, D_MODEL), F32),
        compiler_params=_params(1),
        name="ffn_final" if final_norm else "ffn",
    )(x, g, w1, w3, w2, gf)


def _rope_kernel(inv_ref, cos_ref, sin_ref, *, pos0, rows_per_pos):
    rows = cos_ref.shape[0]
    step = lax.broadcasted_iota(jnp.int32, (rows, V7X_LANES), 0) // rows_per_pos
    ang = (pos0 + step.astype(F32)) * inv_ref[...]
    cos_ref[...] = jnp.cos(ang)
    sin_ref[...] = jnp.sin(ang)


def _rope_tables(inv, rows, pos0, rows_per_pos):
    return pl.pallas_call(
        functools.partial(_rope_kernel, pos0=pos0, rows_per_pos=rows_per_pos),
        out_shape=[jax.ShapeDtypeStruct((rows, V7X_LANES), F32)] * 2,
        name="rope",
    )(inv)


def _chunk_rows(ref):
    return ref[...].reshape(-1, ref.shape[-1])


def _row_time_seq(shape, s_blk):
    r = lax.broadcasted_iota(jnp.int32, shape, 0)
    return r // s_blk, r % s_blk


def _pair_mask(q, s_blk, transposed=False):
    r = lax.broadcasted_iota(jnp.int32, (q, q), 1 if transposed else 0)
    c = lax.broadcasted_iota(jnp.int32, (q, q), 0 if transposed else 1)
    if s_blk == 1:
        return r >= c
    return jnp.logical_and(r // s_blk >= c // s_blk, r % s_blk == c % s_blk)


def _rows_spec(q, s_blk, nblk, nc, width, time_major, *, ahead=0, lead_axes=0, chunks_per_step=1):
    last = nblk * nc - 1

    def chunk_index(args):
        s, c = args[lead_axes], args[lead_axes + 1]
        return jnp.minimum((s * (nc // chunks_per_step) + c) * chunks_per_step + ahead, last)

    if time_major:
        assert nc == 1
        return pl.BlockSpec((q // s_blk, s_blk, width), lambda *a: (0, chunk_index(a), 0))
    return pl.BlockSpec((q, width), lambda *a: (chunk_index(a), 0))


def _project(h_ref, g_ref, w_refs):
    u = _rms(_chunk_rows(h_ref), g_ref[...], NORM_EPS).astype(BF16)
    return [jnp.dot(u, w_ref[...], preferred_element_type=F32) for w_ref in w_refs]


def _conv_pad_rows(s_blk):
    need = (SSM_CONV - 1) * s_blk
    return -(-need // V7X_SUBLANES) * V7X_SUBLANES


def _ssd_kernel(hc_ref, *refs, s_blk, nc, chunks_per_step):
    hn_refs, refs = refs[:chunks_per_step], refs[chunks_per_step:]
    n_in = 11
    y_ref = refs[n_in]
    for k, hn_ref in enumerate(hn_refs):
        if chunks_per_step == 1:
            y_view = y_ref
        else:
            q = y_ref.shape[0] // chunks_per_step
            y_view = y_ref.at[pl.ds(k * q, q)]
        _ssd_chunk(hc_ref, hn_ref, *refs[:n_in], y_view, *refs[n_in + 1:], s_blk=s_blk, nc=nc,
                   chunks_per_step=chunks_per_step, first_of_step=(k == 0))


def _ssd_chunk(hc_ref, hn_ref, gmix_ref, w_ref, h0_ref, conv0_ref, cw_ref, cb_ref, dtb_ref, alog_ref, dexp_ref,
               gn_ref, e_ref, y_ref, hout_ref, convout_ref, xp_scr, zdt_scr, next_scr, *, s_blk, nc, chunks_per_step,
               first_of_step):
    s = pl.program_id(0)
    c = pl.program_id(1)
    pad = _conv_pad_rows(s_blk)
    q = xp_scr.shape[0] - pad
    carry = (SSM_CONV - 1) * s_blk
    steps = q // s_blk
    hg = SSM_HEADS // SSM_GROUPS
    gw = hg * SSM_HEAD_DIM
    h_src = h0_ref if nc == 1 else hout_ref

    if first_of_step:
        @pl.when(jnp.logical_and(s == 0, c == 0))
        def _first():
            next_scr[...] = _project(hc_ref, gmix_ref, [w_ref])[0]

        @pl.when(c == 0)
        def _init():
            xp_scr[0:pad, :] = jnp.zeros((pad, SSM_CONV_DIM), F32)
            xp_scr[pad - carry:pad, :] = conv0_ref[0]
            if nc > 1:
                hout_ref[...] = h0_ref[...]

    xp_scr[pad:pad + q, :] = next_scr[:, SSDP_XBC:SSDP_DT]
    zdt_scr[:, 0:SSM_D_INNER] = next_scr[:, SSDP_Z:SSDP_XBC]
    zdt_scr[:, SSM_D_INNER:SSM_D_INNER + DT_PAD] = next_scr[:, SSDP_DT:SSDP_DIM]
    next_scr[...] = _project(hn_ref, gmix_ref, [w_ref])[0]

    z_gate = _silu(zdt_scr[:, 0:SSM_D_INNER])

    xall = xp_scr[...]
    conv = cb_ref[...] + xall[pad:pad + q, :] * cw_ref[SSM_CONV - 1:SSM_CONV, :]
    for j in range(1, SSM_CONV):
        d = j * s_blk
        if d % V7X_SUBLANES == 0:
            back = xall[pad - d:pad - d + q, :]
        else:
            back = pltpu.roll(xall, d, 0)[pad:pad + q, :]
        conv = conv + back * cw_ref[SSM_CONV - 1 - j:SSM_CONV - j, :]
    @pl.when(c == nc // chunks_per_step - 1)
    def _():
        convout_ref[0] = xp_scr[pad + q - carry:pad + q, :]

    if nc > 1:
        xp_scr[0:pad, :] = xp_scr[q:q + pad, :]
    dt_rows = zdt_scr[:, SSM_D_INNER:SSM_D_INNER + DT_PAD]
    xc = _silu(conv)
    xs = xc[:, 0:SSM_D_INNER]
    bm = xc[:, SSM_D_INNER:SSM_D_INNER + SSM_GROUPS * SSM_STATE]
    cm = xc[:, SSM_D_INNER + SSM_GROUPS * SSM_STATE:SSM_CONV_DIM]

    dt = _softplus(dt_rows + dtb_ref[...])
    a = -jnp.exp(alog_ref[...])
    visible = _pair_mask(q, s_blk)
    log_step = dt * (a * LOG2_E)
    la = _dot_split(jnp.where(visible, 1.0, 0.0).astype(BF16), log_step, 3)
    log_dt = jnp.maximum(jnp.log2(dt), LOG_DT_FLOOR)
    lsd = la - log_dt
    eye_r = lax.broadcasted_iota(jnp.int32, (DT_PAD, DT_PAD), 0)
    eye_c = lax.broadcasted_iota(jnp.int32, (DT_PAD, DT_PAD), 1)
    eye = jnp.where(eye_r == eye_c, 1.0, 0.0).astype(BF16)
    seen_by = _pair_mask(q, s_blk, transposed=True)
    la_t = _dot_split_tn(log_step, jnp.where(seen_by, 1.0, 0.0).astype(BF16), 3)
    lsd_t = la_t - _transpose_split(eye, log_dt, 3)
    la_last = la[q - s_blk:q, :]
    la_last_rows = la_last if s_blk == 1 else jnp.concatenate([la_last] * steps, axis=0)
    dec_t = jnp.exp2(la_t[:, q - s_blk:q])

    lane = lax.broadcasted_iota(jnp.int32, (q, 2 * SSM_HEAD_DIM), 1)
    first_head = lane < SSM_HEAD_DIM
    y_parts = []
    for g in range(SSM_GROUPS):
        cg = cm[:, g * SSM_STATE:(g + 1) * SSM_STATE]
        bg = bm[:, g * SSM_STATE:(g + 1) * SSM_STATE]
        cbg = _dot_nt(cg, bg)
        for pair in range(hg // 2):
            h0 = g * hg + 2 * pair
            ws = []
            for h in (h0, h0 + 1):
                seg = la[:, h:h + 1] - lsd_t[h:h + 1, :]
                ws.append(cbg * jnp.exp2(jnp.where(visible, seg, -jnp.inf)))
            xpair = xs[:, h0 * SSM_HEAD_DIM:(h0 + 2) * SSM_HEAD_DIM]
            rhs = jnp.concatenate([jnp.where(first_head, xpair, 0.0), jnp.where(first_head, 0.0, xpair)], axis=0)
            y_parts.append(_dot(jnp.concatenate(ws, axis=1), rhs))
    y = jnp.concatenate(y_parts, axis=1)

    _, seq_b = _row_time_seq((q, SSM_STATE), s_blk)
    y_state_parts = []
    for g in range(SSM_GROUPS):
        cg = cm[:, g * SSM_STATE:(g + 1) * SSM_STATE]
        if s_blk == 1:
            c_seqs, h_seqs = cg, h_src[0, g * gw:(g + 1) * gw, :]
        else:
            c_seqs = jnp.concatenate([jnp.where(seq_b == b, cg, 0.0) for b in range(s_blk)], axis=1)
            h_seqs = jnp.concatenate([h_src[b, g * gw:(g + 1) * gw, :].astype(BF16) for b in range(s_blk)], axis=1)
        y_state_parts.append(_dot_nt(c_seqs, h_seqs))
    y_state = jnp.concatenate(y_state_parts, axis=1)

    e = e_ref[...]
    y = y + y_state * _dot_split_lhs(jnp.exp2(la), e, EXPAND_PIECES) + xs * dexp_ref[...]
    y = y * z_gate
    y = _rms(y, gn_ref[...], GATED_NORM_EPS)
    y_ref[...] = y.reshape(y_ref.shape).astype(y_ref.dtype)

    tail = jnp.exp2(la_last_rows - lsd)
    xt = (xs * _dot_split_lhs(tail, e, EXPAND_PIECES)).astype(BF16)
    dec =[jnp.broadcast_to(dec_t[:, b:b + 1], (DT_PAD, SSM_STATE)) for b in range(s_blk)]
    for g in range(SSM_GROUPS):
        bg = bm[:, g * SSM_STATE:(g + 1) * SSM_STATE]
        bg_seqs = bg if s_blk == 1 else jnp.concatenate([jnp.where(seq_b == b, bg, 0.0) for b in range(s_blk)], axis=1)
        upd = _dot_tn(xt[:, g * gw:(g + 1) * gw], bg_seqs)
        for b in range(s_blk):
            for hh in range(hg):
                h = g * hg + hh
                r0 = h * SSM_HEAD_DIM
                hout_ref[b, r0:r0 + SSM_HEAD_DIM, :] = (
                    h_src[b, r0:r0 + SSM_HEAD_DIM, :] * dec[b][h:h + 1, :]
                    + upd[hh * SSM_HEAD_DIM:(hh + 1) * SSM_HEAD_DIM, b * SSM_STATE:(b + 1) * SSM_STATE])


def _ssd(h, gmix, w, h0, conv0, cw, cb, dtb, alog, dexp, gn, e, *, nblk, nc, q, s_blk, time_major):
    rows = SSM_HEADS * SSM_HEAD_DIM
    carry = (SSM_CONV - 1) * s_blk
    per_block0 = h0.shape[0] == nblk * s_blk
    assert per_block0 or (h0.shape[0] == s_blk and conv0.shape[0] == 1)
    cps = SSD_CHUNKS_PER_STEP if (nc % SSD_CHUNKS_PER_STEP == 0 and not time_major) else 1
    ncs = nc // cps
    spec = functools.partial(_rows_spec, q, s_blk, nblk, nc, time_major=time_major, chunks_per_step=cps)
    if time_major:
        y_spec = spec(SSM_D_INNER)
        y_shape = (q // s_blk, nblk * s_blk, SSM_D_INNER)
    else:
        y_spec = pl.BlockSpec((cps * q, SSM_D_INNER), lambda s, c: (s * ncs + c, 0))
        y_shape = (nblk * nc * q, SSM_D_INNER)
    return pl.pallas_call(
        functools.partial(_ssd_kernel, s_blk=s_blk, nc=nc, chunks_per_step=cps),
        grid=(nblk, ncs),
        in_specs=[spec(D_MODEL, ahead=k) for k in range(cps + 1)] + [
            _const_spec((1, D_MODEL)),
            _const_spec((D_MODEL, SSDP_DIM)),
            pl.BlockSpec((s_blk, rows, SSM_STATE), lambda s, c: (s if per_block0 else 0, 0, 0)),
            pl.BlockSpec((1, carry, SSM_CONV_DIM), lambda s, c: (s if per_block0 else 0, 0, 0)),
            _const_spec((SSM_CONV, SSM_CONV_DIM)),
            _const_spec((1, SSM_CONV_DIM)),
            _const_spec((1, DT_PAD)),
            _const_spec((1, DT_PAD)),
            _const_spec((1, SSM_D_INNER)),
            _const_spec((1, SSM_D_INNER)),
            _const_spec((EXPAND_PIECES * DT_PAD, SSM_D_INNER)),
        ],
        out_specs=[
            y_spec,
            pl.BlockSpec((s_blk, rows, SSM_STATE), lambda s, c: (s, 0, 0)),
            pl.BlockSpec((1, carry, SSM_CONV_DIM), lambda s, c: (s, 0, 0)),
        ],
        out_shape=[
            jax.ShapeDtypeStruct(y_shape, BF16),
            jax.ShapeDtypeStruct((nblk * s_blk, rows, SSM_STATE), F32),
            jax.ShapeDtypeStruct((nblk, carry, SSM_CONV_DIM), F32),
        ],
        scratch_shapes=[
            pltpu.VMEM((_conv_pad_rows(s_blk) + q, SSM_CONV_DIM), F32),
            pltpu.VMEM((q, SSM_D_INNER + DT_PAD), F32),
            pltpu.VMEM((q, SSDP_DIM), F32),
        ],
        compiler_params=_params(2),
        name="ssd",
    )(*([h] * (cps + 1)), gmix, w, h0, conv0, cw, cb, dtb, alog, dexp, gn, e)


def _ret_log_decay(h):
    return float(np.log1p(-np.exp2(-5.0 - h)))


def _ret_project(h_ref, gmix_ref, w_refs, dst):
    pq, pk, pv, pg = _project(h_ref, gmix_ref, w_refs)
    dst[0][...] = pq
    dst[1][...] = pk
    dst[2][...] = pv.astype(BF16)
    dst[3][...] = pg


def _ret_kernel(*refs, s_blk, nc, n_heads, chunks_per_step):
    io_refs, sets = refs[:-8], (refs[-8:-4], refs[-4:])
    hc_ref, hn_refs, io_refs = io_refs[0], io_refs[1:1 + chunks_per_step], io_refs[1 + chunks_per_step:]
    gmix_ref = io_refs[0]
    w_refs = io_refs[1:5]
    cos_ref, sin_ref = io_refs[5:7]
    tables = io_refs[7:10]
    s0_ref, gn_ref, y_ref, sout_ref = io_refs[10:14]
    s = pl.program_id(1)
    c = pl.program_id(2)
    q = cos_ref.shape[0] // chunks_per_step

    @pl.when(jnp.logical_and(s == 0, c == 0))
    def _first():
        _ret_project(hc_ref, gmix_ref, w_refs, sets[0])

    if nc > 1:
        @pl.when(c == 0)
        def _init():
            sout_ref[...] = s0_ref[...]

    def chunk(k, hn_ref, cur_set, next_set):
        rows = pl.ds(k * q, q)
        y_view = y_ref if chunks_per_step == 1 else y_ref.at[rows]
        _ret_step(hn_ref, gmix_ref, *w_refs, cos_ref.at[rows], sin_ref.at[rows], *tables, s0_ref, gn_ref, y_view,
                  sout_ref, cur_set, next_set, s_blk=s_blk, nc=nc, n_heads=n_heads)

    if chunks_per_step % 2 == 0:
        for k, hn_ref in enumerate(hn_refs):
            chunk(k, hn_ref, sets[k % 2], sets[(k + 1) % 2])
    else:
        assert chunks_per_step == 1
        parity = (s * nc + c) % 2

        @pl.when(parity == 0)
        def _even():
            chunk(0, hn_refs[0], sets[0], sets[1])

        @pl.when(parity == 1)
        def _odd():
            chunk(0, hn_refs[0], sets[1], sets[0])


def _ret_step(hn_ref, gmix_ref, wq_ref, wk_ref, wv_ref, wg_ref, cos_ref, sin_ref, dmat_ref, qdec_ref, kdec_ref,
              s0_ref, gn_ref, y_ref, sout_ref, cur_set, next_set, *, s_blk, nc, n_heads):
    q_scr, k_scr, v_scr, g_scr = cur_set
    hb = pl.program_id(0)
    q = cos_ref.shape[0]
    steps = q // s_blk
    half = RET_QK_DIM // 2
    s_src = s0_ref if nc == 1 else sout_ref

    _ret_project(hn_ref, gmix_ref, [wq_ref, wk_ref, wv_ref, wg_ref], next_set)

    qq = q_scr[...]
    kk = k_scr[...]
    vv = v_scr[...]
    rg = g_scr[...]
    cos = cos_ref[...]
    sin = sin_ref[...]
    _, seq_qk = _row_time_seq((q, RET_QK_DIM), s_blk)

    def rot(t, h):
        t1 = t[:, h * RET_QK_DIM:h * RET_QK_DIM + half]
        t2 = t[:, h * RET_QK_DIM + half:(h + 1) * RET_QK_DIM]
        return jnp.concatenate([t1 * cos - t2 * sin, t1 * sin + t2 * cos], axis=1)

    y_parts = []
    for h in range(n_heads):
        if n_heads == RET_HEADS:
            chunk_dec = math.exp(steps * _ret_log_decay(h))
        else:
            head = hb * n_heads + h
            chunk_dec = jnp.float32(math.exp(steps * _ret_log_decay(0)))
            for hh in range(1, RET_HEADS):
                chunk_dec = jnp.where(head == hh, jnp.float32(math.exp(steps * _ret_log_decay(hh))), chunk_dec)
        dmat = dmat_ref[h]
        q_dec = jnp.concatenate([qdec_ref[h]] * 2, axis=1)
        k_dec = jnp.concatenate([kdec_ref[h]] * 2, axis=1)
        qr = rot(qq, h)
        kr = rot(kk, h) * (RET_QK_DIM ** -0.5)
        vh = vv[:, h * RET_V_DIM:(h + 1) * RET_V_DIM]
        scores = _dot_nt(qr, kr) * dmat
        y = _dot(scores, vh)
        qd = qr * q_dec
        kd = kr * k_dec
        r0 = h * RET_QK_DIM
        for b in range(s_blk):
            qd_b = (qd if s_blk == 1 else jnp.where(seq_qk == b, qd, 0.0)).astype(BF16)
            kd_b = (kd if s_blk == 1 else jnp.where(seq_qk == b, kd, 0.0)).astype(BF16)
            sh = s_src[b, r0:r0 + RET_QK_DIM, :]
            y = y + _dot(qd_b, sh)
            sout_ref[b, r0:r0 + RET_QK_DIM, :] = sh * chunk_dec + _dot_tn(kd_b, vh)
        g = gn_ref[:, h * RET_V_DIM:(h + 1) * RET_V_DIM]
        y = _rms(y, g, NORM_EPS) * _silu(rg[:, h * RET_V_DIM:(h + 1) * RET_V_DIM])
        y_parts.append(y)
    y = y_parts[0] if n_heads == 1 else jnp.concatenate(y_parts, axis=1)
    y_ref[...] = y.reshape(y_ref.shape).astype(y_ref.dtype)


def _ret_decay_tables(q, s_blk):
    t = np.arange(q) // s_blk
    seq = np.arange(q) % s_blk
    steps = q // s_blk
    rel = t[:, None] - t[None, :]
    visible = (rel >= 0) & (seq[:, None] == seq[None, :])
    dmat, qdec, kdec = [], [], []
    for h in range(RET_HEADS):
        lg = _ret_log_decay(h)
        dmat.append(np.where(visible, np.exp(np.maximum(rel, 0) * lg), 0.0))
        qdec.append(np.broadcast_to(np.exp((t + 1.0) * lg)[:, None], (q, V7X_LANES)))
        kdec.append(np.broadcast_to(np.exp((steps - 1.0 - t) * lg)[:, None], (q, V7X_LANES)))
    return tuple(jnp.asarray(np.stack(a), F32) for a in (dmat, qdec, kdec))


def _ret(h, gmix, wq, wk, wv, wg, cos, sin, s0, gn, *, nblk, nc, q, s_blk, n_heads, time_major):
    dmat, qdec, kdec = _ret_decay_tables(q, s_blk)
    nhb = RET_HEADS // n_heads
    qk_w = n_heads * RET_QK_DIM
    v_w = n_heads * RET_V_DIM
    per_block0 = s0.shape[0] == nblk * s_blk
    assert per_block0 or s0.shape[0] == s_blk
    cps = RET_CHUNKS_PER_STEP if (nc % RET_CHUNKS_PER_STEP == 0 and not time_major) else 1
    ncs = nc // cps
    rows = functools.partial(_rows_spec, q, s_blk, nblk, nc, D_MODEL, time_major, lead_axes=1, chunks_per_step=cps)

    def head_cols(nrows, width):
        if nhb == 1:
            return _const_spec((nrows, width))
        return pl.BlockSpec((nrows, width), lambda hb, s, c: (0, hb))

    if time_major:
        y_spec = pl.BlockSpec((q // s_blk, s_blk, v_w), lambda hb, s, c: (0, s, hb))
        y_shape = (q // s_blk, nblk * s_blk, RET_V)
    else:
        y_spec = pl.BlockSpec((cps * q, v_w), lambda hb, s, c: (s * ncs + c, hb))
        y_shape = (nblk * nc * q, RET_V)
    return pl.pallas_call(
        functools.partial(_ret_kernel, s_blk=s_blk, nc=nc, n_heads=n_heads, chunks_per_step=cps),
        grid=(nhb, nblk, ncs),
        in_specs=[rows(ahead=k) for k in range(cps + 1)] + [
            _const_spec((1, D_MODEL)),
            head_cols(D_MODEL, qk_w),
            head_cols(D_MODEL, qk_w),
            head_cols(D_MODEL, v_w),
            head_cols(D_MODEL, v_w),
            pl.BlockSpec((cps * q, V7X_LANES), lambda hb, s, c: (c, 0)),
            pl.BlockSpec((cps * q, V7X_LANES), lambda hb, s, c: (c, 0)),
            pl.BlockSpec((n_heads, q, q), lambda hb, s, c: (hb, 0, 0)),
            pl.BlockSpec((n_heads, q, V7X_LANES), lambda hb, s, c: (hb, 0, 0)),
            pl.BlockSpec((n_heads, q, V7X_LANES), lambda hb, s, c: (hb, 0, 0)),
            pl.BlockSpec((s_blk, qk_w, RET_V_DIM), lambda hb, s, c: (s if per_block0 else 0, hb, 0)),
            head_cols(1, v_w),
        ],
        out_specs=[
            y_spec,
            pl.BlockSpec((s_blk, qk_w, RET_V_DIM), lambda hb, s, c: (s, hb, 0)),
        ],
        out_shape=[
            jax.ShapeDtypeStruct(y_shape, BF16),
            jax.ShapeDtypeStruct((nblk * s_blk, RET_HEADS * RET_QK_DIM, RET_V_DIM), F32),
        ],
        scratch_shapes=[
            pltpu.VMEM((q, qk_w), F32),
            pltpu.VMEM((q, qk_w), F32),
            pltpu.VMEM((q, v_w), BF16),
            pltpu.VMEM((q, v_w), F32),
        ] * 2,
        compiler_params=_params(3),
        name="ret",
    )(*([h] * (cps + 1)), gmix, wq, wk, wv, wg, cos, sin, dmat, qdec, kdec, s0, gn)


def _merge_kernel(h_ref, ys_ref, yr_ref, g_ref, wg_ref, wbs_ref, wbr_ref, wo_ref, o_ref):
    h = h_ref[...]
    u = _rms(h, g_ref[...], NORM_EPS).astype(BF16)
    gates = jnp.dot(u, wg_ref[...], preferred_element_type=F32)
    branch_ssm = jnp.dot(ys_ref[...], wbs_ref[...], preferred_element_type=F32)
    branch_ret = jnp.dot(yr_ref[...], wbr_ref[...], preferred_element_type=F32)
    merged = _sigmoid(gates[:, 0:D_MODEL]) * branch_ssm + _sigmoid(gates[:, D_MODEL:2 * D_MODEL]) * branch_ret
    o_ref[...] = h + jnp.dot(merged.astype(BF16), wo_ref[...], preferred_element_type=F32)


def _merge(h, ys, yr, g, wg, wbs, wbr, wo, *, tm):
    t = h.shape[0]
    assert t % tm == 0
    return pl.pallas_call(
        _merge_kernel,
        grid=(t // tm,),
        in_specs=[
            pl.BlockSpec((tm, D_MODEL), lambda i: (i, 0)),
            pl.BlockSpec((tm, SSM_D_INNER), lambda i: (i, 0)),
            pl.BlockSpec((tm, RET_V), lambda i: (i, 0)),
            _const_spec((1, D_MODEL)),
            _const_spec((D_MODEL, 2 * D_MODEL)),
            _const_spec((SSM_D_INNER, D_MODEL)),
            _const_spec((RET_V, D_MODEL)),
            _const_spec((D_MODEL, D_MODEL)),
        ],
        out_specs=pl.BlockSpec((tm, D_MODEL), lambda i: (i, 0)),
        out_shape=jax.ShapeDtypeStruct((t, D_MODEL), F32),
        compiler_params=_params(1),
        name="merge",
    )(h, ys, yr, g, wg, wbs, wbr, wo)


FFN_TILE = 512
SAMPLE_SEQS_PER_STEP = V7X_SUBLANES
SSD_CHUNKS_PER_STEP = 4
RET_CHUNKS_PER_STEP = 4
SAMPLE_RET_HEADS_PER_STEP = 2


def _layer(x2, pos0, ssm0, conv0, ret0, w, final_gain, *, nblk, nc, q, s_blk, ret_heads, time_major):
    t = x2.shape[0]
    tm_ffn = min(FFN_TILE, t)
    steps = q // s_blk
    nseq = nblk * s_blk

    h = _ffn(x2, w["norm_ffn1"], w["ffn1_w1"], w["ffn1_w3"], w["ffn1_w2"], final_gain, final_norm=False, tm=tm_ffn)
    h_rows = h.reshape(steps, nseq, D_MODEL) if time_major else h

    y_ssm, ssm_new, conv_new = _ssd(
        h_rows, w["norm_mix"], w["w_ssd"], ssm0, conv0, w["conv_w"], w["conv_b"], w["dt_bias"], w["a_log"],
        w["d_exp"], w["ssm_norm"], w["head_expand"], nblk=nblk, nc=nc, q=q, s_blk=s_blk, time_major=time_major)
    cos, sin = _rope_tables(w["rope_inv"], nc * q, pos0, s_blk)
    y_ret, ret_new = _ret(h_rows, w["norm_mix"], w["w_q"], w["w_k"], w["w_v"], w["w_rg"], cos, sin, ret0,
                          w["ret_norm"], nblk=nblk, nc=nc, q=q, s_blk=s_blk, n_heads=ret_heads,
                          time_major=time_major)

    h = _merge(h, y_ssm.reshape(t, SSM_D_INNER), y_ret.reshape(t, RET_V), w["norm_mix"], w["w_gates"],
               w["w_branch_ssm"], w["w_branch_ret"], w["w_out"], tm=tm_ffn)
    y = _ffn(h, w["norm_ffn2"], w["ffn2_w1"], w["ffn2_w3"], w["ffn2_w2"], final_gain, final_norm=True, tm=tm_ffn)
    return y, ssm_new, conv_new, ret_new


def _prompt_layer(x, w, final_gain):
    nseq, seq_len, _ = x.shape
    q = math.gcd(seq_len, CHUNK)
    y, ssm_new, conv_new, ret_new = _layer(
        x.reshape(nseq * seq_len, D_MODEL), 0.0,
        jnp.zeros((1, SSM_HEADS * SSM_HEAD_DIM, SSM_STATE), F32),
        jnp.zeros((1, SSM_CONV - 1, SSM_CONV_DIM), F32),
        jnp.zeros((1, RET_HEADS * RET_QK_DIM, RET_V_DIM), F32),
        w, final_gain, nblk=nseq, nc=seq_len // q, q=q, s_blk=1, ret_heads=RET_HEADS, time_major=False)
    return (y.reshape(nseq, seq_len, D_MODEL), ssm_new.reshape(nseq, SSM_HEADS, SSM_HEAD_DIM, SSM_STATE), conv_new,
            ret_new.reshape(nseq, RET_HEADS, RET_QK_DIM, RET_V_DIM))


def _sample_layer(x, ssm0, conv0, ret0, w, final_gain):
    nseq, seq_len, _ = x.shape
    s_blk = SAMPLE_SEQS_PER_STEP
    assert nseq % s_blk == 0 and seq_len <= CHUNK
    nblk = nseq // s_blk
    kc = SSM_CONV - 1
    x_tm = jnp.transpose(x, (1, 0, 2)).reshape(seq_len * nseq, D_MODEL)
    conv0_tm = jnp.transpose(conv0.reshape(nblk, s_blk, kc, SSM_CONV_DIM), (0, 2, 1, 3)).reshape(
        nblk, kc * s_blk, SSM_CONV_DIM)
    y, ssm_new, conv_new, ret_new = _layer(
        x_tm, float(PAST_LEN), ssm0.reshape(nseq, SSM_HEADS * SSM_HEAD_DIM, SSM_STATE), conv0_tm,
        ret0.reshape(nseq, RET_HEADS * RET_QK_DIM, RET_V_DIM), w, final_gain,
        nblk=nblk, nc=1, q=s_blk * seq_len, s_blk=s_blk, ret_heads=SAMPLE_RET_HEADS_PER_STEP, time_major=True)
    conv_new = jnp.transpose(conv_new.reshape(nblk, kc, s_blk, SSM_CONV_DIM), (0, 2, 1, 3)).reshape(
        nseq, kc, SSM_CONV_DIM)
    return (jnp.transpose(y.reshape(seq_len, nseq, D_MODEL), (1, 0, 2)),
            ssm_new.reshape(nseq, SSM_HEADS, SSM_HEAD_DIM, SSM_STATE), conv_new,
            ret_new.reshape(nseq, RET_HEADS, RET_QK_DIM, RET_V_DIM))


def _prep_weights(norm_ffn1, ffn1_w1, ffn1_w3, ffn1_w2, norm_mix, w_in, conv_w, conv_b, dt_bias, a_log, ssm_d,
                  ssm_norm, ret_norm, w_branch_ssm, w_branch_ret, w_out, norm_ffn2, ffn2_w1, ffn2_w3, ffn2_w2):
    o_z = 0
    o_xbc = o_z + SSM_D_INNER
    o_dt = o_xbc + SSM_CONV_DIM
    o_q = o_dt + SSM_HEADS
    o_k = o_q + RET_QK
    o_v = o_k + RET_QK
    o_rg = o_v + RET_V
    o_ga = o_rg + RET_V
    o_end = o_ga + 2 * D_MODEL
    assert o_end == w_in.shape[1]

    w_in_bf = w_in.astype(BF16)

    def cols(a, b):
        return w_in_bf[:, a:b]

    assert (o_z, o_xbc, o_dt) == (SSDP_Z, SSDP_XBC, SSDP_DT) and SSDP_DIM <= o_end
    half = RET_QK_DIM // 2
    head_of_channel = jnp.arange(SSM_D_INNER, dtype=jnp.int32) // SSM_HEAD_DIM
    return {
        "norm_ffn1": norm_ffn1.reshape(1, -1), "ffn1_w1": ffn1_w1.astype(BF16), "ffn1_w3": ffn1_w3.astype(BF16),
        "ffn1_w2": ffn1_w2.astype(BF16),
        "norm_mix": norm_mix.reshape(1, -1), "w_ssd": w_in_bf,
        "w_q": cols(o_q, o_k), "w_k": cols(o_k, o_v), "w_v": cols(o_v, o_rg), "w_rg": cols(o_rg, o_ga),
        "w_gates": cols(o_ga, o_end),
        "conv_w": conv_w, "conv_b": conv_b.reshape(1, -1),
        "dt_bias": jnp.pad(dt_bias, (0, DT_PAD - SSM_HEADS)).reshape(1, -1),
        "a_log": jnp.pad(a_log, (0, DT_PAD - SSM_HEADS)).reshape(1, -1),
        "d_exp": jnp.repeat(ssm_d, SSM_HEAD_DIM).reshape(1, -1),
        "ssm_norm": ssm_norm.reshape(1, -1), "ret_norm": ret_norm.reshape(1, -1),
        "head_expand": jnp.tile(
            (head_of_channel[None, :] == jnp.arange(DT_PAD, dtype=jnp.int32)[:, None]).astype(BF16),
            (EXPAND_PIECES, 1)),
        "rope_inv": (ROPE_BASE ** (-jnp.arange(half, dtype=F32) / half)).reshape(1, half),
        "w_branch_ssm": w_branch_ssm.astype(BF16), "w_branch_ret": w_branch_ret.astype(BF16),
        "w_out": w_out.astype(BF16),
        "norm_ffn2": norm_ffn2.reshape(1, -1), "ffn2_w1": ffn2_w1.astype(BF16), "ffn2_w3": ffn2_w3.astype(BF16),
        "ffn2_w2": ffn2_w2.astype(BF16),
    }


def kernel(x_prompt, x_sample, state_ssm, state_conv, state_ret, norm_ffn1, ffn1_w1, ffn1_w3, ffn1_w2, norm_mix, w_in,
           conv_w, conv_b, dt_bias, a_log, ssm_d, ssm_norm, ret_norm, w_branch_ssm, w_branch_ret, w_out, norm_ffn2,
           ffn2_w1, ffn2_w3, ffn2_w2, norm_final):
    depth = norm_ffn1.shape[0]
    assert depth == 1, "the final RMSNorm is fused into the last layer's second FFN"
    w = _prep_weights(norm_ffn1[0], ffn1_w1[0], ffn1_w3[0], ffn1_w2[0], norm_mix[0], w_in[0], conv_w[0], conv_b[0],
                      dt_bias[0], a_log[0], ssm_d[0], ssm_norm[0], ret_norm[0], w_branch_ssm[0], w_branch_ret[0],
                      w_out[0], norm_ffn2[0], ffn2_w1[0], ffn2_w3[0], ffn2_w2[0])
    final_gain = norm_final.reshape(1, -1)
    yp, ssm_p, conv_p, ret_p = _prompt_layer(x_prompt, w, final_gain)
    ys, ssm_s, conv_s, ret_s = _sample_layer(x_sample, state_ssm[0], state_conv[0], state_ret[0], w, final_gain)
    return (yp, ys, ssm_p[None], conv_p[None], ret_p[None], ssm_s[None], conv_s[None], ret_s[None])
```
